```python
import jax, jax.numpy as jnp
from jax import lax
import numpy as np

D_MODEL = 2048
BATCH = 2
SEQ = 4096
DEPTH = 1

CHUNK = 64
Q_BLOCK = 128
PLE_DIM = 256
NORM_EPS = 1e-6

RW_HEADS = 16
RW_HEAD_DIM = 64
RW_WIDTH = RW_HEADS * RW_HEAD_DIM
RW_DECAY_LORA = 64
RW_A_LORA = 64
RW_GATE_LORA = 160
RW_GN_EPS = 64e-5
RW_COLS = 3 * RW_WIDTH + RW_DECAY_LORA + RW_A_LORA + RW_GATE_LORA
RW_SPLITS = (RW_WIDTH, 2 * RW_WIDTH, 3 * RW_WIDTH, 3 * RW_WIDTH + RW_DECAY_LORA, 3 * RW_WIDTH + RW_DECAY_LORA + RW_A_LORA)

MLA_HEADS = 8
MLA_Q_RANK = 512
MLA_KV_RANK = 512
MLA_NOPE = 128
MLA_ROPE = 64
MLA_V = 128
MLA_COLS = MLA_Q_RANK + MLA_KV_RANK + MLA_ROPE
ROPE_BASE = 10000.0

GATE_COLS = 2 * D_MODEL
IN_COLS = RW_COLS + MLA_COLS + GATE_COLS

D_FF = 5632
CONV_W = 3

kernel_name = 'hybrid_rwkv7_mla_convffn_block'


def rms_norm(x, g):
    xf = x.astype(jnp.float32)
    y = xf * lax.rsqrt(jnp.mean(xf * xf, axis=-1, keepdims=True) + NORM_EPS)
    return (y * g.astype(jnp.float32)).astype(x.dtype)


def token_shift(z):
    return jnp.pad(z, ((0, 0), (1, 0), (0, 0)))[:, :-1]


def rope(x, cos, sin):
    x1, x2 = jnp.split(x, 2, axis=-1)
    return jnp.concatenate([x1 * cos - x2 * sin, x2 * cos + x1 * sin], axis=-1)


def rwkv7_scan(r, decay, k, v, kk, a):
    B, S, H, N = r.shape

    def step(state, inp):
        r_t, w_t, k_t, v_t, kk_t, a_t = inp
        sa = jnp.einsum('bhvk,bhk->bhv', state, -kk_t)
        state = (state * w_t[:, :, None, :]
                 + sa[..., None] * (kk_t * a_t)[:, :, None, :]
                 + v_t[..., None] * k_t[:, :, None, :])
        y_t = jnp.einsum('bhvk,bhk->bhv', state, r_t)
        return state, y_t

    xs = tuple(jnp.moveaxis(t.astype(jnp.float32), 1, 0) for t in (r, decay, k, v, kk, a))
    s0 = jnp.zeros((B, H, N, N), jnp.float32)
    _, y = lax.scan(step, s0, xs)
    return jnp.moveaxis(y, 0, 1)


def rwkv7_mixer(z, mu, w0, w2, a0, a2, g2, k_k, k_a, r_k, lnx_w, lnx_b):
    B, S, _ = z.shape
    z = z + (token_shift(z) - z) * mu
    r, k, v, wd, ad, gd = jnp.split(z, RW_SPLITS, axis=-1)
    w_log = -jax.nn.softplus(-(w0 + jnp.tanh(wd) @ w2).astype(jnp.float32)) - 0.5
    decay = jnp.exp(-jnp.exp(w_log))
    a = jax.nn.sigmoid(a0 + ad @ a2)
    g = jax.nn.sigmoid(gd) @ g2

    def heads(t):
        return t.reshape(B, S, RW_HEADS, RW_HEAD_DIM)

    kk = heads(k * k_k).astype(jnp.float32)
    kk = kk / jnp.maximum(jnp.sqrt(jnp.sum(kk * kk, axis=-1, keepdims=True)), 1e-12)
    k = k * (1 + (a - 1) * k_a)
    r_h, k_h, v_h, a_h = heads(r), heads(k), heads(v), heads(a)
    y = rwkv7_scan(r_h, heads(decay), k_h, v_h, kk, a_h)
    mean = jnp.mean(y, axis=-1, keepdims=True)
    var = jnp.mean(jnp.square(y - mean), axis=-1, keepdims=True)
    y = ((y - mean) * lax.rsqrt(var + RW_GN_EPS)).reshape(B, S, RW_WIDTH)
    y = (y * lnx_w.astype(jnp.float32) + lnx_b.astype(jnp.float32)).astype(z.dtype)
    bonus = (jnp.sum(r_h * k_h * r_k, axis=-1, keepdims=True) * v_h).reshape(B, S, RW_WIDTH)
    return (y + bonus) * g


def chunk_causal_attention(q, k, v):
    B, S, H, Dqk = q.shape
    Dv = v.shape[-1]
    nb = S // Q_BLOCK
    scale = Dqk ** -0.5
    kf = k.astype(jnp.float32)
    vf = v.astype(jnp.float32)
    key_chunk = jnp.arange(S) // CHUNK
    qb = q.reshape(B, nb, Q_BLOCK, H, Dqk).transpose(1, 0, 3, 2, 4)

    def one_block(args):
        q_blk, i = args
        s = jnp.einsum('bhqd,bkhd->bhqk', q_blk.astype(jnp.float32), kf) * scale
        q_chunk = (i * Q_BLOCK + jnp.arange(Q_BLOCK)) // CHUNK
        mask = key_chunk[None, :] <= q_chunk[:, None]
        s = jnp.where(mask, s, -1e30)
        pr = jax.nn.softmax(s, axis=-1)
        return jnp.einsum('bhqk,bkhd->bqhd', pr, vf)

    o = lax.map(one_block, (qb, jnp.arange(nb)))
    return o.transpose(1, 0, 2, 3, 4).reshape(B, S, H * Dv).astype(v.dtype)


def mla_mixer(z, cos, sin, q_norm, w_q_up, kv_norm, w_kv_up):
    B, S, _ = z.shape
    cq, ckv, k_pe = jnp.split(z, (MLA_Q_RANK, MLA_Q_RANK + MLA_KV_RANK), axis=-1)
    q = (rms_norm(cq, q_norm) @ w_q_up).reshape(B, S, MLA_HEADS, MLA_NOPE + MLA_ROPE)
    kv = (rms_norm(ckv, kv_norm) @ w_kv_up).reshape(B, S, MLA_HEADS, MLA_NOPE + MLA_V)
    q_nope, q_pe = jnp.split(q, (MLA_NOPE,), axis=-1)
    k_nope, v = jnp.split(kv, (MLA_NOPE,), axis=-1)
    q_pe = rope(q_pe, cos[:, :, None, :], sin[:, :, None, :])
    k_pe = rope(k_pe, cos, sin)[:, :, None, :]
    q = jnp.concatenate([q_nope, q_pe], axis=-1)
    k = jnp.concatenate([k_nope, jnp.broadcast_to(k_pe, (B, S, MLA_HEADS, MLA_ROPE))], axis=-1)
    return chunk_causal_attention(q, k, v)


def causal_dwconv(x, w, b):
    C = x.shape[-1]
    y = lax.conv_general_dilated(x, w[:, None, :], window_strides=(1,), padding=[(CONV_W - 1, 0)],
                                 dimension_numbers=('NWC', 'WIO', 'NWC'), feature_group_count=C)
    return y + b


def setup_inputs(seed: int = 0) -> dict:
    key = jax.random.key(seed)
    ks = iter(jax.random.split(key, 48))
    L, D = DEPTH, D_MODEL

    def nrm(shape, fan_in, s=1.0):
        return jax.random.normal(next(ks), shape, jnp.float32) * (s * fan_in ** -0.5)

    def gain(shape):
        return 1.0 + 0.02 * jax.random.normal(next(ks), shape, jnp.float32)

    def small(shape, s):
        return s * jax.random.normal(next(ks), shape, jnp.float32)

    def unif(shape, lo, hi):
        return jax.random.uniform(next(ks), shape, jnp.float32, lo, hi)

    x = jax.random.normal(next(ks), (BATCH, SEQ, D), jnp.float32)
    p = jax.random.normal(next(ks), (L, BATCH, SEQ, PLE_DIM), jnp.float32)
    start = jax.random.randint(next(ks), (BATCH, 1), 0, 4096, jnp.int32)
    positions = start + jnp.arange(SEQ, dtype=jnp.int32)[None, :]
    return {
        'x': x,
        'p': p,
        'positions': positions,
        'pre_mix_norm': gain((L, D)),
        'w_in': nrm((L, D, IN_COLS), D),
        'rw_mu': unif((L, RW_COLS), 0.0, 1.0),
        'rw_w0': unif((L, RW_WIDTH), -6.0, 1.0),
        'rw_w2': nrm((L, RW_DECAY_LORA, RW_WIDTH), RW_DECAY_LORA, 0.1),
        'rw_a0': small((L, RW_WIDTH), 0.5),
        'rw_a2': nrm((L, RW_A_LORA, RW_WIDTH), RW_A_LORA, 0.5),
        'rw_g2': nrm((L, RW_GATE_LORA, RW_WIDTH), RW_GATE_LORA),
        'rw_k_k': 0.85 + small((L, RW_WIDTH), 0.05),
        'rw_k_a': 1.0 + small((L, RW_WIDTH), 0.05),
        'rw_r_k': small((L, RW_HEADS, RW_HEAD_DIM), 0.1),
        'rw_lnx_w': gain((L, RW_WIDTH)),
        'rw_lnx_b': small((L, RW_WIDTH), 0.02),
        'mla_q_norm': gain((L, MLA_Q_RANK)),
        'mla_w_q_up': nrm((L, MLA_Q_RANK, MLA_HEADS * (MLA_NOPE + MLA_ROPE)), MLA_Q_RANK),
        'mla_kv_norm': gain((L, MLA_KV_RANK)),
        'mla_w_kv_up': nrm((L, MLA_KV_RANK, MLA_HEADS * (MLA_NOPE + MLA_V)), MLA_KV_RANK),
        'w_branch_rw': nrm((L, RW_WIDTH, D), RW_WIDTH),
        'w_branch_mla': nrm((L, MLA_HEADS * MLA_V, D), MLA_HEADS * MLA_V),
        'w_out': nrm((L, D, D), D),
        'post_mix_norm': gain((L, D)),
        'pre_ffn_norm': gain((L, D)),
        'w_up': nrm((L, D, 2 * D_FF), D),
        'conv_w': nrm((L, CONV_W, 2 * D_FF), CONV_W),
        'conv_b': small((L, 2 * D_FF), 0.02),
        'w_down': nrm((L, D_FF, D), D_FF),
        'post_ffn_norm': gain((L, D)),
        'w_ple': nrm((L, PLE_DIM, D), PLE_DIM),
        'w_ple_gate': nrm((L, D, D), D),
        'ple_norm': gain((L, D)),
    }


def reference(x, p, positions, pre_mix_norm, w_in, rw_mu, rw_w0, rw_w2, rw_a0, rw_a2, rw_g2,
              rw_k_k, rw_k_a, rw_r_k, rw_lnx_w, rw_lnx_b, mla_q_norm, mla_w_q_up, mla_kv_norm,
              mla_w_kv_up, w_branch_rw, w_branch_mla, w_out, post_mix_norm, pre_ffn_norm, w_up,
              conv_w, conv_b, w_down, post_ffn_norm, w_ple, w_ple_gate, ple_norm):
    inv_freq = ROPE_BASE ** (-jnp.arange(0, MLA_ROPE, 2, dtype=jnp.float32) / MLA_ROPE)
    ang = positions.astype(jnp.float32)[..., None] * inv_freq
    cos = jnp.cos(ang).astype(x.dtype)
    sin = jnp.sin(ang).astype(x.dtype)
    h = x
    for i in range(DEPTH):
        u = rms_norm(h, pre_mix_norm[i])
        z = u @ w_in[i]
        z_rw, z_mla, z_gate = jnp.split(z, (RW_COLS, RW_COLS + MLA_COLS), axis=-1)
        y_rw = rwkv7_mixer(z_rw, rw_mu[i], rw_w0[i], rw_w2[i], rw_a0[i], rw_a2[i], rw_g2[i],
                           rw_k_k[i], rw_k_a[i], rw_r_k[i], rw_lnx_w[i], rw_lnx_b[i])
        y_mla = mla_mixer(z_mla, cos, sin, mla_q_norm[i], mla_w_q_up[i], mla_kv_norm[i], mla_w_kv_up[i])
        g_rw, g_mla = jnp.split(jax.nn.sigmoid(z_gate), 2, axis=-1)
        mix = (g_rw * (y_rw @ w_branch_rw[i]) + g_mla * (y_mla @ w_branch_mla[i])) @ w_out[i]
        h = h + rms_norm(mix, post_mix_norm[i])
        u = rms_norm(h, pre_ffn_norm[i])
        up = causal_dwconv(u @ w_up[i], conv_w[i], conv_b[i])
        gate, val = jnp.split(up, 2, axis=-1)
        f = (jax.nn.gelu(gate, approximate=True) * val) @ w_down[i]
        h = h + rms_norm(f, post_ffn_norm[i])
        e = p[i] @ w_ple[i]
        h = h + rms_norm(jax.nn.sigmoid(h @ w_ple_gate[i]) * e, ple_norm[i])
    return h
```

```python
import functools

import jax
import jax.numpy as jnp
from jax import lax
from jax.experimental import pallas as pl
from jax.experimental.pallas import tpu as pltpu

NORM_EPS = 1e-6
GN_EPS = 64e-5
CHUNK = 64
ROPE_BASE = 10000.0

RW_HEADS = 16
RW_N = 64
RW_WIDTH = RW_HEADS * RW_N
RW_L = 64
LANES = 128
RW_PAIRS_PER_STEP = 4

MLA_HEADS = 8
MLA_NOPE = 128
MLA_ROPE = 64
MLA_V = 128
MLA_RANK = 512

VMEM_LIMIT_BYTES = 56 * 1024 * 1024

BF16 = jnp.bfloat16
F32 = jnp.float32


def _params(*sem):
    return pltpu.CompilerParams(dimension_semantics=sem, vmem_limit_bytes=VMEM_LIMIT_BYTES)


def _bf(x):
    return x.astype(BF16)


def _mm(a, b):
    return jnp.dot(_bf(a), _bf(b), preferred_element_type=F32)


def _mm_nt(a, b):
    return lax.dot_general(_bf(a), _bf(b), (((1,), (1,)), ((), ())), preferred_element_type=F32)


def _rms(x, g):
    return x * lax.rsqrt(jnp.mean(x * x, axis=-1, keepdims=True) + NORM_EPS) * g


def _inproj_kernel(x_ref, g_ref, w_ref, o_ref, xn_ref):
    @pl.when(pl.program_id(1) == 0)
    def _():
        xn_ref[...] = _bf(_rms(x_ref[...], g_ref[...]))

    o_ref[...] = jnp.dot(xn_ref[...], w_ref[...], preferred_element_type=F32)


def _in_proj(x2, g, w, tm=1024, tn=512):
    T, D = x2.shape
    N = w.shape[1]
    return pl.pallas_call(
        _inproj_kernel,
        grid=(T // tm, N // tn),
        in_specs=[
            pl.BlockSpec((tm, D), lambda i, j: (i, 0)),
            pl.BlockSpec((1, D), lambda i, j: (0, 0)),
            pl.BlockSpec((D, tn), lambda i, j: (0, j)),
        ],
        out_specs=pl.BlockSpec((tm, tn), lambda i, j: (i, j)),
        out_shape=jax.ShapeDtypeStruct((T, N), F32),
        scratch_shapes=[pltpu.VMEM((tm, D), BF16)],
        compiler_params=_params("parallel", "arbitrary"),
        name="in_proj",
    )(x2, g, w)


def _split3(x):
    hi = _bf(x)
    r1 = x - hi.astype(F32)
    mid = _bf(r1)
    lo = _bf(r1 - mid.astype(F32))
    return hi, mid, lo


def _rwkv_kernel(zr_ref, zk_ref, zv_ref, zl_ref, mur_ref, muk_ref, muv_ref, mul_ref,
                 w0_ref, a0_ref, kk_ref, ka_ref, rk_ref, lnw_ref, lnb_ref,
                 w2_ref, a2_ref, g2_ref, o_ref,
                 st_ref, pr_ref, pk_ref, pv_ref, plr_ref, *, pairs):
    L = RW_L
    c = pl.program_id(2)

    @pl.when(c == 0)
    def _():
        st_ref[...] = jnp.zeros_like(st_ref)
        pr_ref[...] = jnp.zeros_like(pr_ref)
        pk_ref[...] = jnp.zeros_like(pk_ref)
        pv_ref[...] = jnp.zeros_like(pv_ref)
        plr_ref[...] = jnp.zeros_like(plr_ref)

    def shift_lerp(z_ref, prev_ref, mu_ref):
        z = z_ref[...]
        rolled = pltpu.roll(z, 1, axis=0)
        row = lax.broadcasted_iota(jnp.int32, z.shape, 0)
        zs = jnp.where(row == 0, prev_ref[...], rolled)
        prev_ref[...] = z[L - 1:L, :]
        return z + (zs - z) * mu_ref[...]

    r_all = shift_lerp(zr_ref, pr_ref, mur_ref)
    k_all = shift_lerp(zk_ref, pk_ref, muk_ref)
    v_all = shift_lerp(zv_ref, pv_ref, muv_ref)
    lo_all = shift_lerp(zl_ref, plr_ref, mul_ref)
    wd = jnp.tanh(lo_all[:, 0:64])
    ad = lo_all[:, 64:128]
    gd = jax.nn.sigmoid(lo_all[:, 128:384])

    lane = lax.broadcasted_iota(jnp.int32, (1, LANES), 1)
    m0 = lane < RW_N
    row_l = lax.broadcasted_iota(jnp.int32, (L, LANES), 0)
    col_l = lax.broadcasted_iota(jnp.int32, (L, LANES), 1) & (RW_N - 1)
    strict = col_l < row_l
    incl = col_l <= row_l
    eye_pair = (col_l == row_l).astype(F32)
    row_s = lax.broadcasted_iota(jnp.int32, (LANES, LANES), 0)
    col_s = lax.broadcasted_iota(jnp.int32, (LANES, LANES), 1)
    same_head = (row_s < RW_N) == (col_s < RW_N)
    diag_s = row_s == col_s
    tri = (lax.broadcasted_iota(jnp.int32, (L, L), 1) <= lax.broadcasted_iota(jnp.int32, (L, L), 0)).astype(BF16)
    ones_bd = same_head.astype(BF16)

    def stack(x):
        return jnp.concatenate([jnp.where(m0, x, 0.0), jnp.where(m0, 0.0, x)], axis=0)

    for p in range(pairs):
        sl = slice(p * LANES, (p + 1) * LANES)
        r = r_all[:, sl]
        k = k_all[:, sl]
        v = v_all[:, sl]
        xw = w0_ref[:, sl] + _mm(wd, w2_ref[:, sl])
        logw = -0.6065306597126334 * jax.nn.sigmoid(xw)
        gate = jax.nn.sigmoid(a0_ref[:, sl] + _mm(ad, a2_ref[:, sl]))
        g_out = _mm(gd, g2_ref[:, sl])
        kkv = k * kk_ref[:, sl]
        ss = jnp.dot(_bf(kkv * kkv), ones_bd, preferred_element_type=F32)
        kkn = kkv / jnp.maximum(jnp.sqrt(ss), 1e-12)
        k2 = k * (1.0 + (gate - 1.0) * ka_ref[:, sl])
        hi, mid, lo = _split3(logw)
        cum = (jnp.dot(tri, hi, preferred_element_type=F32) + jnp.dot(tri, mid, preferred_element_type=F32)
               + jnp.dot(tri, lo, preferred_element_type=F32))
        cum_l = cum[L - 1:L, :]
        e_pos = jnp.exp(cum)
        e_neg = jnp.exp(-cum)
        e_end = jnp.exp(cum_l - cum)
        kka = kkn * gate
        rt = r * e_pos
        kt = k2 * e_neg
        bt = kka * e_neg
        at = -kkn * jnp.exp(cum - logw)
        b_end = kka * e_end
        k_end = k2 * e_end
        w_end = jnp.exp(cum_l)

        a_all = _mm_nt(jnp.concatenate([at, rt], axis=0),
                       jnp.concatenate([stack(bt), stack(kt)], axis=0))
        a_ab = jnp.where(strict, a_all[:L, :LANES], 0.0)
        a_ak = jnp.where(strict, a_all[:L, LANES:], 0.0)
        a_rb = jnp.where(incl, a_all[L:, :LANES], 0.0)
        a_rk = jnp.where(incl, a_all[L:, LANES:], 0.0)

        tinv = eye_pair + a_ab
        pw = a_ab
        for _ in range(5):
            pw = _mm(pw, stack(pw))
            tinv = tinv + _mm(tinv, stack(pw))

        av = _mm(a_ak, stack(v))
        pq = _mm(tinv, jnp.concatenate([stack(at), stack(av)], axis=1))
        pm = pq[:, :LANES]
        qm = pq[:, LANES:]
        rm = rt + _mm(a_rb, stack(pm))
        y0 = _mm(jnp.concatenate([a_rb, a_rk], axis=1), jnp.concatenate([stack(qm), stack(v)], axis=0))
        bk_t = jnp.concatenate([b_end, k_end], axis=0).T
        zeros = jnp.zeros_like(pm)
        mn = _mm(bk_t, jnp.concatenate([jnp.concatenate([pm, qm], axis=1),
                                        jnp.concatenate([zeros, v], axis=1)], axis=0))
        m_mat = jnp.where(diag_s, w_end, 0.0) + jnp.where(same_head, mn[:, :LANES], 0.0)
        n_mat = jnp.where(same_head, mn[:, LANES:], 0.0)

        s0 = st_ref[p]
        ys = _mm(jnp.concatenate([rm, m_mat], axis=0), s0)
        y = ys[:L] + y0
        st_ref[p] = ys[L:] + n_mat

        mean = jnp.dot(_bf(y), ones_bd, preferred_element_type=F32) * (1.0 / RW_N)
        yc = y - mean
        var = jnp.dot(_bf(yc * yc), ones_bd, preferred_element_type=F32) * (1.0 / RW_N)
        yn = yc * lax.rsqrt(var + GN_EPS) * lnw_ref[:, sl] + lnb_ref[:, sl]
        bonus = jnp.dot(_bf(r * k2 * rk_ref[:, sl]), ones_bd, preferred_element_type=F32) * v
        o_ref[:, sl] = _bf((yn + bonus) * g_out)


def _rwkv(z_all, S, mu_r, mu_k, mu_v, mu_l, w0, a0, kk, ka, rk, lnw, lnb, w2, a2, g2):
    T = z_all.shape[0]
    B = T // S
    nc = S // RW_L
    G = RW_PAIRS_PER_STEP
    W = G * LANES
    ngrp = RW_WIDTH // W
    zspec = lambda off: pl.BlockSpec((RW_L, W), lambda b, g, c, off=off: (b * nc + c, off * ngrp + g))
    vspec = pl.BlockSpec((1, W), lambda b, g, c: (0, g))
    lora_col = (z_all.shape[1] - 512) // 512
    return pl.pallas_call(
        functools.partial(_rwkv_kernel, pairs=G),
        grid=(B, ngrp, nc),
        in_specs=[zspec(0), zspec(1), zspec(2),
                  pl.BlockSpec((RW_L, 512), lambda b, g, c: (b * nc + c, lora_col)),
                  vspec, vspec, vspec,
                  pl.BlockSpec((1, 512), lambda b, g, c: (0, 0)),
                  vspec, vspec, vspec, vspec, vspec, vspec, vspec,
                  pl.BlockSpec((64, W), lambda b, g, c: (0, g)),
                  pl.BlockSpec((64, W), lambda b, g, c: (0, g)),
                  pl.BlockSpec((256, W), lambda b, g, c: (0, g))],
        out_specs=pl.BlockSpec((RW_L, W), lambda b, g, c: (b * nc + c, g)),
        out_shape=jax.ShapeDtypeStruct((T, RW_WIDTH), BF16),
        scratch_shapes=[pltpu.VMEM((G, LANES, LANES), F32),
                        pltpu.VMEM((1, W), F32), pltpu.VMEM((1, W), F32), pltpu.VMEM((1, W), F32),
                        pltpu.VMEM((1, 512), F32)],
        compiler_params=_params("parallel", "parallel", "arbitrary"),
        name="rwkv",
    )(z_all, z_all, z_all, z_all, mu_r, mu_k, mu_v, mu_l, w0, a0, kk, ka, rk, lnw, lnb, w2, a2, g2)


def _mla_proj_kernel(cq_ref, ckv_ref, kpe_ref, cos_ref, sin_ref, qn_ref, kvn_ref,
                     wqn_ref, wqp_ref, wqs_ref, wkv_ref, q_ref, kv_ref, kr_ref, *, scale):
    cqn = _bf(_rms(cq_ref[...], qn_ref[...]))
    ckvn = _bf(_rms(ckv_ref[...], kvn_ref[...]))
    cos = cos_ref[...]
    sin = sin_ref[...]
    q_nope = jnp.dot(cqn, wqn_ref[...], preferred_element_type=F32)
    q_pe = jnp.dot(cqn, wqp_ref[...], preferred_element_type=F32)
    q_ps = jnp.dot(cqn, wqs_ref[...], preferred_element_type=F32)
    for h in range(MLA_HEADS):
        hs = slice(h * LANES, (h + 1) * LANES)
        q_ref[:, 2 * h * LANES:(2 * h + 1) * LANES] = _bf(q_nope[:, hs] * scale)
        q_ref[:, (2 * h + 1) * LANES:(2 * h + 2) * LANES] = _bf((q_pe[:, hs] * cos + q_ps[:, hs] * sin) * scale)
    kv_ref[...] = _bf(jnp.dot(ckvn, wkv_ref[...], preferred_element_type=F32))
    blk = kpe_ref[...]
    kr_ref[...] = _bf(blk * cos + pltpu.roll(blk, MLA_ROPE, axis=1) * sin)


def _mla_proj(z_all, cos_t, sin_t, qn, kvn, wqn, wqp, wqs, wkv, tm=512):
    T = z_all.shape[0]
    HW = MLA_HEADS * 2 * LANES
    scale = float((MLA_NOPE + MLA_ROPE) ** -0.5)
    full = lambda a: pl.BlockSpec(a.shape, lambda i: (0, 0))
    kpe_blk = (z_all.shape[1] - LANES) // LANES
    return pl.pallas_call(
        functools.partial(_mla_proj_kernel, scale=scale),
        grid=(T // tm,),
        in_specs=[pl.BlockSpec((tm, MLA_RANK), lambda i: (i, 6)),
                  pl.BlockSpec((tm, MLA_RANK), lambda i: (i, 7)),
                  pl.BlockSpec((tm, LANES), lambda i: (i, kpe_blk)),
                  pl.BlockSpec((tm, LANES), lambda i: (i, 0)),
                  pl.BlockSpec((tm, LANES), lambda i: (i, 0)),
                  full(qn), full(kvn), full(wqn), full(wqp), full(wqs), full(wkv)],
        out_specs=[pl.BlockSpec((tm, HW), lambda i: (i, 0)),
                   pl.BlockSpec((tm, HW), lambda i: (i, 0)),
                   pl.BlockSpec((tm, LANES), lambda i: (i, 0))],
        out_shape=[jax.ShapeDtypeStruct((T, HW), BF16),
                   jax.ShapeDtypeStruct((T, HW), BF16),
                   jax.ShapeDtypeStruct((T, LANES), BF16)],
        compiler_params=_params("parallel"),
        name="mla_proj",
    )(z_all, z_all, z_all, cos_t, sin_t, qn, kvn, wqn, wqp, wqs, wkv)


def _attn_kernel(q_ref, kn_ref, kr_ref, v_ref, o_ref, m_ref, l_ref, acc_ref, *, tq, tk):
    qi = pl.program_id(2)
    ki = pl.program_id(3)
    last = qi

    @pl.when(ki == 0)
    def _():
        m_ref[...] = jnp.full_like(m_ref, -1e30)
        l_ref[...] = jnp.zeros_like(l_ref)
        acc_ref[...] = jnp.zeros_like(acc_ref)

    def update(masked):
        kcat = jnp.concatenate([kn_ref[...], kr_ref[...]], axis=1)
        s = lax.dot_general(q_ref[...], kcat, (((1,), (1,)), ((), ())), preferred_element_type=F32)
        if masked:
            q_last = (qi * tq + lax.broadcasted_iota(jnp.int32, (tq, tk), 0)) | (CHUNK - 1)
            k_pos = ki * tk + lax.broadcasted_iota(jnp.int32, (tq, tk), 1)
            s = jnp.where(k_pos <= q_last, s, -1e30)
        m_prev = m_ref[...]
        m_new = jnp.maximum(m_prev, jnp.max(s, axis=-1, keepdims=True))
        alpha = jnp.exp(m_prev - m_new)
        p = jnp.exp(s - m_new)
        l_ref[...] = alpha * l_ref[...] + jnp.sum(p, axis=-1, keepdims=True)
        acc_ref[...] = alpha * acc_ref[...] + jnp.dot(_bf(p), v_ref[...], preferred_element_type=F32)
        m_ref[...] = m_new

    @pl.when(ki < last)
    def _():
        update(False)

    @pl.when(ki == last)
    def _():
        update(True)
        o_ref[...] = _bf(acc_ref[...] / l_ref[...])


def _attention(q, kv, kr, S, tq=512, tk=512):
    T = q.shape[0]
    B = T // S
    nq, nk = S // tq, S // tk

    assert tq == tk

    def kidx(qi, ki):
        return jnp.minimum(ki, qi)

    return pl.pallas_call(
        functools.partial(_attn_kernel, tq=tq, tk=tk),
        grid=(B, MLA_HEADS, nq, nk),
        in_specs=[pl.BlockSpec((tq, 2 * LANES), lambda b, h, qi, ki: (b * nq + qi, h)),
                  pl.BlockSpec((tk, LANES), lambda b, h, qi, ki: (b * nk + kidx(qi, ki), 2 * h)),
                  pl.BlockSpec((tk, LANES), lambda b, h, qi, ki: (b * nk + kidx(qi, ki), 0)),
                  pl.BlockSpec((tk, LANES), lambda b, h, qi, ki: (b * nk + kidx(qi, ki), 2 * h + 1))],
        out_specs=pl.BlockSpec((tq, LANES), lambda b, h, qi, ki: (b * nq + qi, h)),
        out_shape=jax.ShapeDtypeStruct((T, MLA_HEADS * MLA_V), BF16),
        scratch_shapes=[pltpu.VMEM((tq, 1), F32), pltpu.VMEM((tq, 1), F32), pltpu.VMEM((tq, LANES), F32)],
        compiler_params=_params("parallel", "parallel", "parallel", "arbitrary"),
        name="attention",
    )(q, kv, kr, kv)


def _mix_kernel(yrw_ref, ymla_ref, grw_ref, gmla_ref, x_ref, wa_ref, wb_ref, wo_ref, g_ref, o_ref, acc_ref):
    j = pl.program_id(1)

    @pl.when(j == 0)
    def _():
        acc_ref[...] = jnp.zeros_like(acc_ref)

    t = (jax.nn.sigmoid(grw_ref[...]) * jnp.dot(yrw_ref[...], wa_ref[...], preferred_element_type=F32)
         + jax.nn.sigmoid(gmla_ref[...]) * jnp.dot(ymla_ref[...], wb_ref[...], preferred_element_type=F32))
    acc_ref[...] += jnp.dot(_bf(t), wo_ref[...], preferred_element_type=F32)

    @pl.when(j == pl.num_programs(1) - 1)
    def _():
        o_ref[...] = x_ref[...] + _rms(acc_ref[...], g_ref[...])


def _mix(y_rw, y_mla, z_all, x2, wa, wb, wo, g, tm=512, tn=512):
    T, D = x2.shape
    nj = D // tn
    gate0 = 4096 // tn
    return pl.pallas_call(
        _mix_kernel,
        grid=(T // tm, nj),
        in_specs=[pl.BlockSpec((tm, y_rw.shape[1]), lambda i, j: (i, 0)),
                  pl.BlockSpec((tm, y_mla.shape[1]), lambda i, j: (i, 0)),
                  pl.BlockSpec((tm, tn), lambda i, j: (i, gate0 + j)),
                  pl.BlockSpec((tm, tn), lambda i, j: (i, gate0 + nj + j)),
                  pl.BlockSpec((tm, D), lambda i, j: (i, 0)),
                  pl.BlockSpec((wa.shape[0], tn), lambda i, j: (0, j)),
                  pl.BlockSpec((wb.shape[0], tn), lambda i, j: (0, j)),
                  pl.BlockSpec((tn, D), lambda i, j: (j, 0)),
                  pl.BlockSpec((1, D), lambda i, j: (0, 0))],
        out_specs=pl.BlockSpec((tm, D), lambda i, j: (i, 0)),
        out_shape=jax.ShapeDtypeStruct((T, D), F32),
        scratch_shapes=[pltpu.VMEM((tm, D), F32)],
        compiler_params=_params("parallel", "arbitrary"),
        name="mix",
    )(y_rw, y_mla, z_all, z_all, x2, wa, wb, wo, g)


HALO = 16


def _gelu_tanh(x):
    return 0.5 * x * (1.0 + jnp.tanh(0.7978845608028654 * (x + 0.044715 * x * x * x)))


def _ffn_kernel(h_ref, halo_ref, gin_ref, wug_ref, wuv_ref, cwg_ref, cwv_ref, cbg_ref, cbv_ref,
                wd_ref, gout_ref, o_ref, xn_ref, upg_ref, upv_ref, acc_ref, *, tm, tiles_per_seq):
    i = pl.program_id(0)
    j = pl.program_id(1)

    @pl.when(j == 0)
    def _():
        xn_ref[0:HALO, :] = _bf(_rms(halo_ref[...], gin_ref[...]))
        xn_ref[HALO:, :] = _bf(_rms(h_ref[...], gin_ref[...]))
        acc_ref[...] = jnp.zeros_like(acc_ref)

    seq_start = (i % tiles_per_seq) == 0

    def conv_branch(w_ref, up_ref, cw_ref, cb_ref):
        up_ref[...] = jnp.dot(xn_ref[...], w_ref[...], preferred_element_type=F32)

        @pl.when(seq_start)
        def _():
            up_ref[0:HALO, :] = jnp.zeros((HALO, up_ref.shape[1]), F32)

        cw = cw_ref[...]
        return (cw[0:1, :] * up_ref[HALO - 2:HALO - 2 + tm, :] + cw[1:2, :] * up_ref[HALO - 1:HALO - 1 + tm, :]
                + cw[2:3, :] * up_ref[HALO:HALO + tm, :] + cb_ref[...])

    gate = conv_branch(wug_ref, upg_ref, cwg_ref, cbg_ref)
    val = conv_branch(wuv_ref, upv_ref, cwv_ref, cbv_ref)
    acc_ref[...] += jnp.dot(_bf(_gelu_tanh(gate) * val), wd_ref[...], preferred_element_type=F32)

    @pl.when(j == pl.num_programs(1) - 1)
    def _():
        o_ref[...] = h_ref[...] + _rms(acc_ref[...], gout_ref[...])


def _ffn(h1, S, gin, w_up, conv_w, conv_b, w_down, gout, tm=512, tf=512):
    T, D = h1.shape
    F = w_down.shape[0]
    nf = F // tf
    hb = tm // HALO
    return pl.pallas_call(
        functools.partial(_ffn_kernel, tm=tm, tiles_per_seq=S // tm),
        grid=(T // tm, nf),
        in_specs=[pl.BlockSpec((tm, D), lambda i, j: (i, 0)),
                  pl.BlockSpec((HALO, D), lambda i, j: (jnp.maximum(i * hb - 1, 0), 0)),
                  pl.BlockSpec((1, D), lambda i, j: (0, 0)),
                  pl.BlockSpec((D, tf), lambda i, j: (0, j)),
                  pl.BlockSpec((D, tf), lambda i, j: (0, nf + j)),
                  pl.BlockSpec((3, tf), lambda i, j: (0, j)),
                  pl.BlockSpec((3, tf), lambda i, j: (0, nf + j)),
                  pl.BlockSpec((1, tf), lambda i, j: (0, j)),
                  pl.BlockSpec((1, tf), lambda i, j: (0, nf + j)),
                  pl.BlockSpec((tf, D), lambda i, j: (j, 0)),
                  pl.BlockSpec((1, D), lambda i, j: (0, 0))],
        out_specs=pl.BlockSpec((tm, D), lambda i, j: (i, 0)),
        out_shape=jax.ShapeDtypeStruct((T, D), F32),
        scratch_shapes=[pltpu.VMEM((tm + HALO, D), BF16),
                        pltpu.VMEM((tm + HALO, tf), F32), pltpu.VMEM((tm + HALO, tf), F32),
                        pltpu.VMEM((tm, D), F32)],
        compiler_params=_params("parallel", "arbitrary"),
        name="ffn",
    )(h1, h1, gin, w_up, w_up, conv_w, conv_w, conv_b, conv_b, w_down, gout)


def _ple_kernel(h_ref, p_ref, wpg_ref, wple_ref, g_ref, o_ref):
    h = h_ref[...]
    gate = jax.nn.sigmoid(jnp.dot(_bf(h), wpg_ref[...], preferred_element_type=F32))
    e = jnp.dot(_bf(p_ref[...]), wple_ref[...], preferred_element_type=F32)
    o_ref[...] = h + _rms(gate * e, g_ref[...])


def _ple(h2, p2, wpg, wple, g, tm=512):
    T, D = h2.shape
    full = lambda a: pl.BlockSpec(a.shape, lambda i: (0, 0))
    return pl.pallas_call(
        _ple_kernel,
        grid=(T // tm,),
        in_specs=[pl.BlockSpec((tm, D), lambda i: (i, 0)),
                  pl.BlockSpec((tm, p2.shape[1]), lambda i: (i, 0)),
                  full(wpg), full(wple), full(g)],
        out_specs=pl.BlockSpec((tm, D), lambda i: (i, 0)),
        out_shape=jax.ShapeDtypeStruct((T, D), F32),
        compiler_params=_params("parallel"),
        name="ple",
    )(h2, p2, wpg, wple, g)


def _swap_halves(w):
    half = w.shape[-1] // 2
    return jnp.concatenate([w[..., half:], w[..., :half]], axis=-1)


def _pack_w_in(w_in):
    D = w_in.shape[0]
    rkv = w_in[:, :3 * RW_WIDTH]
    lora = w_in[:, 3 * RW_WIDTH:3 * RW_WIDTH + 288]
    m0 = 3 * RW_WIDTH + 288
    cq = w_in[:, m0:m0 + MLA_RANK]
    ckv = w_in[:, m0 + MLA_RANK:m0 + 2 * MLA_RANK]
    kpe = w_in[:, m0 + 2 * MLA_RANK:m0 + 2 * MLA_RANK + MLA_ROPE]
    gates = w_in[:, m0 + 2 * MLA_RANK + MLA_ROPE:]
    pad = jnp.zeros((D, 384 - 288), w_in.dtype)
    return jnp.concatenate([rkv, cq, ckv, gates, lora, pad, kpe, _swap_halves(kpe)], axis=1)


def _row(v):
    return v.reshape(1, -1).astype(F32)


def kernel(x, p, positions, pre_mix_norm, w_in, rw_mu, rw_w0, rw_w2, rw_a0, rw_a2, rw_g2, rw_k_k, rw_k_a, rw_r_k, rw_lnx_w, rw_lnx_b, mla_q_norm, mla_w_q_up, mla_kv_norm, mla_w_kv_up, w_branch_rw, w_branch_mla, w_out, post_mix_norm, pre_ffn_norm, w_up, conv_w, conv_b, w_down, post_ffn_norm, w_ple, w_ple_gate, ple_norm):
    B, S, D = x.shape
    T = B * S
    depth = w_in.shape[0]

    inv_freq = ROPE_BASE ** (-jnp.arange(0, MLA_ROPE, 2, dtype=F32) / MLA_ROPE)
    ang = positions.astype(F32)[..., None] * inv_freq
    cos = jnp.cos(ang).reshape(T, MLA_ROPE // 2)
    sin = jnp.sin(ang).reshape(T, MLA_ROPE // 2)
    zpad = jnp.zeros((T, LANES - MLA_ROPE), F32)
    cos_t = jnp.concatenate([cos, cos, zpad], axis=1)
    sin_t = jnp.concatenate([-sin, sin, zpad], axis=1)

    h = x.reshape(T, D)
    for i in range(depth):
        w_in_p = _bf(_pack_w_in(w_in[i]))
        z_all = _in_proj(h, _row(pre_mix_norm[i]), w_in_p)

        mu = rw_mu[i]
        mu_l = jnp.concatenate([mu[3 * RW_WIDTH:], jnp.zeros((512 - 288,), F32)]).reshape(1, 512)
        g2p = jnp.concatenate([rw_g2[i], jnp.zeros((256 - rw_g2.shape[1], RW_WIDTH), F32)], axis=0)
        w2p = jnp.concatenate([rw_w2[i]], axis=0)
        y_rw = _rwkv(z_all, S, _row(mu[:RW_WIDTH]), _row(mu[RW_WIDTH:2 * RW_WIDTH]), _row(mu[2 * RW_WIDTH:3 * RW_WIDTH]),
                     mu_l, _row(rw_w0[i]), _row(rw_a0[i]), _row(rw_k_k[i]), _row(rw_k_a[i]), _row(rw_r_k[i]),
                     _row(rw_lnx_w[i]), _row(rw_lnx_b[i]), _bf(w2p), _bf(rw_a2[i]), _bf(g2p))

        wq = mla_w_q_up[i].reshape(MLA_RANK, MLA_HEADS, MLA_NOPE + MLA_ROPE)
        wqn = wq[:, :, :MLA_NOPE].reshape(MLA_RANK, MLA_HEADS * MLA_NOPE)
        wq_pe = wq[:, :, MLA_NOPE:]
        padh = jnp.zeros((MLA_RANK, MLA_HEADS, LANES - MLA_ROPE), F32)
        wqp = jnp.concatenate([wq_pe, padh], axis=-1).reshape(MLA_RANK, MLA_HEADS * LANES)
        wqs = jnp.concatenate([_swap_halves(wq_pe), padh], axis=-1).reshape(MLA_RANK, MLA_HEADS * LANES)
        q, kv, kr = _mla_proj(z_all, cos_t, sin_t, _row(mla_q_norm[i]), _row(mla_kv_norm[i]),
                              _bf(wqn), _bf(wqp), _bf(wqs), _bf(mla_w_kv_up[i]))
        y_mla = _attention(q, kv, kr, S)

        h = _mix(y_rw, y_mla, z_all, h, _bf(w_branch_rw[i]), _bf(w_branch_mla[i]), _bf(w_out[i]), _row(post_mix_norm[i]))
        h = _ffn(h, S, _row(pre_ffn_norm[i]), _bf(w_up[i]), conv_w[i], _row(conv_b[i]), _bf(w_down[i]), _row(post_ffn_norm[i]))
        h = _ple(h, p[i].reshape(T, -1), _bf(w_ple_gate[i]), _bf(w_ple[i]), _row(ple_norm[i]))
    return h.reshape(B, S, D)
```

```python
import functools

import jax
import jax.numpy as jnp
from jax import lax
from jax.experimental import pallas as pl
from jax.experimental.pallas import tpu as pltpu

NORM_EPS = 1e-6
GN_EPS = 64e-5
CHUNK = 64
ROPE_BASE = 10000.0

RW_HEADS = 16
RW_N = 64
RW_WIDTH = RW_HEADS * RW_N
RW_L = 64
LANES = 128
RW_PAIRS_PER_STEP = 8

MLA_HEADS = 8
MLA_NOPE = 128
MLA_ROPE = 64
MLA_V = 128
MLA_RANK = 512

VMEM_LIMIT_BYTES = 56 * 1024 * 1024

BF16 = jnp.bfloat16
F32 = jnp.float32


def _params(*sem):
    return pltpu.CompilerParams(dimension_semantics=sem, vmem_limit_bytes=VMEM_LIMIT_BYTES)


def _bf(x):
    return x.astype(BF16)


def _mm(a, b):
    return jnp.dot(_bf(a), _bf(b), preferred_element_type=F32)


def _mm_nt(a, b):
    return lax.dot_general(_bf(a), _bf(b), (((1,), (1,)), ((), ())), preferred_element_type=F32)


def _rms(x, g):
    return x * lax.rsqrt(jnp.mean(x * x, axis=-1, keepdims=True) + NORM_EPS) * g


def _inproj_kernel(x_ref, g_ref, w_ref, o_ref, xn_ref):
    @pl.when(pl.program_id(1) == 0)
    def _():
        xn_ref[...] = _bf(_rms(x_ref[...], g_ref[...]))

    o_ref[...] = jnp.dot(xn_ref[...], w_ref[...], preferred_element_type=F32)


def _in_proj(x2, g, w, tm=1024, tn=512):
    T, D = x2.shape
    N = w.shape[1]
    return pl.pallas_call(
        _inproj_kernel,
        grid=(T // tm, N // tn),
        in_specs=[
            pl.BlockSpec((tm, D), lambda i, j: (i, 0)),
            pl.BlockSpec((1, D), lambda i, j: (0, 0)),
            pl.BlockSpec((D, tn), lambda i, j: (0, j)),
        ],
        out_specs=pl.BlockSpec((tm, tn), lambda i, j: (i, j)),
        out_shape=jax.ShapeDtypeStruct((T, N), F32),
        scratch_shapes=[pltpu.VMEM((tm, D), BF16)],
        compiler_params=_params("parallel", "arbitrary"),
        name="in_proj",
    )(x2, g, w)


def _split3(x):
    hi = _bf(x)
    r1 = x - hi.astype(F32)
    mid = _bf(r1)
    lo = _bf(r1 - mid.astype(F32))
    return hi, mid, lo


def _rwkv_kernel(zr_ref, zk_ref, zv_ref, zl_ref, mur_ref, muk_ref, muv_ref, mul_ref,
                 w0_ref, a0_ref, kk_ref, ka_ref, rk_ref, lnw_ref, lnb_ref,
                 w2_ref, a2_ref, g2_ref, o_ref,
                 st_ref, pr_ref, pk_ref, pv_ref, plr_ref, *, pairs):
    L = RW_L
    c = pl.program_id(2)

    @pl.when(c == 0)
    def _():
        st_ref[...] = jnp.zeros_like(st_ref)
        pr_ref[...] = jnp.zeros_like(pr_ref)
        pk_ref[...] = jnp.zeros_like(pk_ref)
        pv_ref[...] = jnp.zeros_like(pv_ref)
        plr_ref[...] = jnp.zeros_like(plr_ref)

    def shift_lerp(z_ref, prev_ref, mu_ref):
        z = z_ref[...]
        rolled = pltpu.roll(z, 1, axis=0)
        row = lax.broadcasted_iota(jnp.int32, z.shape, 0)
        zs = jnp.where(row == 0, prev_ref[...], rolled)
        prev_ref[...] = z[L - 1:L, :]
        return z + (zs - z) * mu_ref[...]

    r_all = shift_lerp(zr_ref, pr_ref, mur_ref)
    k_all = shift_lerp(zk_ref, pk_ref, muk_ref)
    v_all = shift_lerp(zv_ref, pv_ref, muv_ref)
    lo_all = shift_lerp(zl_ref, plr_ref, mul_ref)
    wd = jnp.tanh(lo_all[:, 0:64])
    ad = lo_all[:, 64:128]
    gd = jax.nn.sigmoid(lo_all[:, 128:384])

    lane = lax.broadcasted_iota(jnp.int32, (1, LANES), 1)
    m0 = lane < RW_N
    row_l = lax.broadcasted_iota(jnp.int32, (L, LANES), 0)
    col_l = lax.broadcasted_iota(jnp.int32, (L, LANES), 1) & (RW_N - 1)
    strict = col_l < row_l
    incl = col_l <= row_l
    eye_pair = (col_l == row_l).astype(F32)
    row_s = lax.broadcasted_iota(jnp.int32, (LANES, LANES), 0)
    col_s = lax.broadcasted_iota(jnp.int32, (LANES, LANES), 1)
    same_head = (row_s < RW_N) == (col_s < RW_N)
    diag_s = row_s == col_s
    tri = (lax.broadcasted_iota(jnp.int32, (L, L), 1) <= lax.broadcasted_iota(jnp.int32, (L, L), 0)).astype(BF16)
    ones_bd = same_head.astype(BF16)

    def stack(x):
        return jnp.concatenate([jnp.where(m0, x, 0.0), jnp.where(m0, 0.0, x)], axis=0)

    def head_sum(x):
        return jnp.dot(_bf(x), ones_bd, preferred_element_type=F32)

    def cat(xs, axis):
        return jnp.concatenate(xs, axis=axis)

    P = range(pairs)
    sls = [slice(p * LANES, (p + 1) * LANES) for p in P]
    r = [r_all[:, sl] for sl in sls]
    k = [k_all[:, sl] for sl in sls]
    v = [v_all[:, sl] for sl in sls]
    xw = [w0_ref[:, sl] + _mm(wd, w2_ref[:, sl]) for sl in sls]
    ga = [a0_ref[:, sl] + _mm(ad, a2_ref[:, sl]) for sl in sls]
    g_out = [_mm(gd, g2_ref[:, sl]) for sl in sls]
    kkv = [k[p] * kk_ref[:, sls[p]] for p in P]
    ss = [head_sum(kkv[p] * kkv[p]) for p in P]
    logw = [-0.6065306597126334 * jax.nn.sigmoid(x) for x in xw]
    parts = [_split3(x) for x in logw]
    cum = [sum(jnp.dot(tri, part, preferred_element_type=F32) for part in parts[p]) for p in P]
    gate = [jax.nn.sigmoid(x) for x in ga]
    kkn = [kkv[p] / jnp.maximum(jnp.sqrt(ss[p]), 1e-12) for p in P]
    k2 = [k[p] * (1.0 + (gate[p] - 1.0) * ka_ref[:, sls[p]]) for p in P]
    bonus_in = [head_sum(r[p] * k2[p] * rk_ref[:, sls[p]]) for p in P]
    cum_l = [c_[L - 1:L, :] for c_ in cum]
    e_neg = [jnp.exp(-c_) for c_ in cum]
    e_end = [jnp.exp(cum_l[p] - cum[p]) for p in P]
    kka = [kkn[p] * gate[p] for p in P]
    rt = [r[p] * jnp.exp(cum[p]) for p in P]
    kt = [k2[p] * e_neg[p] for p in P]
    bt = [kka[p] * e_neg[p] for p in P]
    at = [-kkn[p] * jnp.exp(cum[p] - logw[p]) for p in P]
    bk_t = [cat([kka[p] * e_end[p], k2[p] * e_end[p]], 0).T for p in P]
    w_end = [jnp.exp(c_) for c_ in cum_l]

    a_all = [_mm_nt(cat([at[p], rt[p]], 0), cat([stack(bt[p]), stack(kt[p])], 0)) for p in P]
    a_ab = [jnp.where(strict, a[:L, :LANES], 0.0) for a in a_all]
    a_ak = [jnp.where(strict, a[:L, LANES:], 0.0) for a in a_all]
    a_rb = [jnp.where(incl, a[L:, :LANES], 0.0) for a in a_all]
    a_rk = [jnp.where(incl, a[L:, LANES:], 0.0) for a in a_all]
    av = [_mm(a_ak[p], stack(v[p])) for p in P]

    tinv = [eye_pair + a for a in a_ab]
    pw = a_ab
    for _ in range(5):
        pw = [_mm(x, stack(x)) for x in pw]
        tinv = [tinv[p] + _mm(tinv[p], stack(pw[p])) for p in P]

    pq = [_mm(tinv[p], cat([stack(at[p]), stack(av[p])], 1)) for p in P]
    pm = [x[:, :LANES] for x in pq]
    qm = [x[:, LANES:] for x in pq]
    rm = [rt[p] + _mm(a_rb[p], stack(pm[p])) for p in P]
    y0 = [_mm(cat([a_rb[p], a_rk[p]], 1), cat([stack(qm[p]), stack(v[p])], 0)) for p in P]
    mn = [_mm(bk_t[p], cat([cat([pm[p], qm[p]], 1), cat([jnp.zeros_like(v[p]), v[p]], 1)], 0)) for p in P]
    m_mat = [jnp.where(diag_s, w_end[p], 0.0) + jnp.where(same_head, mn[p][:, :LANES], 0.0) for p in P]
    n_mat = [jnp.where(same_head, mn[p][:, LANES:], 0.0) for p in P]

    ys = [_mm(cat([rm[p], m_mat[p]], 0), st_ref[p]) for p in P]
    for p in P:
        st_ref[p] = ys[p][L:] + n_mat[p]
    y = [ys[p][:L] + y0[p] for p in P]
    mean = [head_sum(x) * (1.0 / RW_N) for x in y]
    yc = [y[p] - mean[p] for p in P]
    var = [head_sum(x * x) * (1.0 / RW_N) for x in yc]
    for p in P:
        yn = yc[p] * lax.rsqrt(var[p] + GN_EPS) * lnw_ref[:, sls[p]] + lnb_ref[:, sls[p]]
        o_ref[:, sls[p]] = _bf((yn + bonus_in[p] * v[p]) * g_out[p])


def _rwkv(z_all, S, mu_r, mu_k, mu_v, mu_l, w0, a0, kk, ka, rk, lnw, lnb, w2, a2, g2):
    T = z_all.shape[0]
    B = T // S
    nc = S // RW_L
    G = RW_PAIRS_PER_STEP
    W = G * LANES
    ngrp = RW_WIDTH // W
    zspec = lambda off: pl.BlockSpec((RW_L, W), lambda b, g, c, off=off: (b * nc + c, off * ngrp + g))
    vspec = pl.BlockSpec((1, W), lambda b, g, c: (0, g))
    lora_col = (z_all.shape[1] - 512) // 512
    return pl.pallas_call(
        functools.partial(_rwkv_kernel, pairs=G),
        grid=(B, ngrp, nc),
        in_specs=[zspec(0), zspec(1), zspec(2),
                  pl.BlockSpec((RW_L, 512), lambda b, g, c: (b * nc + c, lora_col)),
                  vspec, vspec, vspec,
                  pl.BlockSpec((1, 512), lambda b, g, c: (0, 0)),
                  vspec, vspec, vspec, vspec, vspec, vspec, vspec,
                  pl.BlockSpec((64, W), lambda b, g, c: (0, g)),
                  pl.BlockSpec((64, W), lambda b, g, c: (0, g)),
                  pl.BlockSpec((256, W), lambda b, g, c: (0, g))],
        out_specs=pl.BlockSpec((RW_L, W), lambda b, g, c: (b * nc + c, g)),
        out_shape=jax.ShapeDtypeStruct((T, RW_WIDTH), BF16),
        scratch_shapes=[pltpu.VMEM((G, LANES, LANES), F32),
                        pltpu.VMEM((1, W), F32), pltpu.VMEM((1, W), F32), pltpu.VMEM((1, W), F32),
                        pltpu.VMEM((1, 512), F32)],
        compiler_params=_params("parallel", "parallel", "arbitrary"),
        name="rwkv",
    )(z_all, z_all, z_all, z_all, mu_r, mu_k, mu_v, mu_l, w0, a0, kk, ka, rk, lnw, lnb, w2, a2, g2)


def _mla_proj_kernel(cq_ref, ckv_ref, kpe_ref, cos_ref, sin_ref, qn_ref, kvn_ref,
                     wqn_ref, wqp_ref, wqs_ref, wkv_ref, q_ref, kv_ref, kr_ref, *, scale):
    cqn = _bf(_rms(cq_ref[...], qn_ref[...]))
    ckvn = _bf(_rms(ckv_ref[...], kvn_ref[...]))
    cos = cos_ref[...]
    sin = sin_ref[...]
    q_nope = jnp.dot(cqn, wqn_ref[...], preferred_element_type=F32)
    q_pe = jnp.dot(cqn, wqp_ref[...], preferred_element_type=F32)
    q_ps = jnp.dot(cqn, wqs_ref[...], preferred_element_type=F32)
    for h in range(MLA_HEADS):
        hs = slice(h * LANES, (h + 1) * LANES)
        q_ref[:, 2 * h * LANES:(2 * h + 1) * LANES] = _bf(q_nope[:, hs] * scale)
        q_ref[:, (2 * h + 1) * LANES:(2 * h + 2) * LANES] = _bf((q_pe[:, hs] * cos + q_ps[:, hs] * sin) * scale)
    kv_ref[...] = _bf(jnp.dot(ckvn, wkv_ref[...], preferred_element_type=F32))
    blk = kpe_ref[...]
    kr_ref[...] = _bf(blk * cos + pltpu.roll(blk, MLA_ROPE, axis=1) * sin)


def _mla_proj(z_all, cos_t, sin_t, qn, kvn, wqn, wqp, wqs, wkv, tm=512):
    T = z_all.shape[0]
    HW = MLA_HEADS * 2 * LANES
    scale = float((MLA_NOPE + MLA_ROPE) ** -0.5 * 1.4426950408889634)
    full = lambda a: pl.BlockSpec(a.shape, lambda i: (0, 0))
    kpe_blk = (z_all.shape[1] - LANES) // LANES
    return pl.pallas_call(
        functools.partial(_mla_proj_kernel, scale=scale),
        grid=(T // tm,),
        in_specs=[pl.BlockSpec((tm, MLA_RANK), lambda i: (i, 6)),
                  pl.BlockSpec((tm, MLA_RANK), lambda i: (i, 7)),
                  pl.BlockSpec((tm, LANES), lambda i: (i, kpe_blk)),
                  pl.BlockSpec((tm, LANES), lambda i: (i, 0)),
                  pl.BlockSpec((tm, LANES), lambda i: (i, 0)),
                  full(qn), full(kvn), full(wqn), full(wqp), full(wqs), full(wkv)],
        out_specs=[pl.BlockSpec((tm, HW), lambda i: (i, 0)),
                   pl.BlockSpec((tm, HW), lambda i: (i, 0)),
                   pl.BlockSpec((tm, LANES), lambda i: (i, 0))],
        out_shape=[jax.ShapeDtypeStruct((T, HW), BF16),
                   jax.ShapeDtypeStruct((T, HW), BF16),
                   jax.ShapeDtypeStruct((T, LANES), BF16)],
        compiler_params=_params("parallel"),
        name="mla_proj",
    )(z_all, z_all, z_all, cos_t, sin_t, qn, kvn, wqn, wqp, wqs, wkv)


ATT_HEADS_PER_STEP = 2


def _attn_kernel(qi_ref, ki_ref, q_ref, kv_ref, kr_ref, o_ref, m_ref, l_ref, acc_ref, *, tq, tk, heads):
    t = pl.program_id(2)
    qi = qi_ref[t]
    ki = ki_ref[t]
    H = range(heads)

    @pl.when(ki == 0)
    def _():
        m_ref[...] = jnp.full_like(m_ref, -1e30)
        l_ref[...] = jnp.zeros_like(l_ref)
        acc_ref[...] = jnp.zeros_like(acc_ref)

    def update(masked):
        kr = kr_ref[...]
        s = [lax.dot_general(q_ref[:, 2 * h * LANES:(2 * h + 2) * LANES],
                             jnp.concatenate([kv_ref[:, 2 * h * LANES:(2 * h + 1) * LANES], kr], axis=1),
                             (((1,), (1,)), ((), ())), preferred_element_type=F32) for h in H]
        if masked:
            q_last = (qi * tq + lax.broadcasted_iota(jnp.int32, (tq, tk), 0)) | (CHUNK - 1)
            k_pos = ki * tk + lax.broadcasted_iota(jnp.int32, (tq, tk), 1)
            vis = k_pos <= q_last
            s = [jnp.where(vis, x, -1e30) for x in s]
        m_prev = [m_ref[h] for h in H]
        m_new = [jnp.maximum(m_prev[h], jnp.max(s[h], axis=-1, keepdims=True)) for h in H]
        p = [jnp.exp2(s[h] - jnp.concatenate([m_new[h]] * (tk // LANES), axis=1)) for h in H]
        alpha = [jnp.exp2(m_prev[h] - m_new[h]) for h in H]
        ones = jnp.ones((tk, LANES), BF16)
        pv = [jnp.dot(_bf(p[h]), jnp.concatenate([kv_ref[:, (2 * h + 1) * LANES:(2 * h + 2) * LANES], ones], axis=1),
                      preferred_element_type=F32) for h in H]
        for h in H:
            l_ref[h] = alpha[h] * l_ref[h] + pv[h][:, LANES:]
            acc_ref[h] = alpha[h] * acc_ref[h] + pv[h][:, :LANES]
            m_ref[h] = m_new[h]

    @pl.when(ki < qi)
    def _():
        update(False)

    @pl.when(ki == qi)
    def _():
        update(True)
        for h in H:
            o_ref[:, h * LANES:(h + 1) * LANES] = _bf(acc_ref[h] / l_ref[h])


def _attention(q, kv, kr, S, tq=512):
    T = q.shape[0]
    B = T // S
    nq = S // tq
    G = ATT_HEADS_PER_STEP
    pairs = [(a, b) for a in range(nq) for b in range(a + 1)]
    qi_tab = jnp.asarray([a for a, _ in pairs], jnp.int32)
    ki_tab = jnp.asarray([b for _, b in pairs], jnp.int32)
    grid_spec = pltpu.PrefetchScalarGridSpec(
        num_scalar_prefetch=2,
        grid=(B, MLA_HEADS // G, len(pairs)),
        in_specs=[pl.BlockSpec((tq, 2 * G * LANES), lambda b, g, t, qt, kt: (b * nq + qt[t], g)),
                  pl.BlockSpec((tq, 2 * G * LANES), lambda b, g, t, qt, kt: (b * nq + kt[t], g)),
                  pl.BlockSpec((tq, LANES), lambda b, g, t, qt, kt: (b * nq + kt[t], 0))],
        out_specs=pl.BlockSpec((tq, G * LANES), lambda b, g, t, qt, kt: (b * nq + qt[t], g)),
        scratch_shapes=[pltpu.VMEM((G, tq, LANES), F32), pltpu.VMEM((G, tq, LANES), F32), pltpu.VMEM((G, tq, LANES), F32)])
    return pl.pallas_call(
        functools.partial(_attn_kernel, tq=tq, tk=tq, heads=G),
        grid_spec=grid_spec,
        out_shape=jax.ShapeDtypeStruct((T, MLA_HEADS * MLA_V), BF16),
        compiler_params=_params("parallel", "parallel", "arbitrary"),
        name="attention",
    )(qi_tab, ki_tab, q, kv, kr)


def _mix_kernel(yrw_ref, ymla_ref, grw_ref, gmla_ref, x_ref, wa_ref, wb_ref, wo_ref, g_ref, o_ref, acc_ref):
    j = pl.program_id(1)

    @pl.when(j == 0)
    def _():
        acc_ref[...] = jnp.zeros_like(acc_ref)

    t = (jax.nn.sigmoid(grw_ref[...]) * jnp.dot(yrw_ref[...], wa_ref[...], preferred_element_type=F32)
         + jax.nn.sigmoid(gmla_ref[...]) * jnp.dot(ymla_ref[...], wb_ref[...], preferred_element_type=F32))
    acc_ref[...] += jnp.dot(_bf(t), wo_ref[...], preferred_element_type=F32)

    @pl.when(j == pl.num_programs(1) - 1)
    def _():
        o_ref[...] = x_ref[...] + _rms(acc_ref[...], g_ref[...])


def _mix(y_rw, y_mla, z_all, x2, wa, wb, wo, g, tm=512, tn=512):
    T, D = x2.shape
    nj = D // tn
    gate0 = 4096 // tn
    return pl.pallas_call(
        _mix_kernel,
        grid=(T // tm, nj),
        in_specs=[pl.BlockSpec((tm, y_rw.shape[1]), lambda i, j: (i, 0)),
                  pl.BlockSpec((tm, y_mla.shape[1]), lambda i, j: (i, 0)),
                  pl.BlockSpec((tm, tn), lambda i, j: (i, gate0 + j)),
                  pl.BlockSpec((tm, tn), lambda i, j: (i, gate0 + nj + j)),
                  pl.BlockSpec((tm, D), lambda i, j: (i, 0)),
                  pl.BlockSpec((wa.shape[0], tn), lambda i, j: (0, j)),
                  pl.BlockSpec((wb.shape[0], tn), lambda i, j: (0, j)),
                  pl.BlockSpec((tn, D), lambda i, j: (j, 0)),
                  pl.BlockSpec((1, D), lambda i, j: (0, 0))],
        out_specs=pl.BlockSpec((tm, D), lambda i, j: (i, 0)),
        out_shape=jax.ShapeDtypeStruct((T, D), F32),
        scratch_shapes=[pltpu.VMEM((tm, D), F32)],
        compiler_params=_params("parallel", "arbitrary"),
        name="mix",
    )(y_rw, y_mla, z_all, z_all, x2, wa, wb, wo, g)


HALO = 16
FFN_SUB = 256


def _gelu_tanh(x):
    return 0.5 * x * (1.0 + jnp.tanh(0.7978845608028654 * (x + 0.044715 * x * x * x)))


def _ffn_kernel(h_ref, halo_ref, gin_ref, wug_ref, wuv_ref, cwg_ref, cwv_ref, cbg_ref, cbv_ref,
                wd_ref, gout_ref, o_ref, xn_ref, upg_ref, upv_ref, acc_ref, *, tm, tiles_per_seq):
    i = pl.program_id(0)
    j = pl.program_id(1)

    @pl.when(j == 0)
    def _():
        xn_ref[0:HALO, :] = _bf(_rms(halo_ref[...], gin_ref[...]))
        xn_ref[HALO:, :] = _bf(_rms(h_ref[...], gin_ref[...]))
        acc_ref[...] = jnp.zeros_like(acc_ref)

    keep = jnp.where((i % tiles_per_seq) == 0, 0.0, 1.0).astype(F32)
    tf = upg_ref.shape[1]
    subs = [slice(c, c + FFN_SUB) for c in range(0, tf, FFN_SUB)]

    for cs in subs:
        for w_ref, up_ref in ((wug_ref, upg_ref), (wuv_ref, upv_ref)):
            up = jnp.dot(xn_ref[...], w_ref[:, cs], preferred_element_type=F32)
            up_ref[0:HALO, cs] = up[0:HALO] * keep
            up_ref[HALO:, cs] = up[HALO:]

    def conv(up_ref, cw_ref, cb_ref, cs):
        cw = cw_ref[:, cs]
        return (cw[0:1, :] * up_ref[HALO - 2:HALO - 2 + tm, cs] + cw[1:2, :] * up_ref[HALO - 1:HALO - 1 + tm, cs]
                + cw[2:3, :] * up_ref[HALO:HALO + tm, cs] + cb_ref[:, cs])

    act = [_bf(_gelu_tanh(conv(upg_ref, cwg_ref, cbg_ref, cs)) * conv(upv_ref, cwv_ref, cbv_ref, cs)) for cs in subs]
    acc_ref[...] += jnp.dot(jnp.concatenate(act, axis=1), wd_ref[...], preferred_element_type=F32)

    @pl.when(j == pl.num_programs(1) - 1)
    def _():
        o_ref[...] = h_ref[...] + _rms(acc_ref[...], gout_ref[...])


def _ffn(h1, S, gin, w_up, conv_w, conv_b, w_down, gout, tm=512, tf=512):
    T, D = h1.shape
    F = w_down.shape[0]
    nf = F // tf
    hb = tm // HALO
    return pl.pallas_call(
        functools.partial(_ffn_kernel, tm=tm, tiles_per_seq=S // tm),
        grid=(T // tm, nf),
        in_specs=[pl.BlockSpec((tm, D), lambda i, j: (i, 0)),
                  pl.BlockSpec((HALO, D), lambda i, j: (jnp.maximum(i * hb - 1, 0), 0)),
                  pl.BlockSpec((1, D), lambda i, j: (0, 0)),
                  pl.BlockSpec((D, tf), lambda i, j: (0, j)),
                  pl.BlockSpec((D, tf), lambda i, j: (0, nf + j)),
                  pl.BlockSpec((3, tf), lambda i, j: (0, j)),
                  pl.BlockSpec((3, tf), lambda i, j: (0, nf + j)),
                  pl.BlockSpec((1, tf), lambda i, j: (0, j)),
                  pl.BlockSpec((1, tf), lambda i, j: (0, nf + j)),
                  pl.BlockSpec((tf, D), lambda i, j: (j, 0)),
                  pl.BlockSpec((1, D), lambda i, j: (0, 0))],
        out_specs=pl.BlockSpec((tm, D), lambda i, j: (i, 0)),
        out_shape=jax.ShapeDtypeStruct((T, D), F32),
        scratch_shapes=[pltpu.VMEM((tm + HALO, D), BF16),
                        pltpu.VMEM((tm + HALO, tf), F32), pltpu.VMEM((tm + HALO, tf), F32),
                        pltpu.VMEM((tm, D), F32)],
        compiler_params=_params("parallel", "arbitrary"),
        name="ffn",
    )(h1, h1, gin, w_up, w_up, conv_w, conv_w, conv_b, conv_b, w_down, gout)


def _ple_kernel(h_ref, p_ref, wpg_ref, wple_ref, g_ref, o_ref):
    h = h_ref[...]
    gate = jax.nn.sigmoid(jnp.dot(_bf(h), wpg_ref[...], preferred_element_type=F32))
    e = jnp.dot(_bf(p_ref[...]), wple_ref[...], preferred_element_type=F32)
    o_ref[...] = h + _rms(gate * e, g_ref[...])


def _ple(h2, p2, wpg, wple, g, tm=512):
    T, D = h2.shape
    full = lambda a: pl.BlockSpec(a.shape, lambda i: (0, 0))
    return pl.pallas_call(
        _ple_kernel,
        grid=(T // tm,),
        in_specs=[pl.BlockSpec((tm, D), lambda i: (i, 0)),
                  pl.BlockSpec((tm, p2.shape[1]), lambda i: (i, 0)),
                  full(wpg), full(wple), full(g)],
        out_specs=pl.BlockSpec((tm, D), lambda i: (i, 0)),
        out_shape=jax.ShapeDtypeStruct((T, D), F32),
        compiler_params=_params("parallel"),
        name="ple",
    )(h2, p2, wpg, wple, g)


def _swap_halves(w):
    half = w.shape[-1] // 2
    return jnp.concatenate([w[..., half:], w[..., :half]], axis=-1)


def _pack_w_in(w_in):
    D = w_in.shape[0]
    rkv = w_in[:, :3 * RW_WIDTH]
    lora = w_in[:, 3 * RW_WIDTH:3 * RW_WIDTH + 288]
    m0 = 3 * RW_WIDTH + 288
    cq = w_in[:, m0:m0 + MLA_RANK]
    ckv = w_in[:, m0 + MLA_RANK:m0 + 2 * MLA_RANK]
    kpe = w_in[:, m0 + 2 * MLA_RANK:m0 + 2 * MLA_RANK + MLA_ROPE]
    gates = w_in[:, m0 + 2 * MLA_RANK + MLA_ROPE:]
    pad = jnp.zeros((D, 384 - 288), w_in.dtype)
    return jnp.concatenate([rkv, cq, ckv, gates, lora, pad, kpe, _swap_halves(kpe)], axis=1)


def _row(v):
    return v.reshape(1, -1).astype(F32)


def kernel(x, p, positions, pre_mix_norm, w_in, rw_mu, rw_w0, rw_w2, rw_a0, rw_a2, rw_g2, rw_k_k, rw_k_a, rw_r_k, rw_lnx_w, rw_lnx_b, mla_q_norm, mla_w_q_up, mla_kv_norm, mla_w_kv_up, w_branch_rw, w_branch_mla, w_out, post_mix_norm, pre_ffn_norm, w_up, conv_w, conv_b, w_down, post_ffn_norm, w_ple, w_ple_gate, ple_norm):
    B, S, D = x.shape
    T = B * S
    depth = w_in.shape[0]

    inv_freq = ROPE_BASE ** (-jnp.arange(0, MLA_ROPE, 2, dtype=F32) / MLA_ROPE)
    ang = positions.astype(F32)[..., None] * inv_freq
    cos = jnp.cos(ang).reshape(T, MLA_ROPE // 2)
    sin = jnp.sin(ang).reshape(T, MLA_ROPE // 2)
    zpad = jnp.zeros((T, LANES - MLA_ROPE), F32)
    cos_t = jnp.concatenate([cos, cos, zpad], axis=1)
    sin_t = jnp.concatenate([-sin, sin, zpad], axis=1)

    h = x.reshape(T, D)
    for i in range(depth):
        w_in_p = _bf(_pack_w_in(w_in[i]))
        z_all = _in_proj(h, _row(pre_mix_norm[i]), w_in_p)

        mu = rw_mu[i]
        mu_l = jnp.concatenate([mu[3 * RW_WIDTH:], jnp.zeros((512 - 288,), F32)]).reshape(1, 512)
        g2p = jnp.concatenate([rw_g2[i], jnp.zeros((256 - rw_g2.shape[1], RW_WIDTH), F32)], axis=0)
        w2p = jnp.concatenate([rw_w2[i]], axis=0)
        y_rw = _rwkv(z_all, S, _row(mu[:RW_WIDTH]), _row(mu[RW_WIDTH:2 * RW_WIDTH]), _row(mu[2 * RW_WIDTH:3 * RW_WIDTH]),
                     mu_l, _row(rw_w0[i]), _row(rw_a0[i]), _row(rw_k_k[i]), _row(rw_k_a[i]), _row(rw_r_k[i]),
                     _row(rw_lnx_w[i]), _row(rw_lnx_b[i]), _bf(w2p), _bf(rw_a2[i]), _bf(g2p))

        wq = mla_w_q_up[i].reshape(MLA_RANK, MLA_HEADS, MLA_NOPE + MLA_ROPE)
        wqn = wq[:, :, :MLA_NOPE].reshape(MLA_RANK, MLA_HEADS * MLA_NOPE)
        wq_pe = wq[:, :, MLA_NOPE:]
        padh = jnp.zeros((MLA_RANK, MLA_HEADS, LANES - MLA_ROPE), F32)
        wqp = jnp.concatenate([wq_pe, padh], axis=-1).reshape(MLA_RANK, MLA_HEADS * LANES)
        wqs = jnp.concatenate([_swap_halves(wq_pe), padh], axis=-1).reshape(MLA_RANK, MLA_HEADS * LANES)
        q, kv, kr = _mla_proj(z_all, cos_t, sin_t, _row(mla_q_norm[i]), _row(mla_kv_norm[i]),
                              _bf(wqn), _bf(wqp), _bf(wqs), _bf(mla_w_kv_up[i]))
        y_mla = _attention(q, kv, kr, S)

        h = _mix(y_rw, y_mla, z_all, h, _bf(w_branch_rw[i]), _bf(w_branch_mla[i]), _bf(w_out[i]), _row(post_mix_norm[i]))
        h = _ffn(h, S, _row(pre_ffn_norm[i]), _bf(w_up[i]), conv_w[i], _row(conv_b[i]), _bf(w_down[i]), _row(post_ffn_norm[i]))
        h = _ple(h, p[i].reshape(T, -1), _bf(w_ple_gate[i]), _bf(w_ple[i]), _row(ple_norm[i]))
    return h.reshape(B, S, D)
```

```python
import functools

import jax
import jax.numpy as jnp
from jax import lax
from jax.experimental import pallas as pl
from jax.experimental.pallas import tpu as pltpu

NORM_EPS = 1e-6
GN_EPS = 64e-5
CHUNK = 64
ROPE_BASE = 10000.0

RW_HEADS = 16
RW_N = 64
RW_WIDTH = RW_HEADS * RW_N
RW_L = 64
LANES = 128
RW_UNIT_HEADS = 2
RW_UNIT = RW_UNIT_HEADS * RW_N
RW_UNITS_PER_STEP = 8
RW_CHUNKS_PER_STEP = 2

MLA_HEADS = 8
MLA_NOPE = 128
MLA_ROPE = 64
MLA_V = 128
MLA_RANK = 512

VMEM_LIMIT_BYTES = 56 * 1024 * 1024

BF16 = jnp.bfloat16
F32 = jnp.float32


def _params(*sem):
    return pltpu.CompilerParams(dimension_semantics=sem, vmem_limit_bytes=VMEM_LIMIT_BYTES)


def _bf(x):
    return x.astype(BF16)


def _mm(a, b):
    return jnp.dot(_bf(a), _bf(b), preferred_element_type=F32)


def _mm_nt(a, b):
    return lax.dot_general(_bf(a), _bf(b), (((1,), (1,)), ((), ())), preferred_element_type=F32)


def _rms(x, g):
    return x * lax.rsqrt(jnp.mean(x * x, axis=-1, keepdims=True) + NORM_EPS) * g


def _inproj_kernel(x_ref, g_ref, w_ref, o_ref, xn_ref):
    @pl.when(pl.program_id(1) == 0)
    def _():
        xn_ref[...] = _bf(_rms(x_ref[...], g_ref[...]))

    o_ref[...] = jnp.dot(xn_ref[...], w_ref[...], preferred_element_type=F32)


def _in_proj(x2, g, w, tm=1024, tn=512):
    T, D = x2.shape
    N = w.shape[1]
    return pl.pallas_call(
        _inproj_kernel,
        grid=(T // tm, N // tn),
        in_specs=[
            pl.BlockSpec((tm, D), lambda i, j: (i, 0)),
            pl.BlockSpec((1, D), lambda i, j: (0, 0)),
            pl.BlockSpec((D, tn), lambda i, j: (0, j)),
        ],
        out_specs=pl.BlockSpec((tm, tn), lambda i, j: (i, j)),
        out_shape=jax.ShapeDtypeStruct((T, N), F32),
        scratch_shapes=[pltpu.VMEM((tm, D), BF16)],
        compiler_params=_params("parallel", "arbitrary"),
        name="in_proj",
    )(x2, g, w)


def _split3(x):
    hi = _bf(x)
    r1 = x - hi.astype(F32)
    mid = _bf(r1)
    lo = _bf(r1 - mid.astype(F32))
    return hi, mid, lo


def _rwkv_kernel(zr_ref, zk_ref, zv_ref, zl_ref, mur_ref, muk_ref, muv_ref, mul_ref,
                 w0_ref, a0_ref, kk_ref, ka_ref, rk_ref, lnw_ref, lnb_ref,
                 w2_ref, a2_ref, g2_ref, o_ref,
                 st_ref, pr_ref, pk_ref, pv_ref, plr_ref, *, units, chunks):
    L = RW_L
    UW = RW_UNIT
    c = pl.program_id(2)

    @pl.when(c == 0)
    def _():
        st_ref[...] = jnp.zeros_like(st_ref)
        pr_ref[...] = jnp.zeros_like(pr_ref)
        pk_ref[...] = jnp.zeros_like(pk_ref)
        pv_ref[...] = jnp.zeros_like(pv_ref)
        plr_ref[...] = jnp.zeros_like(plr_ref)

    def shift_lerp(z_ref, prev_ref, mu_ref):
        z = z_ref[...]
        rolled = pltpu.roll(z, 1, axis=0)
        row = lax.broadcasted_iota(jnp.int32, z.shape, 0)
        zs = jnp.where(row == 0, prev_ref[...], rolled)
        prev_ref[...] = z[z.shape[0] - 1:, :]
        return z + (zs - z) * mu_ref[...]

    r_all = shift_lerp(zr_ref, pr_ref, mur_ref)
    k_all = shift_lerp(zk_ref, pk_ref, muk_ref)
    v_all = shift_lerp(zv_ref, pv_ref, muv_ref)
    lo_all = shift_lerp(zl_ref, plr_ref, mul_ref)
    wd = jnp.tanh(lo_all[:, 0:64])
    ad = lo_all[:, 64:128]
    gd = jax.nn.sigmoid(lo_all[:, 128:384])

    hshift = RW_N.bit_length() - 1
    head_of_lane = lax.broadcasted_iota(jnp.int32, (1, UW), 1) >> hshift
    head_masks = [head_of_lane == j for j in range(RW_UNIT_HEADS)]
    row_l = lax.broadcasted_iota(jnp.int32, (L, UW), 0)
    col_l = lax.broadcasted_iota(jnp.int32, (L, UW), 1) & (RW_N - 1)
    strict = col_l < row_l
    incl = col_l <= row_l
    eye_pair = (col_l == row_l).astype(F32)
    row_s = lax.broadcasted_iota(jnp.int32, (UW, UW), 0)
    col_s = lax.broadcasted_iota(jnp.int32, (UW, UW), 1)
    same_head = (row_s >> hshift) == (col_s >> hshift)
    diag_s = row_s == col_s
    ones_bd = same_head.astype(BF16)

    def stack(x):
        return jnp.concatenate([jnp.where(m, x, 0.0) for m in head_masks], axis=0)

    def head_sum(x):
        return jnp.dot(_bf(x), ones_bd, preferred_element_type=F32)

    def cat(xs, axis):
        return jnp.concatenate(xs, axis=axis)

    U = range(units)
    uls = [slice(u * UW, (u + 1) * UW) for u in U]
    xw = [w0_ref[:, sl] + _mm(wd, w2_ref[:, sl]) for sl in uls]
    ga = [a0_ref[:, sl] + _mm(ad, a2_ref[:, sl]) for sl in uls]
    g_out = [_mm(gd, g2_ref[:, sl]) for sl in uls]
    kkv = [k_all[:, uls[u]] * kk_ref[:, uls[u]] for u in U]
    ss = [head_sum(x * x) for x in kkv]
    logw_u = [-0.6065306597126334 * jax.nn.sigmoid(x) for x in xw]
    gate_u = [jax.nn.sigmoid(x) for x in ga]
    kkn_u = [kkv[u] / jnp.maximum(jnp.sqrt(ss[u]), 1e-12) for u in U]
    k2_u = [k_all[:, uls[u]] * (1.0 + (gate_u[u] - 1.0) * ka_ref[:, uls[u]]) for u in U]
    bonus_in = [head_sum(r_all[:, uls[u]] * k2_u[u] * rk_ref[:, uls[u]]) for u in U]

    items = [(ci, u) for ci in range(chunks) for u in U]
    P = range(len(items))
    rows = [slice(ci * L, (ci + 1) * L) for ci, _ in items]
    r = [r_all[rows[p], uls[items[p][1]]] for p in P]
    v = [v_all[rows[p], uls[items[p][1]]] for p in P]
    logw = [logw_u[items[p][1]][rows[p]] for p in P]
    gate = [gate_u[items[p][1]][rows[p]] for p in P]
    kkn = [kkn_u[items[p][1]][rows[p]] for p in P]
    k2 = [k2_u[items[p][1]][rows[p]] for p in P]
    parts = [_split3(x) for x in logw]
    tri3 = ((lax.broadcasted_iota(jnp.int32, (L, 3 * L), 1) & (L - 1))
            <= lax.broadcasted_iota(jnp.int32, (L, 3 * L), 0)).astype(BF16)
    cum = [jnp.dot(tri3, cat(list(parts[p]), 0), preferred_element_type=F32) for p in P]
    cum_l = [c_[L - 1:L, :] for c_ in cum]
    e_neg = [jnp.exp(-c_) for c_ in cum]
    e_end = [jnp.exp(cum_l[p] - cum[p]) for p in P]
    kka = [kkn[p] * gate[p] for p in P]
    rt = [r[p] * jnp.exp(cum[p]) for p in P]
    kt = [k2[p] * e_neg[p] for p in P]
    bt = [kka[p] * e_neg[p] for p in P]
    at = [-kkn[p] * jnp.exp(cum[p] - logw[p]) for p in P]
    bk_t = [cat([kka[p] * e_end[p], k2[p] * e_end[p]], 0).T for p in P]
    w_end = [jnp.exp(c_) for c_ in cum_l]

    a_all = [_mm_nt(cat([at[p], rt[p]], 0), cat([stack(bt[p]), stack(kt[p])], 0)) for p in P]
    a_ab = [jnp.where(strict, a[:L, :UW], 0.0) for a in a_all]
    a_ak = [jnp.where(strict, a[:L, UW:], 0.0) for a in a_all]
    a_rb = [jnp.where(incl, a[L:, :UW], 0.0) for a in a_all]
    a_rk = [jnp.where(incl, a[L:, UW:], 0.0) for a in a_all]
    av = [_mm(a_ak[p], stack(v[p])) for p in P]

    tinv = [eye_pair + a for a in a_ab]
    pw = a_ab
    for _ in range(5):
        pw = [_mm(x, stack(x)) for x in pw]
        tinv = [tinv[p] + _mm(tinv[p], stack(pw[p])) for p in P]

    pq = [_mm(tinv[p], cat([stack(at[p]), stack(av[p])], 1)) for p in P]
    pm = [x[:, :UW] for x in pq]
    qm = [x[:, UW:] for x in pq]
    rm = [rt[p] + _mm(a_rb[p], stack(pm[p])) for p in P]
    y0 = [_mm(cat([a_rb[p], a_rk[p]], 1), cat([stack(qm[p]), stack(v[p])], 0)) for p in P]
    mn = [_mm(bk_t[p], cat([cat([pm[p], qm[p]], 1), cat([jnp.zeros_like(v[p]), v[p]], 1)], 0)) for p in P]
    m_mat = [jnp.where(diag_s, w_end[p], 0.0) + jnp.where(same_head, mn[p][:, :UW], 0.0) for p in P]
    n_mat = [jnp.where(same_head, mn[p][:, UW:], 0.0) for p in P]

    state = [st_ref[u] for u in U]
    y_chunks = []
    for ci in range(chunks):
        ps = [ci * units + u for u in U]
        ys = [_mm(cat([rm[p], m_mat[p]], 0), state[u]) for u, p in zip(U, ps)]
        state = [ys[u][L:] + n_mat[p] for u, p in zip(U, ps)]
        y_chunks.append([ys[u][:L] + y0[p] for u, p in zip(U, ps)])
    for u in U:
        st_ref[u] = state[u]

    y = [cat([y_chunks[ci][u] for ci in range(chunks)], 0) for u in U]
    mean = [head_sum(x) * (1.0 / RW_N) for x in y]
    yc = [y[u] - mean[u] for u in U]
    var = [head_sum(x * x) * (1.0 / RW_N) for x in yc]
    for u in U:
        yn = yc[u] * lax.rsqrt(var[u] + GN_EPS) * lnw_ref[:, uls[u]] + lnb_ref[:, uls[u]]
        o_ref[:, uls[u]] = _bf((yn + bonus_in[u] * v_all[:, uls[u]]) * g_out[u])


def _rwkv(z_all, S, mu_r, mu_k, mu_v, mu_l, w0, a0, kk, ka, rk, lnw, lnb, w2, a2, g2):
    T = z_all.shape[0]
    B = T // S
    R = RW_CHUNKS_PER_STEP * RW_L
    nc = S // R
    G = RW_UNITS_PER_STEP
    W = G * RW_UNIT
    ngrp = RW_WIDTH // W
    zspec = lambda off: pl.BlockSpec((R, W), lambda b, g, c, off=off: (b * nc + c, off * ngrp + g))
    vspec = pl.BlockSpec((1, W), lambda b, g, c: (0, g))
    lora_col = (z_all.shape[1] - 512) // 512
    return pl.pallas_call(
        functools.partial(_rwkv_kernel, units=G, chunks=RW_CHUNKS_PER_STEP),
        grid=(B, ngrp, nc),
        in_specs=[zspec(0), zspec(1), zspec(2),
                  pl.BlockSpec((R, 512), lambda b, g, c: (b * nc + c, lora_col)),
                  vspec, vspec, vspec,
                  pl.BlockSpec((1, 512), lambda b, g, c: (0, 0)),
                  vspec, vspec, vspec, vspec, vspec, vspec, vspec,
                  pl.BlockSpec((64, W), lambda b, g, c: (0, g)),
                  pl.BlockSpec((64, W), lambda b, g, c: (0, g)),
                  pl.BlockSpec((256, W), lambda b, g, c: (0, g))],
        out_specs=pl.BlockSpec((R, W), lambda b, g, c: (b * nc + c, g)),
        out_shape=jax.ShapeDtypeStruct((T, RW_WIDTH), BF16),
        scratch_shapes=[pltpu.VMEM((G, RW_UNIT, RW_UNIT), F32),
                        pltpu.VMEM((1, W), F32), pltpu.VMEM((1, W), F32), pltpu.VMEM((1, W), F32),
                        pltpu.VMEM((1, 512), F32)],
        compiler_params=_params("parallel", "parallel", "arbitrary"),
        name="rwkv",
    )(z_all, z_all, z_all, z_all, mu_r, mu_k, mu_v, mu_l, w0, a0, kk, ka, rk, lnw, lnb, w2, a2, g2)


def _mla_proj_kernel(cq_ref, ckv_ref, kpe_ref, cos_ref, sin_ref, qn_ref, kvn_ref,
                     wqn_ref, wqp_ref, wqs_ref, wkv_ref, q_ref, kv_ref, kr_ref, *, scale):
    cqn = _bf(_rms(cq_ref[...], qn_ref[...]))
    ckvn = _bf(_rms(ckv_ref[...], kvn_ref[...]))
    cos = cos_ref[...]
    sin = sin_ref[...]
    q_nope = jnp.dot(cqn, wqn_ref[...], preferred_element_type=F32)
    q_pe = jnp.dot(cqn, wqp_ref[...], preferred_element_type=F32)
    q_ps = jnp.dot(cqn, wqs_ref[...], preferred_element_type=F32)
    for h in range(MLA_HEADS):
        hs = slice(h * LANES, (h + 1) * LANES)
        q_ref[:, 2 * h * LANES:(2 * h + 1) * LANES] = _bf(q_nope[:, hs] * scale)
        q_ref[:, (2 * h + 1) * LANES:(2 * h + 2) * LANES] = _bf((q_pe[:, hs] * cos + q_ps[:, hs] * sin) * scale)
    kv_ref[...] = _bf(jnp.dot(ckvn, wkv_ref[...], preferred_element_type=F32))
    blk = kpe_ref[...]
    kr_ref[...] = _bf(blk * cos + pltpu.roll(blk, MLA_ROPE, axis=1) * sin)


def _mla_proj(z_all, cos_t, sin_t, qn, kvn, wqn, wqp, wqs, wkv, tm=512):
    T = z_all.shape[0]
    HW = MLA_HEADS * 2 * LANES
    scale = float((MLA_NOPE + MLA_ROPE) ** -0.5 * 1.4426950408889634)
    full = lambda a: pl.BlockSpec(a.shape, lambda i: (0, 0))
    kpe_blk = (z_all.shape[1] - LANES) // LANES
    return pl.pallas_call(
        functools.partial(_mla_proj_kernel, scale=scale),
        grid=(T // tm,),
        in_specs=[pl.BlockSpec((tm, MLA_RANK), lambda i: (i, 6)),
                  pl.BlockSpec((tm, MLA_RANK), lambda i: (i, 7)),
                  pl.BlockSpec((tm, LANES), lambda i: (i, kpe_blk)),
                  pl.BlockSpec((tm, LANES), lambda i: (i, 0)),
                  pl.BlockSpec((tm, LANES), lambda i: (i, 0)),
                  full(qn), full(kvn), full(wqn), full(wqp), full(wqs), full(wkv)],
        out_specs=[pl.BlockSpec((tm, HW), lambda i: (i, 0)),
                   pl.BlockSpec((tm, HW), lambda i: (i, 0)),
                   pl.BlockSpec((tm, LANES), lambda i: (i, 0))],
        out_shape=[jax.ShapeDtypeStruct((T, HW), BF16),
                   jax.ShapeDtypeStruct((T, HW), BF16),
                   jax.ShapeDtypeStruct((T, LANES), BF16)],
        compiler_params=_params("parallel"),
        name="mla_proj",
    )(z_all, z_all, z_all, cos_t, sin_t, qn, kvn, wqn, wqp, wqs, wkv)


ATT_HEADS_PER_STEP = 2


def _attn_kernel(qi_ref, ki_ref, q_ref, kv_ref, kr_ref, o_ref, m_ref, l_ref, acc_ref, *, tq, tk, heads):
    t = pl.program_id(2)
    qi = qi_ref[t]
    ki = ki_ref[t]
    H = range(heads)

    @pl.when(ki == 0)
    def _():
        m_ref[...] = jnp.full_like(m_ref, -1e30)
        l_ref[...] = jnp.zeros_like(l_ref)
        acc_ref[...] = jnp.zeros_like(acc_ref)

    def update(masked):
        kr = kr_ref[...]
        s = [lax.dot_general(q_ref[:, 2 * h * LANES:(2 * h + 2) * LANES],
                             jnp.concatenate([kv_ref[:, 2 * h * LANES:(2 * h + 1) * LANES], kr], axis=1),
                             (((1,), (1,)), ((), ())), preferred_element_type=F32) for h in H]
        if masked:
            q_last = (qi * tq + lax.broadcasted_iota(jnp.int32, (tq, tk), 0)) | (CHUNK - 1)
            k_pos = ki * tk + lax.broadcasted_iota(jnp.int32, (tq, tk), 1)
            vis = k_pos <= q_last
            s = [jnp.where(vis, x, -1e30) for x in s]
        m_prev = [m_ref[h] for h in H]
        m_new = [jnp.maximum(m_prev[h], jnp.max(s[h], axis=-1, keepdims=True)) for h in H]
        p = [jnp.exp2(s[h] - jnp.concatenate([m_new[h]] * (tk // LANES), axis=1)) for h in H]
        alpha = [jnp.exp2(m_prev[h] - m_new[h]) for h in H]
        ones = jnp.ones((tk, LANES), BF16)
        pv = [jnp.dot(_bf(p[h]), jnp.concatenate([kv_ref[:, (2 * h + 1) * LANES:(2 * h + 2) * LANES], ones], axis=1),
                      preferred_element_type=F32) for h in H]
        for h in H:
            l_ref[h] = alpha[h] * l_ref[h] + pv[h][:, LANES:]
            acc_ref[h] = alpha[h] * acc_ref[h] + pv[h][:, :LANES]
            m_ref[h] = m_new[h]

    @pl.when(ki < qi)
    def _():
        update(False)

    @pl.when(ki == qi)
    def _():
        update(True)
        for h in H:
            o_ref[:, h * LANES:(h + 1) * LANES] = _bf(acc_ref[h] / l_ref[h])


def _attention(q, kv, kr, S, tq=512):
    T = q.shape[0]
    B = T // S
    nq = S // tq
    G = ATT_HEADS_PER_STEP
    pairs = [(a, b) for a in range(nq) for b in range(a + 1)]
    qi_tab = jnp.asarray([a for a, _ in pairs], jnp.int32)
    ki_tab = jnp.asarray([b for _, b in pairs], jnp.int32)
    grid_spec = pltpu.PrefetchScalarGridSpec(
        num_scalar_prefetch=2,
        grid=(B, MLA_HEADS // G, len(pairs)),
        in_specs=[pl.BlockSpec((tq, 2 * G * LANES), lambda b, g, t, qt, kt: (b * nq + qt[t], g)),
                  pl.BlockSpec((tq, 2 * G * LANES), lambda b, g, t, qt, kt: (b * nq + kt[t], g)),
                  pl.BlockSpec((tq, LANES), lambda b, g, t, qt, kt: (b * nq + kt[t], 0))],
        out_specs=pl.BlockSpec((tq, G * LANES), lambda b, g, t, qt, kt: (b * nq + qt[t], g)),
        scratch_shapes=[pltpu.VMEM((G, tq, LANES), F32), pltpu.VMEM((G, tq, LANES), F32), pltpu.VMEM((G, tq, LANES), F32)])
    return pl.pallas_call(
        functools.partial(_attn_kernel, tq=tq, tk=tq, heads=G),
        grid_spec=grid_spec,
        out_shape=jax.ShapeDtypeStruct((T, MLA_HEADS * MLA_V), BF16),
        compiler_params=_params("parallel", "parallel", "arbitrary"),
        name="attention",
    )(qi_tab, ki_tab, q, kv, kr)


def _mix_kernel(yrw_ref, ymla_ref, gate_ref, x_ref, wa_ref, wb_ref, wo_ref, g_ref, o_ref):
    D = x_ref.shape[1]
    t = (jax.nn.sigmoid(gate_ref[:, :D]) * jnp.dot(yrw_ref[...], wa_ref[...], preferred_element_type=F32)
         + jax.nn.sigmoid(gate_ref[:, D:]) * jnp.dot(ymla_ref[...], wb_ref[...], preferred_element_type=F32))
    mix = jnp.dot(_bf(t), wo_ref[...], preferred_element_type=F32)
    o_ref[...] = x_ref[...] + _rms(mix, g_ref[...])


def _resident(a):
    return pl.BlockSpec(a.shape, lambda *_: (0,) * a.ndim, pipeline_mode=pl.Buffered(1))


def _mix(y_rw, y_mla, z_all, x2, wa, wb, wo, g, tm=256):
    T, D = x2.shape
    return pl.pallas_call(
        _mix_kernel,
        grid=(T // tm,),
        in_specs=[pl.BlockSpec((tm, y_rw.shape[1]), lambda i: (i, 0)),
                  pl.BlockSpec((tm, y_mla.shape[1]), lambda i: (i, 0)),
                  pl.BlockSpec((tm, 2 * D), lambda i: (i, 1)),
                  pl.BlockSpec((tm, D), lambda i: (i, 0)),
                  _resident(wa), _resident(wb), _resident(wo), _resident(g)],
        out_specs=pl.BlockSpec((tm, D), lambda i: (i, 0)),
        out_shape=jax.ShapeDtypeStruct((T, D), F32),
        compiler_params=_params("parallel"),
        name="mix",
    )(y_rw, y_mla, z_all, x2, wa, wb, wo, g)


HALO = 16
FFN_SUB = 256


def _gelu_tanh(x):
    return 0.5 * x * (1.0 + jnp.tanh(0.7978845608028654 * (x + 0.044715 * x * x * x)))


def _ffn_kernel(h_ref, halo_ref, gin_ref, wug_ref, wuv_ref, cwg_ref, cwv_ref, cbg_ref, cbv_ref,
                wd_ref, gout_ref, o_ref, xn_ref, upg_ref, upv_ref, *, tm, tiles_per_seq):
    i = pl.program_id(0)
    j = pl.program_id(1)

    @pl.when(j == 0)
    def _():
        xn_ref[0:HALO, :] = _bf(_rms(halo_ref[...], gin_ref[...]))
        xn_ref[HALO:, :] = _bf(_rms(h_ref[...], gin_ref[...]))
        o_ref[...] = jnp.zeros_like(o_ref)

    keep = jnp.where((i % tiles_per_seq) == 0, 0.0, 1.0).astype(F32)
    tf = upg_ref.shape[1]
    subs = [slice(c, c + FFN_SUB) for c in range(0, tf, FFN_SUB)]

    for cs in subs:
        for w_ref, up_ref in ((wug_ref, upg_ref), (wuv_ref, upv_ref)):
            up = jnp.dot(xn_ref[...], w_ref[:, cs], preferred_element_type=F32)
            up_ref[0:HALO, cs] = up[0:HALO] * keep
            up_ref[HALO:, cs] = up[HALO:]

    def conv(up_ref, cw_ref, cb_ref, cs):
        cw = cw_ref[:, cs]
        return (cw[0:1, :] * up_ref[HALO - 2:HALO - 2 + tm, cs] + cw[1:2, :] * up_ref[HALO - 1:HALO - 1 + tm, cs]
                + cw[2:3, :] * up_ref[HALO:HALO + tm, cs] + cb_ref[:, cs])

    act = [_bf(_gelu_tanh(conv(upg_ref, cwg_ref, cbg_ref, cs)) * conv(upv_ref, cwv_ref, cbv_ref, cs)) for cs in subs]
    o_ref[...] += jnp.dot(jnp.concatenate(act, axis=1), wd_ref[...], preferred_element_type=F32)

    @pl.when(j == pl.num_programs(1) - 1)
    def _():
        o_ref[...] = h_ref[...] + _rms(o_ref[...], gout_ref[...])


def _ffn(h1, S, gin, w_up, conv_w, conv_b, w_down, gout, tm=1024, tf=512):
    T, D = h1.shape
    F = w_down.shape[0]
    nf = F // tf
    hb = tm // HALO
    return pl.pallas_call(
        functools.partial(_ffn_kernel, tm=tm, tiles_per_seq=S // tm),
        grid=(T // tm, nf),
        in_specs=[pl.BlockSpec((tm, D), lambda i, j: (i, 0), pipeline_mode=pl.Buffered(1)),
                  pl.BlockSpec((HALO, D), lambda i, j: (jnp.maximum(i * hb - 1, 0), 0)),
                  pl.BlockSpec((1, D), lambda i, j: (0, 0)),
                  pl.BlockSpec((D, tf), lambda i, j: (0, j)),
                  pl.BlockSpec((D, tf), lambda i, j: (0, nf + j)),
                  pl.BlockSpec((3, tf), lambda i, j: (0, j)),
                  pl.BlockSpec((3, tf), lambda i, j: (0, nf + j)),
                  pl.BlockSpec((1, tf), lambda i, j: (0, j)),
                  pl.BlockSpec((1, tf), lambda i, j: (0, nf + j)),
                  pl.BlockSpec((tf, D), lambda i, j: (j, 0)),
                  pl.BlockSpec((1, D), lambda i, j: (0, 0))],
        out_specs=pl.BlockSpec((tm, D), lambda i, j: (i, 0)),
        out_shape=jax.ShapeDtypeStruct((T, D), F32),
        scratch_shapes=[pltpu.VMEM((tm + HALO, D), BF16),
                        pltpu.VMEM((tm + HALO, tf), F32), pltpu.VMEM((tm + HALO, tf), F32)],
        compiler_params=_params("parallel", "arbitrary"),
        name="ffn",
    )(h1, h1, gin, w_up, w_up, conv_w, conv_w, conv_b, conv_b, w_down, gout)


def _ple_kernel(h_ref, p_ref, wpg_ref, wple_ref, g_ref, o_ref):
    h = h_ref[...]
    gate = jax.nn.sigmoid(jnp.dot(_bf(h), wpg_ref[...], preferred_element_type=F32))
    e = jnp.dot(_bf(p_ref[...]), wple_ref[...], preferred_element_type=F32)
    o_ref[...] = h + _rms(gate * e, g_ref[...])


def _ple(h2, p2, wpg, wple, g, tm=512):
    T, D = h2.shape
    full = lambda a: pl.BlockSpec(a.shape, lambda i: (0, 0))
    return pl.pallas_call(
        _ple_kernel,
        grid=(T // tm,),
        in_specs=[pl.BlockSpec((tm, D), lambda i: (i, 0)),
                  pl.BlockSpec((tm, p2.shape[1]), lambda i: (i, 0)),
                  full(wpg), full(wple), full(g)],
        out_specs=pl.BlockSpec((tm, D), lambda i: (i, 0)),
        out_shape=jax.ShapeDtypeStruct((T, D), F32),
        compiler_params=_params("parallel"),
        name="ple",
    )(h2, p2, wpg, wple, g)


def _swap_halves(w):
    half = w.shape[-1] // 2
    return jnp.concatenate([w[..., half:], w[..., :half]], axis=-1)


RW_LORA = 288
W_IN_PACKED = 3 * RW_WIDTH + 2 * MLA_RANK + 4096 + 512


def _pack_w_in_kernel(w_ref, o_ref):
    w = w_ref[...]
    rkv_end = 3 * RW_WIDTH
    m0 = rkv_end + RW_LORA
    g0 = m0 + 2 * MLA_RANK + MLA_ROPE
    o_ref[:, :rkv_end] = _bf(w[:, :rkv_end])
    o_ref[:, rkv_end:rkv_end + 2 * MLA_RANK] = _bf(w[:, m0:m0 + 2 * MLA_RANK])
    o_ref[:, rkv_end + 2 * MLA_RANK:rkv_end + 2 * MLA_RANK + 4096] = _bf(w[:, g0:g0 + 4096])
    kpe = w[:, m0 + 2 * MLA_RANK:g0]
    tail = jnp.concatenate([w[:, rkv_end:m0], jnp.zeros((w.shape[0], 384 - RW_LORA), F32), kpe, _swap_halves(kpe)], axis=1)
    o_ref[:, rkv_end + 2 * MLA_RANK + 4096:] = _bf(tail)


def _pack_w_in(w_in, tr=128):
    D, N = w_in.shape
    return pl.pallas_call(
        _pack_w_in_kernel,
        grid=(D // tr,),
        in_specs=[pl.BlockSpec((tr, N), lambda i: (i, 0))],
        out_specs=pl.BlockSpec((tr, W_IN_PACKED), lambda i: (i, 0)),
        out_shape=jax.ShapeDtypeStruct((D, W_IN_PACKED), BF16),
        compiler_params=_params("parallel"),
        name="pack_w_in",
    )(w_in)


def _row(v):
    return v.reshape(1, -1).astype(F32)


def kernel(x, p, positions, pre_mix_norm, w_in, rw_mu, rw_w0, rw_w2, rw_a0, rw_a2, rw_g2, rw_k_k, rw_k_a, rw_r_k, rw_lnx_w, rw_lnx_b, mla_q_norm, mla_w_q_up, mla_kv_norm, mla_w_kv_up, w_branch_rw, w_branch_mla, w_out, post_mix_norm, pre_ffn_norm, w_up, conv_w, conv_b, w_down, post_ffn_norm, w_ple, w_ple_gate, ple_norm):
    B, S, D = x.shape
    T = B * S
    depth = w_in.shape[0]

    inv_freq = ROPE_BASE ** (-jnp.arange(0, MLA_ROPE, 2, dtype=F32) / MLA_ROPE)
    ang = positions.astype(F32)[..., None] * inv_freq
    cos = jnp.cos(ang).reshape(T, MLA_ROPE // 2)
    sin = jnp.sin(ang).reshape(T, MLA_ROPE // 2)
    zpad = jnp.zeros((T, LANES - MLA_ROPE), F32)
    cos_t = jnp.concatenate([cos, cos, zpad], axis=1)
    sin_t = jnp.concatenate([-sin, sin, zpad], axis=1)

    h = x.reshape(T, D)
    for i in range(depth):
        w_in_p = _pack_w_in(w_in[i])
        z_all = _in_proj(h, _row(pre_mix_norm[i]), w_in_p)

        mu = rw_mu[i]
        mu_l = jnp.concatenate([mu[3 * RW_WIDTH:], jnp.zeros((512 - 288,), F32)]).reshape(1, 512)
        g2p = jnp.concatenate([rw_g2[i], jnp.zeros((256 - rw_g2.shape[1], RW_WIDTH), F32)], axis=0)
        w2p = jnp.concatenate([rw_w2[i]], axis=0)
        y_rw = _rwkv(z_all, S, _row(mu[:RW_WIDTH]), _row(mu[RW_WIDTH:2 * RW_WIDTH]), _row(mu[2 * RW_WIDTH:3 * RW_WIDTH]),
                     mu_l, _row(rw_w0[i]), _row(rw_a0[i]), _row(rw_k_k[i]), _row(rw_k_a[i]), _row(rw_r_k[i]),
                     _row(rw_lnx_w[i]), _row(rw_lnx_b[i]), _bf(w2p), _bf(rw_a2[i]), _bf(g2p))

        wq = mla_w_q_up[i].reshape(MLA_RANK, MLA_HEADS, MLA_NOPE + MLA_ROPE)
        wqn = wq[:, :, :MLA_NOPE].reshape(MLA_RANK, MLA_HEADS * MLA_NOPE)
        wq_pe = wq[:, :, MLA_NOPE:]
        padh = jnp.zeros((MLA_RANK, MLA_HEADS, LANES - MLA_ROPE), F32)
        wqp = jnp.concatenate([wq_pe, padh], axis=-1).reshape(MLA_RANK, MLA_HEADS * LANES)
        wqs = jnp.concatenate([_swap_halves(wq_pe), padh], axis=-1).reshape(MLA_RANK, MLA_HEADS * LANES)
        q, kv, kr = _mla_proj(z_all, cos_t, sin_t, _row(mla_q_norm[i]), _row(mla_kv_norm[i]),
                              _bf(wqn), _bf(wqp), _bf(wqs), _bf(mla_w_kv_up[i]))
        y_mla = _attention(q, kv, kr, S)

        h = _mix(y_rw, y_mla, z_all, h, _bf(w_branch_rw[i]), _bf(w_branch_mla[i]), _bf(w_out[i]), _row(post_mix_norm[i]))
        h = _ffn(h, S, _row(pre_ffn_norm[i]), _bf(w_up[i]), conv_w[i], _row(conv_b[i]), _bf(w_down[i]), _row(post_ffn_norm[i]))
        h = _ple(h, p[i].reshape(T, -1), _bf(w_ple_gate[i]), _bf(w_ple[i]), _row(ple_norm[i]))
    return h.reshape(B, S, D)
```

```python
import functools

import jax
import jax.numpy as jnp
from jax import lax
from jax.experimental import pallas as pl
from jax.experimental.pallas import tpu as pltpu

NORM_EPS = 1e-6
GN_EPS = 64e-5
CHUNK = 64
ROPE_BASE = 10000.0

RW_HEADS = 16
RW_N = 64
RW_WIDTH = RW_HEADS * RW_N
RW_L = 64
LANES = 128
RW_UNIT_HEADS = 2
RW_UNIT = RW_UNIT_HEADS * RW_N
RW_UNITS_PER_STEP = 8
RW_CHUNKS_PER_STEP = 2

MLA_HEADS = 8
MLA_NOPE = 128
MLA_ROPE = 64
MLA_V = 128
MLA_RANK = 512

VMEM_LIMIT_BYTES = 56 * 1024 * 1024

BF16 = jnp.bfloat16
F32 = jnp.float32


def _params(*sem):
    return pltpu.CompilerParams(dimension_semantics=sem, vmem_limit_bytes=VMEM_LIMIT_BYTES)


def _bf(x):
    return x.astype(BF16)


def _mm(a, b):
    return jnp.dot(_bf(a), _bf(b), preferred_element_type=F32)


def _mm_nt(a, b):
    return lax.dot_general(_bf(a), _bf(b), (((1,), (1,)), ((), ())), preferred_element_type=F32)


def _rms(x, g):
    return x * lax.rsqrt(jnp.mean(x * x, axis=-1, keepdims=True) + NORM_EPS) * g


def _inproj_kernel(x_ref, g_ref, w_ref, o_ref, xn_ref):
    @pl.when(pl.program_id(1) == 0)
    def _():
        xn_ref[...] = _bf(_rms(x_ref[...], g_ref[...]))

    o_ref[...] = jnp.dot(xn_ref[...], w_ref[...], preferred_element_type=F32)


def _in_proj(x2, g, w, tm=1024, tn=512):
    T, D = x2.shape
    N = w.shape[1]
    return pl.pallas_call(
        _inproj_kernel,
        grid=(T // tm, N // tn),
        in_specs=[
            pl.BlockSpec((tm, D), lambda i, j: (i, 0)),
            pl.BlockSpec((1, D), lambda i, j: (0, 0)),
            pl.BlockSpec((D, tn), lambda i, j: (0, j)),
        ],
        out_specs=pl.BlockSpec((tm, tn), lambda i, j: (i, j)),
        out_shape=jax.ShapeDtypeStruct((T, N), F32),
        scratch_shapes=[pltpu.VMEM((tm, D), BF16)],
        compiler_params=_params("parallel", "arbitrary"),
        name="in_proj",
    )(x2, g, w)


def _split3(x):
    hi = _bf(x)
    r1 = x - hi.astype(F32)
    mid = _bf(r1)
    lo = _bf(r1 - mid.astype(F32))
    return hi, mid, lo


def _rwkv_kernel(zr_ref, zk_ref, zv_ref, zl_ref, mur_ref, muk_ref, muv_ref, mul_ref,
                 w0_ref, a0_ref, kk_ref, ka_ref, rk_ref, lnw_ref, lnb_ref,
                 w2_ref, a2_ref, g2_ref, o_ref,
                 st_ref, pr_ref, pk_ref, pv_ref, plr_ref, *, units, chunks):
    L = RW_L
    UW = RW_UNIT
    c = pl.program_id(2)

    @pl.when(c == 0)
    def _():
        st_ref[...] = jnp.zeros_like(st_ref)
        pr_ref[...] = jnp.zeros_like(pr_ref)
        pk_ref[...] = jnp.zeros_like(pk_ref)
        pv_ref[...] = jnp.zeros_like(pv_ref)
        plr_ref[...] = jnp.zeros_like(plr_ref)

    def shift_lerp(z_ref, prev_ref, mu_ref):
        z = z_ref[...]
        rolled = pltpu.roll(z, 1, axis=0)
        row = lax.broadcasted_iota(jnp.int32, z.shape, 0)
        zs = jnp.where(row == 0, prev_ref[...], rolled)
        prev_ref[...] = z[z.shape[0] - 1:, :]
        return z + (zs - z) * mu_ref[...]

    r_all = shift_lerp(zr_ref, pr_ref, mur_ref)
    k_all = shift_lerp(zk_ref, pk_ref, muk_ref)
    v_all = shift_lerp(zv_ref, pv_ref, muv_ref)
    lo_all = shift_lerp(zl_ref, plr_ref, mul_ref)
    wd = jnp.tanh(lo_all[:, 0:64])
    ad = lo_all[:, 64:128]
    gd = jax.nn.sigmoid(lo_all[:, 128:384])

    hshift = RW_N.bit_length() - 1
    head_of_lane = lax.broadcasted_iota(jnp.int32, (1, UW), 1) >> hshift
    head_masks = [head_of_lane == j for j in range(RW_UNIT_HEADS)]
    row_l = lax.broadcasted_iota(jnp.int32, (L, UW), 0)
    col_l = lax.broadcasted_iota(jnp.int32, (L, UW), 1) & (RW_N - 1)
    strict = col_l < row_l
    incl = col_l <= row_l
    eye_pair = (col_l == row_l).astype(F32)
    row_s = lax.broadcasted_iota(jnp.int32, (UW, UW), 0)
    col_s = lax.broadcasted_iota(jnp.int32, (UW, UW), 1)
    same_head = (row_s >> hshift) == (col_s >> hshift)
    diag_s = row_s == col_s
    ones_bd = same_head.astype(BF16)

    def stack(x):
        return jnp.concatenate([jnp.where(m, x, 0.0) for m in head_masks], axis=0)

    def head_sum(x):
        return jnp.dot(_bf(x), ones_bd, preferred_element_type=F32)

    def cat(xs, axis):
        return jnp.concatenate(xs, axis=axis)

    U = range(units)
    uls = [slice(u * UW, (u + 1) * UW) for u in U]
    xw = [w0_ref[:, sl] + _mm(wd, w2_ref[:, sl]) for sl in uls]
    ga = [a0_ref[:, sl] + _mm(ad, a2_ref[:, sl]) for sl in uls]
    g_out = [_mm(gd, g2_ref[:, sl]) for sl in uls]
    kkv = [k_all[:, uls[u]] * kk_ref[:, uls[u]] for u in U]
    ss = [head_sum(x * x) for x in kkv]
    logw_u = [-0.6065306597126334 * jax.nn.sigmoid(x) for x in xw]
    gate_u = [jax.nn.sigmoid(x) for x in ga]
    kkn_u = [kkv[u] / jnp.maximum(jnp.sqrt(ss[u]), 1e-12) for u in U]
    k2_u = [k_all[:, uls[u]] * (1.0 + (gate_u[u] - 1.0) * ka_ref[:, uls[u]]) for u in U]
    bonus_in = [head_sum(r_all[:, uls[u]] * k2_u[u] * rk_ref[:, uls[u]]) for u in U]

    items = [(ci, u) for ci in range(chunks) for u in U]
    P = range(len(items))
    rows = [slice(ci * L, (ci + 1) * L) for ci, _ in items]
    r = [r_all[rows[p], uls[items[p][1]]] for p in P]
    v = [v_all[rows[p], uls[items[p][1]]] for p in P]
    logw = [logw_u[items[p][1]][rows[p]] for p in P]
    gate = [gate_u[items[p][1]][rows[p]] for p in P]
    kkn = [kkn_u[items[p][1]][rows[p]] for p in P]
    k2 = [k2_u[items[p][1]][rows[p]] for p in P]
    parts = [_split3(x) for x in logw]
    tri3 = ((lax.broadcasted_iota(jnp.int32, (L, 3 * L), 1) & (L - 1))
            <= lax.broadcasted_iota(jnp.int32, (L, 3 * L), 0)).astype(BF16)
    cum = [jnp.dot(tri3, cat(list(parts[p]), 0), preferred_element_type=F32) for p in P]
    cum_l = [c_[L - 1:L, :] for c_ in cum]
    e_neg = [jnp.exp(-c_) for c_ in cum]
    e_end = [jnp.exp(cum_l[p] - cum[p]) for p in P]
    kka = [kkn[p] * gate[p] for p in P]
    rt = [r[p] * jnp.exp(cum[p]) for p in P]
    kt = [k2[p] * e_neg[p] for p in P]
    bt = [kka[p] * e_neg[p] for p in P]
    at = [-kkn[p] * jnp.exp(cum[p] - logw[p]) for p in P]
    bk_t = [cat([kka[p] * e_end[p], k2[p] * e_end[p]], 0).T for p in P]
    w_end = [jnp.exp(c_) for c_ in cum_l]

    a_all = [_mm_nt(cat([at[p], rt[p]], 0), cat([stack(bt[p]), stack(kt[p])], 0)) for p in P]
    a_ab = [jnp.where(strict, a[:L, :UW], 0.0) for a in a_all]
    a_ak = [jnp.where(strict, a[:L, UW:], 0.0) for a in a_all]
    a_rb = [jnp.where(incl, a[L:, :UW], 0.0) for a in a_all]
    a_rk = [jnp.where(incl, a[L:, UW:], 0.0) for a in a_all]
    av = [_mm(a_ak[p], stack(v[p])) for p in P]

    tinv = [eye_pair + a for a in a_ab]
    pw = a_ab
    for _ in range(5):
        pw = [_mm(x, stack(x)) for x in pw]
        tinv = [tinv[p] + _mm(tinv[p], stack(pw[p])) for p in P]

    pq = [_mm(tinv[p], cat([stack(at[p]), stack(av[p])], 1)) for p in P]
    pm = [x[:, :UW] for x in pq]
    qm = [x[:, UW:] for x in pq]
    rm = [rt[p] + _mm(a_rb[p], stack(pm[p])) for p in P]
    y0 = [_mm(cat([a_rb[p], a_rk[p]], 1), cat([stack(qm[p]), stack(v[p])], 0)) for p in P]
    mn = [_mm(bk_t[p], cat([cat([pm[p], qm[p]], 1), cat([jnp.zeros_like(v[p]), v[p]], 1)], 0)) for p in P]
    m_mat = [jnp.where(diag_s, w_end[p], 0.0) + jnp.where(same_head, mn[p][:, :UW], 0.0) for p in P]
    n_mat = [jnp.where(same_head, mn[p][:, UW:], 0.0) for p in P]

    state = [st_ref[u] for u in U]
    y_chunks = []
    for ci in range(chunks):
        ps = [ci * units + u for u in U]
        ys = [_mm(cat([rm[p], m_mat[p]], 0), state[u]) for u, p in zip(U, ps)]
        state = [ys[u][L:] + n_mat[p] for u, p in zip(U, ps)]
        y_chunks.append([ys[u][:L] + y0[p] for u, p in zip(U, ps)])
    for u in U:
        st_ref[u] = state[u]

    y = [cat([y_chunks[ci][u] for ci in range(chunks)], 0) for u in U]
    mean = [head_sum(x) * (1.0 / RW_N) for x in y]
    yc = [y[u] - mean[u] for u in U]
    var = [head_sum(x * x) * (1.0 / RW_N) for x in yc]
    for u in U:
        yn = yc[u] * lax.rsqrt(var[u] + GN_EPS) * lnw_ref[:, uls[u]] + lnb_ref[:, uls[u]]
        o_ref[:, uls[u]] = _bf((yn + bonus_in[u] * v_all[:, uls[u]]) * g_out[u])


def _rwkv(z_all, S, mu_r, mu_k, mu_v, mu_l, w0, a0, kk, ka, rk, lnw, lnb, w2, a2, g2):
    T = z_all.shape[0]
    B = T // S
    R = RW_CHUNKS_PER_STEP * RW_L
    nc = S // R
    G = RW_UNITS_PER_STEP
    W = G * RW_UNIT
    ngrp = RW_WIDTH // W
    zspec = lambda off: pl.BlockSpec((R, W), lambda b, g, c, off=off: (b * nc + c, off * ngrp + g))
    vspec = pl.BlockSpec((1, W), lambda b, g, c: (0, g))
    lora_col = (z_all.shape[1] - 512) // 512
    return pl.pallas_call(
        functools.partial(_rwkv_kernel, units=G, chunks=RW_CHUNKS_PER_STEP),
        grid=(B, ngrp, nc),
        in_specs=[zspec(0), zspec(1), zspec(2),
                  pl.BlockSpec((R, 512), lambda b, g, c: (b * nc + c, lora_col)),
                  vspec, vspec, vspec,
                  pl.BlockSpec((1, 512), lambda b, g, c: (0, 0)),
                  vspec, vspec, vspec, vspec, vspec, vspec, vspec,
                  pl.BlockSpec((64, W), lambda b, g, c: (0, g)),
                  pl.BlockSpec((64, W), lambda b, g, c: (0, g)),
                  pl.BlockSpec((256, W), lambda b, g, c: (0, g))],
        out_specs=pl.BlockSpec((R, W), lambda b, g, c: (b * nc + c, g)),
        out_shape=jax.ShapeDtypeStruct((T, RW_WIDTH), BF16),
        scratch_shapes=[pltpu.VMEM((G, RW_UNIT, RW_UNIT), F32),
                        pltpu.VMEM((1, W), F32), pltpu.VMEM((1, W), F32), pltpu.VMEM((1, W), F32),
                        pltpu.VMEM((1, 512), F32)],
        compiler_params=_params("parallel", "parallel", "arbitrary"),
        name="rwkv",
    )(z_all, z_all, z_all, z_all, mu_r, mu_k, mu_v, mu_l, w0, a0, kk, ka, rk, lnw, lnb, w2, a2, g2)


def _mla_proj_kernel(cq_ref, ckv_ref, kpe_ref, cos_ref, sin_ref, qn_ref, kvn_ref,
                     wqn_ref, wqp_ref, wqs_ref, wkv_ref, q_ref, kv_ref, kr_ref, *, scale):
    cqn = _bf(_rms(cq_ref[...], qn_ref[...]))
    ckvn = _bf(_rms(ckv_ref[...], kvn_ref[...]))
    cos = cos_ref[...]
    sin = sin_ref[...]
    q_nope = jnp.dot(cqn, wqn_ref[...], preferred_element_type=F32)
    q_pe = jnp.dot(cqn, wqp_ref[...], preferred_element_type=F32)
    q_ps = jnp.dot(cqn, wqs_ref[...], preferred_element_type=F32)
    for h in range(MLA_HEADS):
        hs = slice(h * LANES, (h + 1) * LANES)
        q_ref[:, 2 * h * LANES:(2 * h + 1) * LANES] = _bf(q_nope[:, hs] * scale)
        q_ref[:, (2 * h + 1) * LANES:(2 * h + 2) * LANES] = _bf((q_pe[:, hs] * cos + q_ps[:, hs] * sin) * scale)
    kv_ref[...] = _bf(jnp.dot(ckvn, wkv_ref[...], preferred_element_type=F32))
    blk = kpe_ref[...]
    kr_ref[...] = _bf(blk * cos + pltpu.roll(blk, MLA_ROPE, axis=1) * sin)


def _mla_proj(z_all, cos_t, sin_t, qn, kvn, wqn, wqp, wqs, wkv, tm=512):
    T = z_all.shape[0]
    HW = MLA_HEADS * 2 * LANES
    scale = float((MLA_NOPE + MLA_ROPE) ** -0.5 * 1.4426950408889634)
    full = lambda a: pl.BlockSpec(a.shape, lambda i: (0, 0))
    kpe_blk = (z_all.shape[1] - LANES) // LANES
    return pl.pallas_call(
        functools.partial(_mla_proj_kernel, scale=scale),
        grid=(T // tm,),
        in_specs=[pl.BlockSpec((tm, MLA_RANK), lambda i: (i, 6)),
                  pl.BlockSpec((tm, MLA_RANK), lambda i: (i, 7)),
                  pl.BlockSpec((tm, LANES), lambda i: (i, kpe_blk)),
                  pl.BlockSpec((tm, LANES), lambda i: (i, 0)),
                  pl.BlockSpec((tm, LANES), lambda i: (i, 0)),
                  full(qn), full(kvn), full(wqn), full(wqp), full(wqs), full(wkv)],
        out_specs=[pl.BlockSpec((tm, HW), lambda i: (i, 0)),
                   pl.BlockSpec((tm, HW), lambda i: (i, 0)),
                   pl.BlockSpec((tm, LANES), lambda i: (i, 0))],
        out_shape=[jax.ShapeDtypeStruct((T, HW), BF16),
                   jax.ShapeDtypeStruct((T, HW), BF16),
                   jax.ShapeDtypeStruct((T, LANES), BF16)],
        compiler_params=_params("parallel"),
        name="mla_proj",
    )(z_all, z_all, z_all, cos_t, sin_t, qn, kvn, wqn, wqp, wqs, wkv)


ATT_HEADS_PER_STEP = 8


def _attn_kernel(qi_ref, ki_ref, q_ref, kv_ref, kr_ref, o_ref, m_ref, l_ref, acc_ref, *, tq, tk, heads):
    t = pl.program_id(2)
    qi = qi_ref[t]
    ki = ki_ref[t]
    H = range(heads)

    @pl.when(ki == 0)
    def _():
        m_ref[...] = jnp.full_like(m_ref, -1e30)
        l_ref[...] = jnp.zeros_like(l_ref)
        acc_ref[...] = jnp.zeros_like(acc_ref)

    def update(masked):
        kr = kr_ref[...]
        s = [lax.dot_general(q_ref[:, 2 * h * LANES:(2 * h + 2) * LANES],
                             jnp.concatenate([kv_ref[:, 2 * h * LANES:(2 * h + 1) * LANES], kr], axis=1),
                             (((1,), (1,)), ((), ())), preferred_element_type=F32) for h in H]
        if masked:
            q_last = (qi * tq + lax.broadcasted_iota(jnp.int32, (tq, tk), 0)) | (CHUNK - 1)
            k_pos = ki * tk + lax.broadcasted_iota(jnp.int32, (tq, tk), 1)
            vis = k_pos <= q_last
            s = [jnp.where(vis, x, -1e30) for x in s]
        m_prev = [m_ref[h] for h in H]
        m_new = [jnp.maximum(m_prev[h], jnp.max(s[h], axis=-1, keepdims=True)) for h in H]
        p = [jnp.exp2(s[h] - jnp.concatenate([m_new[h]] * (tk // LANES), axis=1)) for h in H]
        alpha = [jnp.exp2(m_prev[h] - m_new[h]) for h in H]
        pv = [jnp.dot(_bf(p[h]), kv_ref[:, (2 * h + 1) * LANES:(2 * h + 2) * LANES], preferred_element_type=F32) for h in H]
        for h in H:
            psum = sum(p[h][:, c:c + LANES] for c in range(0, tk, LANES))
            l_ref[h] = alpha[h] * l_ref[h] + psum
            acc_ref[h] = alpha[h] * acc_ref[h] + pv[h]
            m_ref[h] = m_new[h]

    @pl.when(ki < qi)
    def _():
        update(False)

    @pl.when(ki == qi)
    def _():
        update(True)
        for h in H:
            o_ref[:, h * LANES:(h + 1) * LANES] = _bf(acc_ref[h] / jnp.sum(l_ref[h], axis=-1, keepdims=True))


def _attention(q, kv, kr, S, tq=512):
    T = q.shape[0]
    B = T // S
    nq = S // tq
    G = ATT_HEADS_PER_STEP
    pairs = [(a, b) for a in range(nq) for b in range(a + 1)]
    qi_tab = jnp.asarray([a for a, _ in pairs], jnp.int32)
    ki_tab = jnp.asarray([b for _, b in pairs], jnp.int32)
    grid_spec = pltpu.PrefetchScalarGridSpec(
        num_scalar_prefetch=2,
        grid=(B, MLA_HEADS // G, len(pairs)),
        in_specs=[pl.BlockSpec((tq, 2 * G * LANES), lambda b, g, t, qt, kt: (b * nq + qt[t], g)),
                  pl.BlockSpec((tq, 2 * G * LANES), lambda b, g, t, qt, kt: (b * nq + kt[t], g)),
                  pl.BlockSpec((tq, LANES), lambda b, g, t, qt, kt: (b * nq + kt[t], 0))],
        out_specs=pl.BlockSpec((tq, G * LANES), lambda b, g, t, qt, kt: (b * nq + qt[t], g)),
        scratch_shapes=[pltpu.VMEM((G, tq, LANES), F32), pltpu.VMEM((G, tq, LANES), F32), pltpu.VMEM((G, tq, LANES), F32)])
    return pl.pallas_call(
        functools.partial(_attn_kernel, tq=tq, tk=tq, heads=G),
        grid_spec=grid_spec,
        out_shape=jax.ShapeDtypeStruct((T, MLA_HEADS * MLA_V), BF16),
        compiler_params=_params("parallel", "parallel", "arbitrary"),
        name="attention",
    )(qi_tab, ki_tab, q, kv, kr)


def _mix_kernel(yrw_ref, ymla_ref, gate_ref, x_ref, wa_ref, wb_ref, wo_ref, g_ref, o_ref):
    D = x_ref.shape[1]
    t = (jax.nn.sigmoid(gate_ref[:, :D]) * jnp.dot(yrw_ref[...], wa_ref[...], preferred_element_type=F32)
         + jax.nn.sigmoid(gate_ref[:, D:]) * jnp.dot(ymla_ref[...], wb_ref[...], preferred_element_type=F32))
    mix = jnp.dot(_bf(t), wo_ref[...], preferred_element_type=F32)
    o_ref[...] = x_ref[...] + _rms(mix, g_ref[...])


def _resident(a):
    return pl.BlockSpec(a.shape, lambda *_: (0,) * a.ndim, pipeline_mode=pl.Buffered(1))


def _mix(y_rw, y_mla, z_all, x2, wa, wb, wo, g, tm=256):
    T, D = x2.shape
    return pl.pallas_call(
        _mix_kernel,
        grid=(T // tm,),
        in_specs=[pl.BlockSpec((tm, y_rw.shape[1]), lambda i: (i, 0)),
                  pl.BlockSpec((tm, y_mla.shape[1]), lambda i: (i, 0)),
                  pl.BlockSpec((tm, 2 * D), lambda i: (i, 1)),
                  pl.BlockSpec((tm, D), lambda i: (i, 0)),
                  _resident(wa), _resident(wb), _resident(wo), _resident(g)],
        out_specs=pl.BlockSpec((tm, D), lambda i: (i, 0)),
        out_shape=jax.ShapeDtypeStruct((T, D), F32),
        compiler_params=_params("parallel"),
        name="mix",
    )(y_rw, y_mla, z_all, x2, wa, wb, wo, g)


HALO = 16
FFN_SUB = 256
FFN_ROW_BLOCKS = 4


def _gelu_tanh(x):
    return 0.5 * x * (1.0 + jnp.tanh(0.7978845608028654 * (x + 0.044715 * x * x * x)))


def _ffn_kernel(h_ref, halo_ref, gin_ref, wug_ref, wuv_ref, cwg_ref, cwv_ref, cbg_ref, cbv_ref,
                wd_ref, gout_ref, o_ref, xn_ref, upg_ref, upv_ref, *, tm, tiles_per_seq):
    i = pl.program_id(0)
    j = pl.program_id(1)

    @pl.when(j == 0)
    def _():
        xn_ref[0:HALO, :] = _bf(_rms(halo_ref[...], gin_ref[...]))
        xn_ref[HALO:, :] = _bf(_rms(h_ref[...], gin_ref[...]))
        o_ref[...] = jnp.zeros_like(o_ref)

    keep = jnp.where((i % tiles_per_seq) == 0, 0.0, 1.0).astype(F32)
    tf = upg_ref.shape[1]
    subs = [slice(c, c + FFN_SUB) for c in range(0, tf, FFN_SUB)]

    for cs in subs:
        for w_ref, up_ref in ((wug_ref, upg_ref), (wuv_ref, upv_ref)):
            up = jnp.dot(xn_ref[...], w_ref[:, cs], preferred_element_type=F32)
            up_ref[0:HALO, cs] = up[0:HALO] * keep
            up_ref[HALO:, cs] = up[HALO:]

    def conv(up_ref, cw_ref, cb_ref, cs, r0, nr):
        cw = cw_ref[:, cs]
        return (cw[0:1, :] * up_ref[HALO - 2 + r0:HALO - 2 + r0 + nr, cs]
                + cw[1:2, :] * up_ref[HALO - 1 + r0:HALO - 1 + r0 + nr, cs]
                + cw[2:3, :] * up_ref[HALO + r0:HALO + r0 + nr, cs] + cb_ref[:, cs])

    def act(cs, r0, nr):
        return _bf(_gelu_tanh(conv(upg_ref, cwg_ref, cbg_ref, cs, r0, nr)) * conv(upv_ref, cwv_ref, cbv_ref, cs, r0, nr))

    head = [act(cs, 0, tm) for cs in subs[:-1]]
    nr = tm // FFN_ROW_BLOCKS
    for r0 in range(0, tm, nr):
        a = jnp.concatenate([x[r0:r0 + nr] for x in head] + [act(subs[-1], r0, nr)], axis=1)
        o_ref[r0:r0 + nr, :] += jnp.dot(a, wd_ref[...], preferred_element_type=F32)

    @pl.when(j == pl.num_programs(1) - 1)
    def _():
        o_ref[...] = h_ref[...] + _rms(o_ref[...], gout_ref[...])


def _ffn(h1, S, gin, w_up, conv_w, conv_b, w_down, gout, tm=1024, tf=512):
    T, D = h1.shape
    F = w_down.shape[0]
    nf = F // tf
    hb = tm // HALO
    return pl.pallas_call(
        functools.partial(_ffn_kernel, tm=tm, tiles_per_seq=S // tm),
        grid=(T // tm, nf),
        in_specs=[pl.BlockSpec((tm, D), lambda i, j: (i, 0), pipeline_mode=pl.Buffered(1)),
                  pl.BlockSpec((HALO, D), lambda i, j: (jnp.maximum(i * hb - 1, 0), 0)),
                  pl.BlockSpec((1, D), lambda i, j: (0, 0)),
                  pl.BlockSpec((D, tf), lambda i, j: (0, j)),
                  pl.BlockSpec((D, tf), lambda i, j: (0, nf + j)),
                  pl.BlockSpec((3, tf), lambda i, j: (0, j)),
                  pl.BlockSpec((3, tf), lambda i, j: (0, nf + j)),
                  pl.BlockSpec((1, tf), lambda i, j: (0, j)),
                  pl.BlockSpec((1, tf), lambda i, j: (0, nf + j)),
                  pl.BlockSpec((tf, D), lambda i, j: (j, 0)),
                  pl.BlockSpec((1, D), lambda i, j: (0, 0))],
        out_specs=pl.BlockSpec((tm, D), lambda i, j: (i, 0)),
        out_shape=jax.ShapeDtypeStruct((T, D), F32),
        scratch_shapes=[pltpu.VMEM((tm + HALO, D), BF16),
                        pltpu.VMEM((tm + HALO, tf), F32), pltpu.VMEM((tm + HALO, tf), F32)],
        compiler_params=_params("parallel", "arbitrary"),
        name="ffn",
    )(h1, h1, gin, w_up, w_up, conv_w, conv_w, conv_b, conv_b, w_down, gout)


def _ple_kernel(h_ref, p_ref, wpg_ref, wple_ref, g_ref, o_ref):
    h = h_ref[...]
    gate = jax.nn.sigmoid(jnp.dot(_bf(h), wpg_ref[...], preferred_element_type=F32))
    e = jnp.dot(_bf(p_ref[...]), wple_ref[...], preferred_element_type=F32)
    o_ref[...] = h + _rms(gate * e, g_ref[...])


def _ple(h2, p2, wpg, wple, g, tm=512):
    T, D = h2.shape
    full = lambda a: pl.BlockSpec(a.shape, lambda i: (0, 0))
    return pl.pallas_call(
        _ple_kernel,
        grid=(T // tm,),
        in_specs=[pl.BlockSpec((tm, D), lambda i: (i, 0)),
                  pl.BlockSpec((tm, p2.shape[1]), lambda i: (i, 0)),
                  full(wpg), full(wple), full(g)],
        out_specs=pl.BlockSpec((tm, D), lambda i: (i, 0)),
        out_shape=jax.ShapeDtypeStruct((T, D), F32),
        compiler_params=_params("parallel"),
        name="ple",
    )(h2, p2, wpg, wple, g)


def _swap_halves(w):
    half = w.shape[-1] // 2
    return jnp.concatenate([w[..., half:], w[..., :half]], axis=-1)


RW_LORA = 288
W_IN_PACKED = 3 * RW_WIDTH + 2 * MLA_RANK + 4096 + 512


def _pack_w_in_kernel(w_ref, o_ref):
    w = w_ref[...]
    rkv_end = 3 * RW_WIDTH
    m0 = rkv_end + RW_LORA
    g0 = m0 + 2 * MLA_RANK + MLA_ROPE
    o_ref[:, :rkv_end] = _bf(w[:, :rkv_end])
    o_ref[:, rkv_end:rkv_end + 2 * MLA_RANK] = _bf(w[:, m0:m0 + 2 * MLA_RANK])
    o_ref[:, rkv_end + 2 * MLA_RANK:rkv_end + 2 * MLA_RANK + 4096] = _bf(w[:, g0:g0 + 4096])
    kpe = w[:, m0 + 2 * MLA_RANK:g0]
    tail = jnp.concatenate([w[:, rkv_end:m0], jnp.zeros((w.shape[0], 384 - RW_LORA), F32), kpe, _swap_halves(kpe)], axis=1)
    o_ref[:, rkv_end + 2 * MLA_RANK + 4096:] = _bf(tail)


def _pack_w_in(w_in, layer, tr=128):
    _, D, N = w_in.shape
    return pl.pallas_call(
        _pack_w_in_kernel,
        grid=(D // tr,),
        in_specs=[pl.BlockSpec((None, tr, N), lambda i: (layer, i, 0))],
        out_specs=pl.BlockSpec((tr, W_IN_PACKED), lambda i: (i, 0)),
        out_shape=jax.ShapeDtypeStruct((D, W_IN_PACKED), BF16),
        compiler_params=_params("parallel"),
        name="pack_w_in",
    )(w_in)


def _row(v):
    return v.reshape(1, -1).astype(F32)


def kernel(x, p, positions, pre_mix_norm, w_in, rw_mu, rw_w0, rw_w2, rw_a0, rw_a2, rw_g2, rw_k_k, rw_k_a, rw_r_k, rw_lnx_w, rw_lnx_b, mla_q_norm, mla_w_q_up, mla_kv_norm, mla_w_kv_up, w_branch_rw, w_branch_mla, w_out, post_mix_norm, pre_ffn_norm, w_up, conv_w, conv_b, w_down, post_ffn_norm, w_ple, w_ple_gate, ple_norm):
    B, S, D = x.shape
    T = B * S
    depth = w_in.shape[0]

    inv_freq = ROPE_BASE ** (-jnp.arange(0, MLA_ROPE, 2, dtype=F32) / MLA_ROPE)
    ang = positions.astype(F32)[..., None] * inv_freq
    cos = jnp.cos(ang).reshape(T, MLA_ROPE // 2)
    sin = jnp.sin(ang).reshape(T, MLA_ROPE // 2)
    zpad = jnp.zeros((T, LANES - MLA_ROPE), F32)
    cos_t = jnp.concatenate([cos, cos, zpad], axis=1)
    sin_t = jnp.concatenate([-sin, sin, zpad], axis=1)

    h = x.reshape(T, D)
    for i in range(depth):
        w_in_p = _pack_w_in(w_in, i)
        z_all = _in_proj(h, _row(pre_mix_norm[i]), w_in_p)

        mu = rw_mu[i]
        mu_l = jnp.concatenate([mu[3 * RW_WIDTH:], jnp.zeros((512 - 288,), F32)]).reshape(1, 512)
        g2p = jnp.concatenate([rw_g2[i], jnp.zeros((256 - rw_g2.shape[1], RW_WIDTH), F32)], axis=0)
        w2p = jnp.concatenate([rw_w2[i]], axis=0)
        y_rw = _rwkv(z_all, S, _row(mu[:RW_WIDTH]), _row(mu[RW_WIDTH:2 * RW_WIDTH]), _row(mu[2 * RW_WIDTH:3 * RW_WIDTH]),
                     mu_l, _row(rw_w0[i]), _row(rw_a0[i]), _row(rw_k_k[i]), _row(rw_k_a[i]), _row(rw_r_k[i]),
                     _row(rw_lnx_w[i]), _row(rw_lnx_b[i]), _bf(w2p), _bf(rw_a2[i]), _bf(g2p))

        wq = mla_w_q_up[i].reshape(MLA_RANK, MLA_HEADS, MLA_NOPE + MLA_ROPE)
        wqn = wq[:, :, :MLA_NOPE].reshape(MLA_RANK, MLA_HEADS * MLA_NOPE)
        wq_pe = wq[:, :, MLA_NOPE:]
        padh = jnp.zeros((MLA_RANK, MLA_HEADS, LANES - MLA_ROPE), F32)
        wqp = jnp.concatenate([wq_pe, padh], axis=-1).reshape(MLA_RANK, MLA_HEADS * LANES)
        wqs = jnp.concatenate([_swap_halves(wq_pe), padh], axis=-1).reshape(MLA_RANK, MLA_HEADS * LANES)
        q, kv, kr = _mla_proj(z_all, cos_t, sin_t, _row(mla_q_norm[i]), _row(mla_kv_norm[i]),
                              _bf(wqn), _bf(wqp), _bf(wqs), _bf(mla_w_kv_up[i]))
        y_mla = _attention(q, kv, kr, S)

        h = _mix(y_rw, y_mla, z_all, h, _bf(w_branch_rw[i]), _bf(w_branch_mla[i]), _bf(w_out[i]), _row(post_mix_norm[i]))
        h = _ffn(h, S, _row(pre_ffn_norm[i]), _bf(w_up[i]), conv_w[i], _row(conv_b[i]), _bf(w_down[i]), _row(post_ffn_norm[i]))
        h = _ple(h, p[i].reshape(T, -1), _bf(w_ple_gate[i]), _bf(w_ple[i]), _row(ple_norm[i]))
    return h.reshape(B, S, D)
```

```python
import functools

import jax
import jax.numpy as jnp
from jax import lax
from jax.experimental import pallas as pl
from jax.experimental.pallas import tpu as pltpu

NORM_EPS = 1e-6
GN_EPS = 64e-5
CHUNK = 64
ROPE_BASE = 10000.0

RW_HEADS = 16
RW_N = 64
RW_WIDTH = RW_HEADS * RW_N
RW_L = 64
LANES = 128
RW_UNIT_HEADS = 2
RW_UNIT = RW_UNIT_HEADS * RW_N
RW_UNITS_PER_STEP = 8
RW_CHUNKS_PER_STEP = 2

MLA_HEADS = 8
MLA_NOPE = 128
MLA_ROPE = 64
MLA_V = 128
MLA_RANK = 512

RW_LORA = 288
W_IN_TILE = 512
W_IN_ALIGN = 32
W_IN_PACKED = 3 * RW_WIDTH + 2 * MLA_RANK + 4096 + W_IN_TILE

VMEM_LIMIT_BYTES = 56 * 1024 * 1024

BF16 = jnp.bfloat16
F32 = jnp.float32


def _params(*sem):
    return pltpu.CompilerParams(dimension_semantics=sem, vmem_limit_bytes=VMEM_LIMIT_BYTES)


def _bf(x):
    return x.astype(BF16)


def _mm(a, b):
    return jnp.dot(_bf(a), _bf(b), preferred_element_type=F32)


def _mm_nt(a, b):
    return lax.dot_general(_bf(a), _bf(b), (((1,), (1,)), ((), ())), preferred_element_type=F32)


def _rms(x, g):
    return x * lax.rsqrt(jnp.mean(x * x, axis=-1, keepdims=True) + NORM_EPS) * g


def _inproj_kernel(x_ref, g_ref, w_ref, o_ref, xn_ref):
    @pl.when(pl.program_id(1) == 0)
    def _():
        xn_ref[...] = _bf(_rms(x_ref[...], g_ref[...]))

    o_ref[...] = lax.dot_general(xn_ref[...], w_ref[...], (((1,), (1,)), ((), ())), preferred_element_type=F32)


def _in_proj(x2, g, w_t, tm=2048, tn=W_IN_TILE):
    T, D = x2.shape
    N = w_t.shape[0]
    return pl.pallas_call(
        _inproj_kernel,
        grid=(T // tm, N // tn),
        in_specs=[
            pl.BlockSpec((tm, D), lambda i, j: (i, 0), pipeline_mode=pl.Buffered(1)),
            pl.BlockSpec((1, D), lambda i, j: (0, 0)),
            pl.BlockSpec((tn, D), lambda i, j: (j, 0)),
        ],
        out_specs=pl.BlockSpec((tm, tn), lambda i, j: (i, j)),
        out_shape=jax.ShapeDtypeStruct((T, N), F32),
        scratch_shapes=[pltpu.VMEM((tm, D), BF16)],
        compiler_params=_params("parallel", "arbitrary"),
        name="in_proj",
    )(x2, g, w_t)


def _split3(x):
    hi = _bf(x)
    r1 = x - hi.astype(F32)
    mid = _bf(r1)
    lo = _bf(r1 - mid.astype(F32))
    return hi, mid, lo


def _rwkv_kernel(zr_ref, zk_ref, zv_ref, zl_ref, mur_ref, muk_ref, muv_ref, mul_ref,
                 w0_ref, a0_ref, kk_ref, ka_ref, rk_ref, lnw_ref, lnb_ref,
                 w2_ref, a2_ref, g2_ref, o_ref,
                 st_ref, pr_ref, pk_ref, pv_ref, plr_ref, *, units, chunks):
    L = RW_L
    UW = RW_UNIT
    c = pl.program_id(2)

    @pl.when(c == 0)
    def _():
        st_ref[...] = jnp.zeros_like(st_ref)
        pr_ref[...] = jnp.zeros_like(pr_ref)
        pk_ref[...] = jnp.zeros_like(pk_ref)
        pv_ref[...] = jnp.zeros_like(pv_ref)
        plr_ref[...] = jnp.zeros_like(plr_ref)

    def shift_lerp(z_ref, prev_ref, mu_ref):
        z = z_ref[...]
        rolled = pltpu.roll(z, 1, axis=0)
        row = lax.broadcasted_iota(jnp.int32, z.shape, 0)
        zs = jnp.where(row == 0, prev_ref[...], rolled)
        prev_ref[...] = z[z.shape[0] - 1:, :]
        return z + (zs - z) * mu_ref[...]

    r_all = shift_lerp(zr_ref, pr_ref, mur_ref)
    k_all = shift_lerp(zk_ref, pk_ref, muk_ref)
    v_all = shift_lerp(zv_ref, pv_ref, muv_ref)
    lo_all = shift_lerp(zl_ref, plr_ref, mul_ref)
    wd = jnp.tanh(lo_all[:, 0:64])
    ad = lo_all[:, 64:128]
    gd = jax.nn.sigmoid(lo_all[:, 128:384])

    hshift = RW_N.bit_length() - 1
    head_of_lane = lax.broadcasted_iota(jnp.int32, (1, UW), 1) >> hshift
    head_masks = [head_of_lane == j for j in range(RW_UNIT_HEADS)]
    row_l = lax.broadcasted_iota(jnp.int32, (L, UW), 0)
    col_l = lax.broadcasted_iota(jnp.int32, (L, UW), 1) & (RW_N - 1)
    strict = col_l < row_l
    incl = col_l <= row_l
    eye_pair = (col_l == row_l).astype(F32)
    row_s = lax.broadcasted_iota(jnp.int32, (UW, UW), 0)
    col_s = lax.broadcasted_iota(jnp.int32, (UW, UW), 1)
    same_head = (row_s >> hshift) == (col_s >> hshift)
    diag_s = row_s == col_s
    ones_bd = same_head.astype(BF16)

    def stack(x):
        return jnp.concatenate([jnp.where(m, x, 0.0) for m in head_masks], axis=0)

    def head_sum(x):
        return jnp.dot(_bf(x), ones_bd, preferred_element_type=F32)

    def cat(xs, axis):
        return jnp.concatenate(xs, axis=axis)

    U = range(units)
    uls = [slice(u * UW, (u + 1) * UW) for u in U]
    xw = [w0_ref[:, sl] + _mm(wd, w2_ref[:, sl]) for sl in uls]
    ga = [a0_ref[:, sl] + _mm(ad, a2_ref[:, sl]) for sl in uls]
    g_out = [_mm(gd, g2_ref[:, sl]) for sl in uls]
    kkv = [k_all[:, uls[u]] * kk_ref[:, uls[u]] for u in U]
    ss = [head_sum(x * x) for x in kkv]
    logw_u = [-0.6065306597126334 * jax.nn.sigmoid(x) for x in xw]
    gate_u = [jax.nn.sigmoid(x) for x in ga]
    kkn_u = [kkv[u] / jnp.maximum(jnp.sqrt(ss[u]), 1e-12) for u in U]
    k2_u = [k_all[:, uls[u]] * (1.0 + (gate_u[u] - 1.0) * ka_ref[:, uls[u]]) for u in U]
    bonus_in = [head_sum(r_all[:, uls[u]] * k2_u[u] * rk_ref[:, uls[u]]) for u in U]

    items = [(ci, u) for ci in range(chunks) for u in U]
    P = range(len(items))
    rows = [slice(ci * L, (ci + 1) * L) for ci, _ in items]
    r = [r_all[rows[p], uls[items[p][1]]] for p in P]
    v = [v_all[rows[p], uls[items[p][1]]] for p in P]
    logw = [logw_u[items[p][1]][rows[p]] for p in P]
    gate = [gate_u[items[p][1]][rows[p]] for p in P]
    kkn = [kkn_u[items[p][1]][rows[p]] for p in P]
    k2 = [k2_u[items[p][1]][rows[p]] for p in P]
    parts = [_split3(x) for x in logw]
    tri3 = ((lax.broadcasted_iota(jnp.int32, (L, 3 * L), 1) & (L - 1))
            <= lax.broadcasted_iota(jnp.int32, (L, 3 * L), 0)).astype(BF16)
    cum = [jnp.dot(tri3, cat(list(parts[p]), 0), preferred_element_type=F32) for p in P]
    cum_l = [c_[L - 1:L, :] for c_ in cum]
    e_neg = [jnp.exp(-c_) for c_ in cum]
    e_end = [jnp.exp(cum_l[p] - cum[p]) for p in P]
    kka = [kkn[p] * gate[p] for p in P]
    rt = [r[p] * jnp.exp(cum[p]) for p in P]
    kt = [k2[p] * e_neg[p] for p in P]
    bt = [kka[p] * e_neg[p] for p in P]
    at = [-kkn[p] * jnp.exp(cum[p] - logw[p]) for p in P]
    bk_t = [cat([kka[p] * e_end[p], k2[p] * e_end[p]], 0).T for p in P]
    w_end = [jnp.exp(c_) for c_ in cum_l]

    a_all = [_mm_nt(cat([at[p], rt[p]], 0), cat([stack(bt[p]), stack(kt[p])], 0)) for p in P]
    a_ab = [jnp.where(strict, a[:L, :UW], 0.0) for a in a_all]
    a_ak = [jnp.where(strict, a[:L, UW:], 0.0) for a in a_all]
    a_rb = [jnp.where(incl, a[L:, :UW], 0.0) for a in a_all]
    a_rk = [jnp.where(incl, a[L:, UW:], 0.0) for a in a_all]
    av = [_mm(a_ak[p], stack(v[p])) for p in P]

    tinv = [eye_pair + a for a in a_ab]
    pw = a_ab
    for _ in range(5):
        pw = [_mm(x, stack(x)) for x in pw]
        tinv = [tinv[p] + _mm(tinv[p], stack(pw[p])) for p in P]

    pq = [_mm(tinv[p], cat([stack(at[p]), stack(av[p])], 1)) for p in P]
    pm = [x[:, :UW] for x in pq]
    qm = [x[:, UW:] for x in pq]
    rm = [rt[p] + _mm(a_rb[p], stack(pm[p])) for p in P]
    y0 = [_mm(cat([a_rb[p], a_rk[p]], 1), cat([stack(qm[p]), stack(v[p])], 0)) for p in P]
    mn = [_mm(bk_t[p], cat([cat([pm[p], qm[p]], 1), cat([jnp.zeros_like(v[p]), v[p]], 1)], 0)) for p in P]
    m_mat = [jnp.where(diag_s, w_end[p], 0.0) + jnp.where(same_head, mn[p][:, :UW], 0.0) for p in P]
    n_mat = [jnp.where(same_head, mn[p][:, UW:], 0.0) for p in P]

    state = [st_ref[u] for u in U]
    y_chunks = []
    for ci in range(chunks):
        ps = [ci * units + u for u in U]
        ys = [_mm(cat([rm[p], m_mat[p]], 0), state[u]) for u, p in zip(U, ps)]
        state = [ys[u][L:] + n_mat[p] for u, p in zip(U, ps)]
        y_chunks.append([ys[u][:L] + y0[p] for u, p in zip(U, ps)])
    for u in U:
        st_ref[u] = state[u]

    y = [cat([y_chunks[ci][u] for ci in range(chunks)], 0) for u in U]
    mean = [head_sum(x) * (1.0 / RW_N) for x in y]
    yc = [y[u] - mean[u] for u in U]
    var = [head_sum(x * x) * (1.0 / RW_N) for x in yc]
    for u in U:
        yn = yc[u] * lax.rsqrt(var[u] + GN_EPS) * lnw_ref[:, uls[u]] + lnb_ref[:, uls[u]]
        o_ref[:, uls[u]] = _bf((yn + bonus_in[u] * v_all[:, uls[u]]) * g_out[u])


def _rwkv(z_all, S, mu_r, mu_k, mu_v, mu_l, w0, a0, kk, ka, rk, lnw, lnb, w2, a2, g2):
    T = z_all.shape[0]
    B = T // S
    R = RW_CHUNKS_PER_STEP * RW_L
    nc = S // R
    G = RW_UNITS_PER_STEP
    W = G * RW_UNIT
    ngrp = RW_WIDTH // W
    zspec = lambda off: pl.BlockSpec((R, W), lambda b, g, c, off=off: (b * nc + c, off * ngrp + g))
    vspec = pl.BlockSpec((1, W), lambda b, g, c: (0, g))
    lora_col = (z_all.shape[1] - 512) // 512
    return pl.pallas_call(
        functools.partial(_rwkv_kernel, units=G, chunks=RW_CHUNKS_PER_STEP),
        grid=(B, ngrp, nc),
        in_specs=[zspec(0), zspec(1), zspec(2),
                  pl.BlockSpec((R, 512), lambda b, g, c: (b * nc + c, lora_col)),
                  vspec, vspec, vspec,
                  pl.BlockSpec((1, 512), lambda b, g, c: (0, 0)),
                  vspec, vspec, vspec, vspec, vspec, vspec, vspec,
                  pl.BlockSpec((64, W), lambda b, g, c: (0, g)),
                  pl.BlockSpec((64, W), lambda b, g, c: (0, g)),
                  pl.BlockSpec((256, W), lambda b, g, c: (0, g))],
        out_specs=pl.BlockSpec((R, W), lambda b, g, c: (b * nc + c, g)),
        out_shape=jax.ShapeDtypeStruct((T, RW_WIDTH), BF16),
        scratch_shapes=[pltpu.VMEM((G, RW_UNIT, RW_UNIT), F32),
                        pltpu.VMEM((1, W), F32), pltpu.VMEM((1, W), F32), pltpu.VMEM((1, W), F32),
                        pltpu.VMEM((1, 512), F32)],
        compiler_params=_params("parallel", "parallel", "arbitrary"),
        name="rwkv",
    )(z_all, z_all, z_all, z_all, mu_r, mu_k, mu_v, mu_l, w0, a0, kk, ka, rk, lnw, lnb, w2, a2, g2)


def _mla_proj_kernel(cq_ref, ckv_ref, kpe_ref, cos_ref, sin_ref, qn_ref, kvn_ref,
                     wqn_ref, wqp_ref, wqs_ref, wkv_ref, q_ref, kv_ref, kr_ref, *, scale):
    cqn = _bf(_rms(cq_ref[...], qn_ref[...]))
    ckvn = _bf(_rms(ckv_ref[...], kvn_ref[...]))
    cos = cos_ref[...]
    sin = sin_ref[...]
    q_nope = jnp.dot(cqn, wqn_ref[...], preferred_element_type=F32)
    q_pe = jnp.dot(cqn, wqp_ref[...], preferred_element_type=F32)
    q_ps = jnp.dot(cqn, wqs_ref[...], preferred_element_type=F32)
    for h in range(MLA_HEADS):
        hs = slice(h * LANES, (h + 1) * LANES)
        q_ref[:, 2 * h * LANES:(2 * h + 1) * LANES] = _bf(q_nope[:, hs] * scale)
        q_ref[:, (2 * h + 1) * LANES:(2 * h + 2) * LANES] = _bf((q_pe[:, hs] * cos + q_ps[:, hs] * sin) * scale)
    kv_ref[...] = _bf(jnp.dot(ckvn, wkv_ref[...], preferred_element_type=F32))
    blk = kpe_ref[...]
    kr_ref[...] = _bf(blk * cos + pltpu.roll(blk, MLA_ROPE, axis=1) * sin)


def _mla_proj(z_all, cos_t, sin_t, qn, kvn, wqn, wqp, wqs, wkv, tm=512):
    T = z_all.shape[0]
    HW = MLA_HEADS * 2 * LANES
    scale = float((MLA_NOPE + MLA_ROPE) ** -0.5 * 1.4426950408889634)
    full = lambda a: pl.BlockSpec(a.shape, lambda i: (0, 0))
    kpe_blk = (z_all.shape[1] - LANES) // LANES
    return pl.pallas_call(
        functools.partial(_mla_proj_kernel, scale=scale),
        grid=(T // tm,),
        in_specs=[pl.BlockSpec((tm, MLA_RANK), lambda i: (i, 6)),
                  pl.BlockSpec((tm, MLA_RANK), lambda i: (i, 7)),
                  pl.BlockSpec((tm, LANES), lambda i: (i, kpe_blk)),
                  pl.BlockSpec((tm, LANES), lambda i: (i, 0)),
                  pl.BlockSpec((tm, LANES), lambda i: (i, 0)),
                  full(qn), full(kvn), full(wqn), full(wqp), full(wqs), full(wkv)],
        out_specs=[pl.BlockSpec((tm, HW), lambda i: (i, 0)),
                   pl.BlockSpec((tm, HW), lambda i: (i, 0)),
                   pl.BlockSpec((tm, LANES), lambda i: (i, 0))],
        out_shape=[jax.ShapeDtypeStruct((T, HW), BF16),
                   jax.ShapeDtypeStruct((T, HW), BF16),
                   jax.ShapeDtypeStruct((T, LANES), BF16)],
        compiler_params=_params("parallel"),
        name="mla_proj",
    )(z_all, z_all, z_all, cos_t, sin_t, qn, kvn, wqn, wqp, wqs, wkv)


ATT_HEADS_PER_STEP = 8


def _attn_kernel(qi_ref, ki_ref, q_ref, kv_ref, kr_ref, o_ref, m_ref, l_ref, acc_ref, *, tq, tk, heads):
    t = pl.program_id(2)
    qi = qi_ref[t]
    ki = ki_ref[t]
    H = range(heads)

    @pl.when(ki == 0)
    def _():
        m_ref[...] = jnp.full_like(m_ref, -1e30)
        l_ref[...] = jnp.zeros_like(l_ref)
        acc_ref[...] = jnp.zeros_like(acc_ref)

    def update(masked):
        kr = kr_ref[...]
        s = [lax.dot_general(q_ref[:, 2 * h * LANES:(2 * h + 2) * LANES],
                             jnp.concatenate([kv_ref[:, 2 * h * LANES:(2 * h + 1) * LANES], kr], axis=1),
                             (((1,), (1,)), ((), ())), preferred_element_type=F32) for h in H]
        if masked:
            q_last = (qi * tq + lax.broadcasted_iota(jnp.int32, (tq, tk), 0)) | (CHUNK - 1)
            k_pos = ki * tk + lax.broadcasted_iota(jnp.int32, (tq, tk), 1)
            vis = k_pos <= q_last
            s = [jnp.where(vis, x, -1e30) for x in s]
        m_prev = [m_ref[h] for h in H]
        m_new = [jnp.maximum(m_prev[h], jnp.max(s[h], axis=-1, keepdims=True)) for h in H]
        p = [jnp.exp2(s[h] - jnp.concatenate([m_new[h]] * (tk // LANES), axis=1)) for h in H]
        alpha = [jnp.exp2(m_prev[h] - m_new[h]) for h in H]
        pv = [jnp.dot(_bf(p[h]), kv_ref[:, (2 * h + 1) * LANES:(2 * h + 2) * LANES], preferred_element_type=F32) for h in H]
        for h in H:
            psum = sum(p[h][:, c:c + LANES] for c in range(0, tk, LANES))
            l_ref[h] = alpha[h] * l_ref[h] + psum
            acc_ref[h] = alpha[h] * acc_ref[h] + pv[h]
            m_ref[h] = m_new[h]

    @pl.when(ki < qi)
    def _():
        update(False)

    @pl.when(ki == qi)
    def _():
        update(True)
        for h in H:
            o_ref[:, h * LANES:(h + 1) * LANES] = _bf(acc_ref[h] / jnp.sum(l_ref[h], axis=-1, keepdims=True))


def _attention(q, kv, kr, S, tq=512):
    T = q.shape[0]
    B = T // S
    nq = S // tq
    G = ATT_HEADS_PER_STEP
    pairs = [(a, b) for a in range(nq) for b in range(a + 1)]
    qi_tab = jnp.asarray([a for a, _ in pairs], jnp.int32)
    ki_tab = jnp.asarray([b for _, b in pairs], jnp.int32)
    grid_spec = pltpu.PrefetchScalarGridSpec(
        num_scalar_prefetch=2,
        grid=(B, MLA_HEADS // G, len(pairs)),
        in_specs=[pl.BlockSpec((tq, 2 * G * LANES), lambda b, g, t, qt, kt: (b * nq + qt[t], g)),
                  pl.BlockSpec((tq, 2 * G * LANES), lambda b, g, t, qt, kt: (b * nq + kt[t], g)),
                  pl.BlockSpec((tq, LANES), lambda b, g, t, qt, kt: (b * nq + kt[t], 0))],
        out_specs=pl.BlockSpec((tq, G * LANES), lambda b, g, t, qt, kt: (b * nq + qt[t], g)),
        scratch_shapes=[pltpu.VMEM((G, tq, LANES), F32), pltpu.VMEM((G, tq, LANES), F32), pltpu.VMEM((G, tq, LANES), F32)])
    return pl.pallas_call(
        functools.partial(_attn_kernel, tq=tq, tk=tq, heads=G),
        grid_spec=grid_spec,
        out_shape=jax.ShapeDtypeStruct((T, MLA_HEADS * MLA_V), BF16),
        compiler_params=_params("parallel", "parallel", "arbitrary"),
        name="attention",
    )(qi_tab, ki_tab, q, kv, kr)


def _mix_kernel(yrw_ref, ymla_ref, gate_ref, x_ref, wa_ref, wb_ref, wo_ref, g_ref, o_ref):
    D = x_ref.shape[1]
    t = (jax.nn.sigmoid(gate_ref[:, :D]) * jnp.dot(yrw_ref[...], wa_ref[...], preferred_element_type=F32)
         + jax.nn.sigmoid(gate_ref[:, D:]) * jnp.dot(ymla_ref[...], wb_ref[...], preferred_element_type=F32))
    mix = jnp.dot(_bf(t), wo_ref[...], preferred_element_type=F32)
    o_ref[...] = x_ref[...] + _rms(mix, g_ref[...])


def _resident(a):
    return pl.BlockSpec(a.shape, lambda *_: (0,) * a.ndim, pipeline_mode=pl.Buffered(1))


def _mix(y_rw, y_mla, z_all, x2, wa, wb, wo, g, tm=256):
    T, D = x2.shape
    return pl.pallas_call(
        _mix_kernel,
        grid=(T // tm,),
        in_specs=[pl.BlockSpec((tm, y_rw.shape[1]), lambda i: (i, 0)),
                  pl.BlockSpec((tm, y_mla.shape[1]), lambda i: (i, 0)),
                  pl.BlockSpec((tm, 2 * D), lambda i: (i, 1)),
                  pl.BlockSpec((tm, D), lambda i: (i, 0)),
                  _resident(wa), _resident(wb), _resident(wo), _resident(g)],
        out_specs=pl.BlockSpec((tm, D), lambda i: (i, 0)),
        out_shape=jax.ShapeDtypeStruct((T, D), F32),
        compiler_params=_params("parallel"),
        name="mix",
    )(y_rw, y_mla, z_all, x2, wa, wb, wo, g)


HALO = 16
FFN_SUB = 256
FFN_ROW_BLOCKS = 4


def _gelu_tanh(x):
    return 0.5 * x * (1.0 + jnp.tanh(0.7978845608028654 * (x + 0.044715 * x * x * x)))


def _ffn_kernel(h_ref, halo_ref, gin_ref, wug_ref, wuv_ref, cwg_ref, cwv_ref, cbg_ref, cbv_ref,
                wd_ref, gout_ref, o_ref, xn_ref, upg_ref, upv_ref, *, tm, tiles_per_seq):
    i = pl.program_id(0)
    j = pl.program_id(1)

    @pl.when(j == 0)
    def _():
        xn_ref[0:HALO, :] = _bf(_rms(halo_ref[...], gin_ref[...]))
        xn_ref[HALO:, :] = _bf(_rms(h_ref[...], gin_ref[...]))
        o_ref[...] = jnp.zeros_like(o_ref)

    keep = jnp.where((i % tiles_per_seq) == 0, 0.0, 1.0).astype(F32)
    tf = upg_ref.shape[1]
    subs = [slice(c, c + FFN_SUB) for c in range(0, tf, FFN_SUB)]

    for cs in subs:
        for w_ref, up_ref in ((wug_ref, upg_ref), (wuv_ref, upv_ref)):
            up = jnp.dot(xn_ref[...], w_ref[:, cs], preferred_element_type=F32)
            up_ref[0:HALO, cs] = up[0:HALO] * keep
            up_ref[HALO:, cs] = up[HALO:]

    def conv(up_ref, cw_ref, cb_ref, cs, r0, nr):
        cw = cw_ref[:, cs]
        return (cw[0:1, :] * up_ref[HALO - 2 + r0:HALO - 2 + r0 + nr, cs]
                + cw[1:2, :] * up_ref[HALO - 1 + r0:HALO - 1 + r0 + nr, cs]
                + cw[2:3, :] * up_ref[HALO + r0:HALO + r0 + nr, cs] + cb_ref[:, cs])

    def act(cs, r0, nr):
        return _bf(_gelu_tanh(conv(upg_ref, cwg_ref, cbg_ref, cs, r0, nr)) * conv(upv_ref, cwv_ref, cbv_ref, cs, r0, nr))

    head = [act(cs, 0, tm) for cs in subs[:-1]]
    nr = tm // FFN_ROW_BLOCKS
    for r0 in range(0, tm, nr):
        a = jnp.concatenate([x[r0:r0 + nr] for x in head] + [act(subs[-1], r0, nr)], axis=1)
        o_ref[r0:r0 + nr, :] += jnp.dot(a, wd_ref[...], preferred_element_type=F32)

    @pl.when(j == pl.num_programs(1) - 1)
    def _():
        o_ref[...] = h_ref[...] + _rms(o_ref[...], gout_ref[...])


def _ffn(h1, S, gin, w_up, conv_w, conv_b, w_down, gout, tm=1024, tf=512):
    T, D = h1.shape
    F = w_down.shape[0]
    nf = F // tf
    hb = tm // HALO
    return pl.pallas_call(
        functools.partial(_ffn_kernel, tm=tm, tiles_per_seq=S // tm),
        grid=(T // tm, nf),
        in_specs=[pl.BlockSpec((tm, D), lambda i, j: (i, 0), pipeline_mode=pl.Buffered(1)),
                  pl.BlockSpec((HALO, D), lambda i, j: (jnp.maximum(i * hb - 1, 0), 0)),
                  pl.BlockSpec((1, D), lambda i, j: (0, 0)),
                  pl.BlockSpec((D, tf), lambda i, j: (0, j)),
                  pl.BlockSpec((D, tf), lambda i, j: (0, nf + j)),
                  pl.BlockSpec((3, tf), lambda i, j: (0, j)),
                  pl.BlockSpec((3, tf), lambda i, j: (0, nf + j)),
                  pl.BlockSpec((1, tf), lambda i, j: (0, j)),
                  pl.BlockSpec((1, tf), lambda i, j: (0, nf + j)),
                  pl.BlockSpec((tf, D), lambda i, j: (j, 0)),
                  pl.BlockSpec((1, D), lambda i, j: (0, 0))],
        out_specs=pl.BlockSpec((tm, D), lambda i, j: (i, 0)),
        out_shape=jax.ShapeDtypeStruct((T, D), F32),
        scratch_shapes=[pltpu.VMEM((tm + HALO, D), BF16),
                        pltpu.VMEM((tm + HALO, tf), F32), pltpu.VMEM((tm + HALO, tf), F32)],
        compiler_params=_params("parallel", "arbitrary"),
        name="ffn",
    )(h1, h1, gin, w_up, w_up, conv_w, conv_w, conv_b, conv_b, w_down, gout)


def _ple_kernel(h_ref, p_ref, wpg_ref, wple_ref, g_ref, o_ref):
    h = h_ref[...]
    gate = jax.nn.sigmoid(jnp.dot(_bf(h), wpg_ref[...], preferred_element_type=F32))
    e = jnp.dot(_bf(p_ref[...]), wple_ref[...], preferred_element_type=F32)
    o_ref[...] = h + _rms(gate * e, g_ref[...])


def _ple(h2, p2, wpg, wple, g, tm=512):
    T, D = h2.shape
    full = lambda a: pl.BlockSpec(a.shape, lambda i: (0, 0))
    return pl.pallas_call(
        _ple_kernel,
        grid=(T // tm,),
        in_specs=[pl.BlockSpec((tm, D), lambda i: (i, 0)),
                  pl.BlockSpec((tm, p2.shape[1]), lambda i: (i, 0)),
                  full(wpg), full(wple), full(g)],
        out_specs=pl.BlockSpec((tm, D), lambda i: (i, 0)),
        out_shape=jax.ShapeDtypeStruct((T, D), F32),
        compiler_params=_params("parallel"),
        name="ple",
    )(h2, p2, wpg, wple, g)


def _swap_halves(w):
    half = w.shape[-1] // 2
    return jnp.concatenate([w[..., half:], w[..., :half]], axis=-1)


def _pack_w_in_kernel(off_ref, w_ref, kpe_ref, o_ref):
    j = pl.program_id(0)
    last = pl.num_programs(0) - 1

    @pl.when(j < last)
    def _():
        o_ref[...] = _bf(w_ref[...])

    @pl.when(j == last)
    def _():
        kpe = kpe_ref[...]
        half = MLA_ROPE // 2
        pad = jnp.zeros((W_IN_TILE - RW_LORA - 2 * MLA_ROPE, kpe.shape[1]), F32)
        o_ref[...] = _bf(jnp.concatenate([w_ref[0:RW_LORA, :], pad, kpe, kpe[half:], kpe[:half]], axis=0))


def _pack_w_in(w_in_t):
    N, D = w_in_t.shape
    rkv_end = 3 * RW_WIDTH
    m0 = rkv_end + RW_LORA
    g0 = m0 + 2 * MLA_RANK + MLA_ROPE
    starts = (list(range(0, rkv_end, W_IN_TILE)) + [m0, m0 + MLA_RANK]
              + list(range(g0, g0 + 4096, W_IN_TILE)) + [rkv_end])
    assert len(starts) * W_IN_TILE == W_IN_PACKED and N == g0 + 4096
    assert all(s % W_IN_ALIGN == 0 for s in starts)
    starts = [s // W_IN_ALIGN for s in starts]
    grid_spec = pltpu.PrefetchScalarGridSpec(
        num_scalar_prefetch=1,
        grid=(len(starts),),
        in_specs=[pl.BlockSpec((pl.Element(W_IN_TILE), pl.Element(D)),
                               lambda j, off: (pl.multiple_of(off[j] * W_IN_ALIGN, W_IN_ALIGN), 0)),
                  pl.BlockSpec((pl.Element(MLA_ROPE), pl.Element(D)), lambda j, off: (m0 + 2 * MLA_RANK, 0))],
        out_specs=pl.BlockSpec((W_IN_TILE, D), lambda j, off: (j, 0)))
    return pl.pallas_call(
        _pack_w_in_kernel,
        grid_spec=grid_spec,
        out_shape=jax.ShapeDtypeStruct((W_IN_PACKED, D), BF16),
        compiler_params=_params("arbitrary"),
        name="pack_w_in",
    )(jnp.asarray(starts, jnp.int32), w_in_t, w_in_t)


def _row(v):
    return v.reshape(1, -1).astype(F32)


def kernel(x, p, positions, pre_mix_norm, w_in, rw_mu, rw_w0, rw_w2, rw_a0, rw_a2, rw_g2, rw_k_k, rw_k_a, rw_r_k, rw_lnx_w, rw_lnx_b, mla_q_norm, mla_w_q_up, mla_kv_norm, mla_w_kv_up, w_branch_rw, w_branch_mla, w_out, post_mix_norm, pre_ffn_norm, w_up, conv_w, conv_b, w_down, post_ffn_norm, w_ple, w_ple_gate, ple_norm):
    B, S, D = x.shape
    T = B * S
    depth = w_in.shape[0]

    inv_freq = ROPE_BASE ** (-jnp.arange(0, MLA_ROPE, 2, dtype=F32) / MLA_ROPE)
    ang = positions.astype(F32)[..., None] * inv_freq
    cos = jnp.cos(ang).reshape(T, MLA_ROPE // 2)
    sin = jnp.sin(ang).reshape(T, MLA_ROPE // 2)
    zpad = jnp.zeros((T, LANES - MLA_ROPE), F32)
    cos_t = jnp.concatenate([cos, cos, zpad], axis=1)
    sin_t = jnp.concatenate([-sin, sin, zpad], axis=1)

    h = x.reshape(T, D)
    for i in range(depth):
        w_in_p = _pack_w_in(jnp.swapaxes(w_in, 1, 2)[i])
        z_all = _in_proj(h, _row(pre_mix_norm[i]), w_in_p)

        mu = rw_mu[i]
        mu_l = jnp.concatenate([mu[3 * RW_WIDTH:], jnp.zeros((512 - 288,), F32)]).reshape(1, 512)
        g2p = jnp.concatenate([rw_g2[i], jnp.zeros((256 - rw_g2.shape[1], RW_WIDTH), F32)], axis=0)
        w2p = jnp.concatenate([rw_w2[i]], axis=0)
        y_rw = _rwkv(z_all, S, _row(mu[:RW_WIDTH]), _row(mu[RW_WIDTH:2 * RW_WIDTH]), _row(mu[2 * RW_WIDTH:3 * RW_WIDTH]),
                     mu_l, _row(rw_w0[i]), _row(rw_a0[i]), _row(rw_k_k[i]), _row(rw_k_a[i]), _row(rw_r_k[i]),
                     _row(rw_lnx_w[i]), _row(rw_lnx_b[i]), _bf(w2p), _bf(rw_a2[i]), _bf(g2p))

        wq = mla_w_q_up[i].reshape(MLA_RANK, MLA_HEADS, MLA_NOPE + MLA_ROPE)
        wqn = wq[:, :, :MLA_NOPE].reshape(MLA_RANK, MLA_HEADS * MLA_NOPE)
        wq_pe = wq[:, :, MLA_NOPE:]
        padh = jnp.zeros((MLA_RANK, MLA_HEADS, LANES - MLA_ROPE), F32)
        wqp = jnp.concatenate([wq_pe, padh], axis=-1).reshape(MLA_RANK, MLA_HEADS * LANES)
        wqs = jnp.concatenate([_swap_halves(wq_pe), padh], axis=-1).reshape(MLA_RANK, MLA_HEADS * LANES)
        q, kv, kr = _mla_proj(z_all, cos_t, sin_t, _row(mla_q_norm[i]), _row(mla_kv_norm[i]),
                              _bf(wqn), _bf(wqp), _bf(wqs), _bf(mla_w_kv_up[i]))
        y_mla = _attention(q, kv, kr, S)

        h = _mix(y_rw, y_mla, z_all, h, _bf(w_branch_rw[i]), _bf(w_branch_mla[i]), _bf(w_out[i]), _row(post_mix_norm[i]))
        h = _ffn(h, S, _row(pre_ffn_norm[i]), _bf(w_up[i]), conv_w[i], _row(conv_b[i]), _bf(w_down[i]), _row(post_ffn_norm[i]))
        h = _ple(h, p[i].reshape(T, -1), _bf(w_ple_gate[i]), _bf(w_ple[i]), _row(ple_norm[i]))
    return h.reshape(B, S, D)
```

```python
import functools

import jax
import jax.numpy as jnp
from jax import lax
from jax.experimental import pallas as pl
from jax.experimental.pallas import tpu as pltpu

NORM_EPS = 1e-6
GN_EPS = 64e-5
CHUNK = 64
ROPE_BASE = 10000.0

RW_HEADS = 16
RW_N = 64
RW_WIDTH = RW_HEADS * RW_N
RW_L = 64
LANES = 128
RW_UNIT_HEADS = 2
RW_UNIT = RW_UNIT_HEADS * RW_N
RW_UNITS_PER_STEP = 8
RW_CHUNKS_PER_STEP = 2

MLA_HEADS = 8
MLA_NOPE = 128
MLA_ROPE = 64
MLA_V = 128
MLA_RANK = 512

RW_LORA = 288
W_IN_TILE = 512
W_IN_ALIGN = 32
W_IN_PACKED = 3 * RW_WIDTH + 2 * MLA_RANK + 4096 + W_IN_TILE

VMEM_LIMIT_BYTES = 56 * 1024 * 1024

BF16 = jnp.bfloat16
F32 = jnp.float32


def _params(*sem):
    return pltpu.CompilerParams(dimension_semantics=sem, vmem_limit_bytes=VMEM_LIMIT_BYTES)


def _bf(x):
    return x.astype(BF16)


def _mm(a, b):
    return jnp.dot(_bf(a), _bf(b), preferred_element_type=F32)


def _mm_nt(a, b):
    return lax.dot_general(_bf(a), _bf(b), (((1,), (1,)), ((), ())), preferred_element_type=F32)


def _rms(x, g):
    return x * lax.rsqrt(jnp.mean(x * x, axis=-1, keepdims=True) + NORM_EPS) * g


def _inproj_kernel(*refs, n_cast):
    x_ref, g_ref, w_ref = refs[:3]
    cast_in = refs[3:3 + n_cast]
    o_ref = refs[3 + n_cast]
    cast_out = refs[4 + n_cast:4 + 2 * n_cast]
    xn_ref = refs[-1]

    @pl.when(pl.program_id(1) == 0)
    def _():
        xn_ref[...] = _bf(_rms(x_ref[...], g_ref[...]))

    o_ref[...] = lax.dot_general(xn_ref[...], w_ref[...], (((1,), (1,)), ((), ())), preferred_element_type=F32)
    for src, dst in zip(cast_in, cast_out):
        dst[...] = _bf(src[...])


def _cast_rows(n_rows, n_steps):
    rb = 16
    while n_rows % rb or n_rows // rb > n_steps:
        rb += 16
    return rb


def _in_proj(x2, g, w_t, cast, tm=2048, tn=W_IN_TILE):
    T, D = x2.shape
    N = w_t.shape[0]
    ni, nj = T // tm, N // tn
    cast_specs = []
    for a in cast:
        rb = _cast_rows(a.shape[0], ni * nj)
        last = a.shape[0] // rb - 1
        cast_specs.append(pl.BlockSpec((rb, a.shape[1]), lambda i, j, last=last: (jnp.minimum(i * nj + j, last), 0)))
    outs = pl.pallas_call(
        functools.partial(_inproj_kernel, n_cast=len(cast)),
        grid=(ni, nj),
        in_specs=[
            pl.BlockSpec((tm, D), lambda i, j: (i, 0), pipeline_mode=pl.Buffered(1)),
            pl.BlockSpec((1, D), lambda i, j: (0, 0)),
            pl.BlockSpec((tn, D), lambda i, j: (j, 0)),
        ] + cast_specs,
        out_specs=[pl.BlockSpec((tm, tn), lambda i, j: (i, j))] + cast_specs,
        out_shape=[jax.ShapeDtypeStruct((T, N), F32)] + [jax.ShapeDtypeStruct(a.shape, BF16) for a in cast],
        scratch_shapes=[pltpu.VMEM((tm, D), BF16)],
        compiler_params=_params("arbitrary", "arbitrary"),
        name="in_proj",
    )(x2, g, w_t, *cast)
    return outs[0], outs[1:]


def _split3(x):
    hi = _bf(x)
    r1 = x - hi.astype(F32)
    mid = _bf(r1)
    lo = _bf(r1 - mid.astype(F32))
    return hi, mid, lo


def _rwkv_kernel(zr_ref, zk_ref, zv_ref, zl_ref, mur_ref, muk_ref, muv_ref, mul_ref,
                 w0_ref, a0_ref, kk_ref, ka_ref, rk_ref, lnw_ref, lnb_ref,
                 w2_ref, a2_ref, g2_ref, o_ref,
                 st_ref, pr_ref, pk_ref, pv_ref, plr_ref, *, units, chunks):
    L = RW_L
    UW = RW_UNIT
    c = pl.program_id(2)

    @pl.when(c == 0)
    def _():
        st_ref[...] = jnp.zeros_like(st_ref)
        pr_ref[...] = jnp.zeros_like(pr_ref)
        pk_ref[...] = jnp.zeros_like(pk_ref)
        pv_ref[...] = jnp.zeros_like(pv_ref)
        plr_ref[...] = jnp.zeros_like(plr_ref)

    def shift_lerp(z_ref, prev_ref, mu_ref):
        z = z_ref[...]
        rolled = pltpu.roll(z, 1, axis=0)
        row = lax.broadcasted_iota(jnp.int32, z.shape, 0)
        zs = jnp.where(row == 0, prev_ref[...], rolled)
        prev_ref[...] = z[z.shape[0] - 1:, :]
        return z + (zs - z) * mu_ref[...]

    r_all = shift_lerp(zr_ref, pr_ref, mur_ref)
    k_all = shift_lerp(zk_ref, pk_ref, muk_ref)
    v_all = shift_lerp(zv_ref, pv_ref, muv_ref)
    lo_all = shift_lerp(zl_ref, plr_ref, mul_ref)
    wd = jnp.tanh(lo_all[:, 0:64])
    ad = lo_all[:, 64:128]
    gd = jax.nn.sigmoid(lo_all[:, 128:384])

    hshift = RW_N.bit_length() - 1
    head_of_lane = lax.broadcasted_iota(jnp.int32, (1, UW), 1) >> hshift
    head_masks = [head_of_lane == j for j in range(RW_UNIT_HEADS)]
    row_l = lax.broadcasted_iota(jnp.int32, (L, UW), 0)
    col_l = lax.broadcasted_iota(jnp.int32, (L, UW), 1) & (RW_N - 1)
    strict = col_l < row_l
    incl = col_l <= row_l
    eye_pair = (col_l == row_l).astype(F32)
    row_s = lax.broadcasted_iota(jnp.int32, (UW, UW), 0)
    col_s = lax.broadcasted_iota(jnp.int32, (UW, UW), 1)
    same_head = (row_s >> hshift) == (col_s >> hshift)
    diag_s = row_s == col_s
    ones_bd = same_head.astype(BF16)

    def stack(x):
        return jnp.concatenate([jnp.where(m, x, 0.0) for m in head_masks], axis=0)

    def head_sum(x):
        return jnp.dot(_bf(x), ones_bd, preferred_element_type=F32)

    def cat(xs, axis):
        return jnp.concatenate(xs, axis=axis)

    U = range(units)
    uls = [slice(u * UW, (u + 1) * UW) for u in U]
    xw = [w0_ref[:, sl] + _mm(wd, w2_ref[:, sl]) for sl in uls]
    ga = [a0_ref[:, sl] + _mm(ad, a2_ref[:, sl]) for sl in uls]
    g_out = [_mm(gd, g2_ref[:, sl]) for sl in uls]
    kkv = [k_all[:, uls[u]] * kk_ref[:, uls[u]] for u in U]
    ss = [head_sum(x * x) for x in kkv]
    logw_u = [-0.6065306597126334 * jax.nn.sigmoid(x) for x in xw]
    gate_u = [jax.nn.sigmoid(x) for x in ga]
    kkn_u = [kkv[u] / jnp.maximum(jnp.sqrt(ss[u]), 1e-12) for u in U]
    k2_u = [k_all[:, uls[u]] * (1.0 + (gate_u[u] - 1.0) * ka_ref[:, uls[u]]) for u in U]
    bonus_in = [head_sum(r_all[:, uls[u]] * k2_u[u] * rk_ref[:, uls[u]]) for u in U]

    items = [(ci, u) for ci in range(chunks) for u in U]
    P = range(len(items))
    rows = [slice(ci * L, (ci + 1) * L) for ci, _ in items]
    r = [r_all[rows[p], uls[items[p][1]]] for p in P]
    v = [v_all[rows[p], uls[items[p][1]]] for p in P]
    logw = [logw_u[items[p][1]][rows[p]] for p in P]
    gate = [gate_u[items[p][1]][rows[p]] for p in P]
    kkn = [kkn_u[items[p][1]][rows[p]] for p in P]
    k2 = [k2_u[items[p][1]][rows[p]] for p in P]
    parts = [_split3(x) for x in logw]
    tri3 = ((lax.broadcasted_iota(jnp.int32, (L, 3 * L), 1) & (L - 1))
            <= lax.broadcasted_iota(jnp.int32, (L, 3 * L), 0)).astype(BF16)
    cum = [jnp.dot(tri3, cat(list(parts[p]), 0), preferred_element_type=F32) for p in P]
    cum_l = [c_[L - 1:L, :] for c_ in cum]
    e_neg = [jnp.exp(-c_) for c_ in cum]
    e_end = [jnp.exp(cum_l[p] - cum[p]) for p in P]
    kka = [kkn[p] * gate[p] for p in P]
    rt = [r[p] * jnp.exp(cum[p]) for p in P]
    kt = [k2[p] * e_neg[p] for p in P]
    bt = [kka[p] * e_neg[p] for p in P]
    at = [-kkn[p] * jnp.exp(cum[p] - logw[p]) for p in P]
    bk_t = [cat([kka[p] * e_end[p], k2[p] * e_end[p]], 0).T for p in P]
    w_end = [jnp.exp(c_) for c_ in cum_l]

    a_all = [_mm_nt(cat([at[p], rt[p]], 0), cat([stack(bt[p]), stack(kt[p])], 0)) for p in P]
    a_ab = [jnp.where(strict, a[:L, :UW], 0.0) for a in a_all]
    a_ak = [jnp.where(strict, a[:L, UW:], 0.0) for a in a_all]
    a_rb = [jnp.where(incl, a[L:, :UW], 0.0) for a in a_all]
    a_rk = [jnp.where(incl, a[L:, UW:], 0.0) for a in a_all]
    av = [_mm(a_ak[p], stack(v[p])) for p in P]

    tinv = [eye_pair + a for a in a_ab]
    pw = a_ab
    for _ in range(5):
        pw = [_mm(x, stack(x)) for x in pw]
        tinv = [tinv[p] + _mm(tinv[p], stack(pw[p])) for p in P]

    pq = [_mm(tinv[p], cat([stack(at[p]), stack(av[p])], 1)) for p in P]
    pm = [x[:, :UW] for x in pq]
    qm = [x[:, UW:] for x in pq]
    rm = [rt[p] + _mm(a_rb[p], stack(pm[p])) for p in P]
    y0 = [_mm(cat([a_rb[p], a_rk[p]], 1), cat([stack(qm[p]), stack(v[p])], 0)) for p in P]
    mn = [_mm(bk_t[p], cat([cat([pm[p], qm[p]], 1), cat([jnp.zeros_like(v[p]), v[p]], 1)], 0)) for p in P]
    m_mat = [jnp.where(diag_s, w_end[p], 0.0) + jnp.where(same_head, mn[p][:, :UW], 0.0) for p in P]
    n_mat = [jnp.where(same_head, mn[p][:, UW:], 0.0) for p in P]

    state = [st_ref[u] for u in U]
    y_chunks = []
    for ci in range(chunks):
        ps = [ci * units + u for u in U]
        ys = [_mm(cat([rm[p], m_mat[p]], 0), state[u]) for u, p in zip(U, ps)]
        state = [ys[u][L:] + n_mat[p] for u, p in zip(U, ps)]
        y_chunks.append([ys[u][:L] + y0[p] for u, p in zip(U, ps)])
    for u in U:
        st_ref[u] = state[u]

    y = [cat([y_chunks[ci][u] for ci in range(chunks)], 0) for u in U]
    mean = [head_sum(x) * (1.0 / RW_N) for x in y]
    yc = [y[u] - mean[u] for u in U]
    var = [head_sum(x * x) * (1.0 / RW_N) for x in yc]
    for u in U:
        yn = yc[u] * lax.rsqrt(var[u] + GN_EPS) * lnw_ref[:, uls[u]] + lnb_ref[:, uls[u]]
        o_ref[:, uls[u]] = _bf((yn + bonus_in[u] * v_all[:, uls[u]]) * g_out[u])


def _rwkv(z_all, S, mu_r, mu_k, mu_v, mu_l, w0, a0, kk, ka, rk, lnw, lnb, w2, a2, g2):
    T = z_all.shape[0]
    B = T // S
    R = RW_CHUNKS_PER_STEP * RW_L
    nc = S // R
    G = RW_UNITS_PER_STEP
    W = G * RW_UNIT
    ngrp = RW_WIDTH // W
    zspec = lambda off: pl.BlockSpec((R, W), lambda b, g, c, off=off: (b * nc + c, off * ngrp + g))
    vspec = pl.BlockSpec((1, W), lambda b, g, c: (0, g))
    lora_col = (z_all.shape[1] - 512) // 512
    return pl.pallas_call(
        functools.partial(_rwkv_kernel, units=G, chunks=RW_CHUNKS_PER_STEP),
        grid=(B, ngrp, nc),
        in_specs=[zspec(0), zspec(1), zspec(2),
                  pl.BlockSpec((R, 512), lambda b, g, c: (b * nc + c, lora_col)),
                  vspec, vspec, vspec,
                  pl.BlockSpec((1, 512), lambda b, g, c: (0, 0)),
                  vspec, vspec, vspec, vspec, vspec, vspec, vspec,
                  pl.BlockSpec((64, W), lambda b, g, c: (0, g)),
                  pl.BlockSpec((64, W), lambda b, g, c: (0, g)),
                  pl.BlockSpec((256, W), lambda b, g, c: (0, g))],
        out_specs=pl.BlockSpec((R, W), lambda b, g, c: (b * nc + c, g)),
        out_shape=jax.ShapeDtypeStruct((T, RW_WIDTH), BF16),
        scratch_shapes=[pltpu.VMEM((G, RW_UNIT, RW_UNIT), F32),
                        pltpu.VMEM((1, W), F32), pltpu.VMEM((1, W), F32), pltpu.VMEM((1, W), F32),
                        pltpu.VMEM((1, 512), F32)],
        compiler_params=_params("parallel", "parallel", "arbitrary"),
        name="rwkv",
    )(z_all, z_all, z_all, z_all, mu_r, mu_k, mu_v, mu_l, w0, a0, kk, ka, rk, lnw, lnb, w2, a2, g2)


def _mla_proj_kernel(cq_ref, ckv_ref, kpe_ref, cos_ref, sin_ref, qn_ref, kvn_ref,
                     wqn_ref, wqp_ref, wqs_ref, wkv_ref, q_ref, kv_ref, kr_ref, *, scale):
    cqn = _bf(_rms(cq_ref[...], qn_ref[...]))
    ckvn = _bf(_rms(ckv_ref[...], kvn_ref[...]))
    cos = cos_ref[...]
    sin = sin_ref[...]
    q_nope = jnp.dot(cqn, wqn_ref[...], preferred_element_type=F32)
    q_pe = jnp.dot(cqn, wqp_ref[...], preferred_element_type=F32)
    q_ps = jnp.dot(cqn, wqs_ref[...], preferred_element_type=F32)
    for h in range(MLA_HEADS):
        hs = slice(h * LANES, (h + 1) * LANES)
        q_ref[:, 2 * h * LANES:(2 * h + 1) * LANES] = _bf(q_nope[:, hs] * scale)
        q_ref[:, (2 * h + 1) * LANES:(2 * h + 2) * LANES] = _bf((q_pe[:, hs] * cos + q_ps[:, hs] * sin) * scale)
    kv_ref[...] = _bf(jnp.dot(ckvn, wkv_ref[...], preferred_element_type=F32))
    blk = kpe_ref[...]
    kr_ref[...] = _bf(blk * cos + pltpu.roll(blk, MLA_ROPE, axis=1) * sin)


def _mla_proj(z_all, cos_t, sin_t, qn, kvn, wqn, wqp, wqs, wkv, tm=512):
    T = z_all.shape[0]
    HW = MLA_HEADS * 2 * LANES
    scale = float((MLA_NOPE + MLA_ROPE) ** -0.5 * 1.4426950408889634)
    full = lambda a: pl.BlockSpec(a.shape, lambda i: (0, 0))
    kpe_blk = (z_all.shape[1] - LANES) // LANES
    return pl.pallas_call(
        functools.partial(_mla_proj_kernel, scale=scale),
        grid=(T // tm,),
        in_specs=[pl.BlockSpec((tm, MLA_RANK), lambda i: (i, 6)),
                  pl.BlockSpec((tm, MLA_RANK), lambda i: (i, 7)),
                  pl.BlockSpec((tm, LANES), lambda i: (i, kpe_blk)),
                  pl.BlockSpec((tm, LANES), lambda i: (i, 0)),
                  pl.BlockSpec((tm, LANES), lambda i: (i, 0)),
                  full(qn), full(kvn), full(wqn), full(wqp), full(wqs), full(wkv)],
        out_specs=[pl.BlockSpec((tm, HW), lambda i: (i, 0)),
                   pl.BlockSpec((tm, HW), lambda i: (i, 0)),
                   pl.BlockSpec((tm, LANES), lambda i: (i, 0))],
        out_shape=[jax.ShapeDtypeStruct((T, HW), BF16),
                   jax.ShapeDtypeStruct((T, HW), BF16),
                   jax.ShapeDtypeStruct((T, LANES), BF16)],
        compiler_params=_params("parallel"),
        name="mla_proj",
    )(z_all, z_all, z_all, cos_t, sin_t, qn, kvn, wqn, wqp, wqs, wkv)


ATT_HEADS_PER_STEP = 8


def _attn_kernel(qi_ref, ki_ref, q_ref, kv_ref, kr_ref, o_ref, m_ref, l_ref, acc_ref, *, tq, tk, heads):
    t = pl.program_id(2)
    qi = qi_ref[t]
    ki = ki_ref[t]
    H = range(heads)

    @pl.when(ki == 0)
    def _():
        m_ref[...] = jnp.full_like(m_ref, -1e30)
        l_ref[...] = jnp.zeros_like(l_ref)
        acc_ref[...] = jnp.zeros_like(acc_ref)

    def update(masked):
        kr = kr_ref[...]
        s = [lax.dot_general(q_ref[:, 2 * h * LANES:(2 * h + 2) * LANES],
                             jnp.concatenate([kv_ref[:, 2 * h * LANES:(2 * h + 1) * LANES], kr], axis=1),
                             (((1,), (1,)), ((), ())), preferred_element_type=F32) for h in H]
        if masked:
            q_last = (qi * tq + lax.broadcasted_iota(jnp.int32, (tq, tk), 0)) | (CHUNK - 1)
            k_pos = ki * tk + lax.broadcasted_iota(jnp.int32, (tq, tk), 1)
            vis = k_pos <= q_last
            s = [jnp.where(vis, x, -1e30) for x in s]
        m_prev = [m_ref[h] for h in H]
        m_new = [jnp.maximum(m_prev[h], jnp.max(s[h], axis=-1, keepdims=True)) for h in H]
        p = [jnp.exp2(s[h] - jnp.concatenate([m_new[h]] * (tk // LANES), axis=1)) for h in H]
        alpha = [jnp.exp2(m_prev[h] - m_new[h]) for h in H]
        pv = [jnp.dot(_bf(p[h]), kv_ref[:, (2 * h + 1) * LANES:(2 * h + 2) * LANES], preferred_element_type=F32) for h in H]
        for h in H:
            psum = sum(p[h][:, c:c + LANES] for c in range(0, tk, LANES))
            l_ref[h] = alpha[h] * l_ref[h] + psum
            acc_ref[h] = alpha[h] * acc_ref[h] + pv[h]
            m_ref[h] = m_new[h]

    @pl.when(ki < qi)
    def _():
        update(False)

    @pl.when(ki == qi)
    def _():
        update(True)
        for h in H:
            o_ref[:, h * LANES:(h + 1) * LANES] = _bf(acc_ref[h] / jnp.sum(l_ref[h], axis=-1, keepdims=True))


def _attention(q, kv, kr, S, tq=512):
    T = q.shape[0]
    B = T // S
    nq = S // tq
    G = ATT_HEADS_PER_STEP
    pairs = [(a, b) for a in range(nq) for b in range(a + 1)]
    qi_tab = jnp.asarray([a for a, _ in pairs], jnp.int32)
    ki_tab = jnp.asarray([b for _, b in pairs], jnp.int32)
    grid_spec = pltpu.PrefetchScalarGridSpec(
        num_scalar_prefetch=2,
        grid=(B, MLA_HEADS // G, len(pairs)),
        in_specs=[pl.BlockSpec((tq, 2 * G * LANES), lambda b, g, t, qt, kt: (b * nq + qt[t], g)),
                  pl.BlockSpec((tq, 2 * G * LANES), lambda b, g, t, qt, kt: (b * nq + kt[t], g)),
                  pl.BlockSpec((tq, LANES), lambda b, g, t, qt, kt: (b * nq + kt[t], 0))],
        out_specs=pl.BlockSpec((tq, G * LANES), lambda b, g, t, qt, kt: (b * nq + qt[t], g)),
        scratch_shapes=[pltpu.VMEM((G, tq, LANES), F32), pltpu.VMEM((G, tq, LANES), F32), pltpu.VMEM((G, tq, LANES), F32)])
    return pl.pallas_call(
        functools.partial(_attn_kernel, tq=tq, tk=tq, heads=G),
        grid_spec=grid_spec,
        out_shape=jax.ShapeDtypeStruct((T, MLA_HEADS * MLA_V), BF16),
        compiler_params=_params("parallel", "parallel", "arbitrary"),
        name="attention",
    )(qi_tab, ki_tab, q, kv, kr)


def _mix_kernel(yrw_ref, ymla_ref, gate_ref, x_ref, wa_ref, wb_ref, wo_ref, g_ref, o_ref):
    D = x_ref.shape[1]
    t = (jax.nn.sigmoid(gate_ref[:, :D]) * jnp.dot(yrw_ref[...], wa_ref[...], preferred_element_type=F32)
         + jax.nn.sigmoid(gate_ref[:, D:]) * jnp.dot(ymla_ref[...], wb_ref[...], preferred_element_type=F32))
    mix = jnp.dot(_bf(t), wo_ref[...], preferred_element_type=F32)
    o_ref[...] = x_ref[...] + _rms(mix, g_ref[...])


def _resident(a):
    return pl.BlockSpec(a.shape, lambda *_: (0,) * a.ndim, pipeline_mode=pl.Buffered(1))


def _mix(y_rw, y_mla, z_all, x2, wa, wb, wo, g, tm=256):
    T, D = x2.shape
    return pl.pallas_call(
        _mix_kernel,
        grid=(T // tm,),
        in_specs=[pl.BlockSpec((tm, y_rw.shape[1]), lambda i: (i, 0)),
                  pl.BlockSpec((tm, y_mla.shape[1]), lambda i: (i, 0)),
                  pl.BlockSpec((tm, 2 * D), lambda i: (i, 1)),
                  pl.BlockSpec((tm, D), lambda i: (i, 0)),
                  _resident(wa), _resident(wb), _resident(wo), _resident(g)],
        out_specs=pl.BlockSpec((tm, D), lambda i: (i, 0)),
        out_shape=jax.ShapeDtypeStruct((T, D), F32),
        compiler_params=_params("parallel"),
        name="mix",
    )(y_rw, y_mla, z_all, x2, wa, wb, wo, g)


HALO = 16
FFN_SUB = 256
FFN_ROW_BLOCKS = 4


def _gelu_tanh(x):
    return 0.5 * x * (1.0 + jnp.tanh(0.7978845608028654 * (x + 0.044715 * x * x * x)))


def _ffn_kernel(h_ref, halo_ref, gin_ref, wug_ref, wuv_ref, cwg_ref, cwv_ref, cbg_ref, cbv_ref,
                wd_ref, gout_ref, o_ref, xn_ref, upg_ref, upv_ref, *, tm, tiles_per_seq):
    i = pl.program_id(0)
    j = pl.program_id(1)

    @pl.when(j == 0)
    def _():
        xn_ref[0:HALO, :] = _bf(_rms(halo_ref[...], gin_ref[...]))
        xn_ref[HALO:, :] = _bf(_rms(h_ref[...], gin_ref[...]))
        o_ref[...] = jnp.zeros_like(o_ref)

    keep = jnp.where((i % tiles_per_seq) == 0, 0.0, 1.0).astype(F32)
    tf = upg_ref.shape[1]
    subs = [slice(c, c + FFN_SUB) for c in range(0, tf, FFN_SUB)]

    for cs in subs:
        for w_ref, up_ref in ((wug_ref, upg_ref), (wuv_ref, upv_ref)):
            up = jnp.dot(xn_ref[...], w_ref[:, cs], preferred_element_type=F32)
            up_ref[0:HALO, cs] = up[0:HALO] * keep
            up_ref[HALO:, cs] = up[HALO:]

    def conv(up_ref, cw_ref, cb_ref, cs, r0, nr):
        cw = cw_ref[:, cs]
        return (cw[0:1, :] * up_ref[HALO - 2 + r0:HALO - 2 + r0 + nr, cs]
                + cw[1:2, :] * up_ref[HALO - 1 + r0:HALO - 1 + r0 + nr, cs]
                + cw[2:3, :] * up_ref[HALO + r0:HALO + r0 + nr, cs] + cb_ref[:, cs])

    def act(cs, r0, nr):
        return _bf(_gelu_tanh(conv(upg_ref, cwg_ref, cbg_ref, cs, r0, nr)) * conv(upv_ref, cwv_ref, cbv_ref, cs, r0, nr))

    head = [act(cs, 0, tm) for cs in subs[:-1]]
    nr = tm // FFN_ROW_BLOCKS
    for r0 in range(0, tm, nr):
        a = jnp.concatenate([x[r0:r0 + nr] for x in head] + [act(subs[-1], r0, nr)], axis=1)
        o_ref[r0:r0 + nr, :] += jnp.dot(a, wd_ref[...], preferred_element_type=F32)

    @pl.when(j == pl.num_programs(1) - 1)
    def _():
        o_ref[...] = h_ref[...] + _rms(o_ref[...], gout_ref[...])


def _ffn(h1, S, gin, w_up, conv_w, conv_b, w_down, gout, tm=1024, tf=512):
    T, D = h1.shape
    F = w_down.shape[0]
    nf = F // tf
    hb = tm // HALO
    return pl.pallas_call(
        functools.partial(_ffn_kernel, tm=tm, tiles_per_seq=S // tm),
        grid=(T // tm, nf),
        in_specs=[pl.BlockSpec((tm, D), lambda i, j: (i, 0), pipeline_mode=pl.Buffered(1)),
                  pl.BlockSpec((HALO, D), lambda i, j: (jnp.maximum(i * hb - 1, 0), 0)),
                  pl.BlockSpec((1, D), lambda i, j: (0, 0)),
                  pl.BlockSpec((D, tf), lambda i, j: (0, j)),
                  pl.BlockSpec((D, tf), lambda i, j: (0, nf + j)),
                  pl.BlockSpec((3, tf), lambda i, j: (0, j)),
                  pl.BlockSpec((3, tf), lambda i, j: (0, nf + j)),
                  pl.BlockSpec((1, tf), lambda i, j: (0, j)),
                  pl.BlockSpec((1, tf), lambda i, j: (0, nf + j)),
                  pl.BlockSpec((tf, D), lambda i, j: (j, 0)),
                  pl.BlockSpec((1, D), lambda i, j: (0, 0))],
        out_specs=pl.BlockSpec((tm, D), lambda i, j: (i, 0)),
        out_shape=jax.ShapeDtypeStruct((T, D), F32),
        scratch_shapes=[pltpu.VMEM((tm + HALO, D), BF16),
                        pltpu.VMEM((tm + HALO, tf), F32), pltpu.VMEM((tm + HALO, tf), F32)],
        compiler_params=_params("parallel", "arbitrary"),
        name="ffn",
    )(h1, h1, gin, w_up, w_up, conv_w, conv_w, conv_b, conv_b, w_down, gout)


def _ple_kernel(h_ref, p_ref, wpg_ref, wple_ref, g_ref, o_ref):
    h = h_ref[...]
    gate = jax.nn.sigmoid(jnp.dot(_bf(h), wpg_ref[...], preferred_element_type=F32))
    e = jnp.dot(_bf(p_ref[...]), wple_ref[...], preferred_element_type=F32)
    o_ref[...] = h + _rms(gate * e, g_ref[...])


def _ple(h2, p2, wpg, wple, g, tm=512):
    T, D = h2.shape
    full = lambda a: pl.BlockSpec(a.shape, lambda i: (0, 0))
    return pl.pallas_call(
        _ple_kernel,
        grid=(T // tm,),
        in_specs=[pl.BlockSpec((tm, D), lambda i: (i, 0)),
                  pl.BlockSpec((tm, p2.shape[1]), lambda i: (i, 0)),
                  full(wpg), full(wple), full(g)],
        out_specs=pl.BlockSpec((tm, D), lambda i: (i, 0)),
        out_shape=jax.ShapeDtypeStruct((T, D), F32),
        compiler_params=_params("parallel"),
        name="ple",
    )(h2, p2, wpg, wple, g)


def _swap_halves(w):
    half = w.shape[-1] // 2
    return jnp.concatenate([w[..., half:], w[..., :half]], axis=-1)


def _pack_w_in_kernel(off_ref, w_ref, kpe_ref, o_ref):
    j = pl.program_id(0)
    last = pl.num_programs(0) - 1

    @pl.when(j < last)
    def _():
        o_ref[...] = _bf(w_ref[...])

    @pl.when(j == last)
    def _():
        kpe = kpe_ref[...]
        half = MLA_ROPE // 2
        pad = jnp.zeros((W_IN_TILE - RW_LORA - 2 * MLA_ROPE, kpe.shape[1]), F32)
        o_ref[...] = _bf(jnp.concatenate([w_ref[0:RW_LORA, :], pad, kpe, kpe[half:], kpe[:half]], axis=0))


def _pack_w_in(w_in_t):
    N, D = w_in_t.shape
    rkv_end = 3 * RW_WIDTH
    m0 = rkv_end + RW_LORA
    g0 = m0 + 2 * MLA_RANK + MLA_ROPE
    starts = (list(range(0, rkv_end, W_IN_TILE)) + [m0, m0 + MLA_RANK]
              + list(range(g0, g0 + 4096, W_IN_TILE)) + [rkv_end])
    assert len(starts) * W_IN_TILE == W_IN_PACKED and N == g0 + 4096
    assert all(s % W_IN_ALIGN == 0 for s in starts)
    starts = [s // W_IN_ALIGN for s in starts]
    grid_spec = pltpu.PrefetchScalarGridSpec(
        num_scalar_prefetch=1,
        grid=(len(starts),),
        in_specs=[pl.BlockSpec((pl.Element(W_IN_TILE), pl.Element(D)),
                               lambda j, off: (pl.multiple_of(off[j] * W_IN_ALIGN, W_IN_ALIGN), 0)),
                  pl.BlockSpec((pl.Element(MLA_ROPE), pl.Element(D)), lambda j, off: (m0 + 2 * MLA_RANK, 0))],
        out_specs=pl.BlockSpec((W_IN_TILE, D), lambda j, off: (j, 0)))
    return pl.pallas_call(
        _pack_w_in_kernel,
        grid_spec=grid_spec,
        out_shape=jax.ShapeDtypeStruct((W_IN_PACKED, D), BF16),
        compiler_params=_params("arbitrary"),
        name="pack_w_in",
    )(jnp.asarray(starts, jnp.int32), w_in_t, w_in_t)


def _row(v):
    return v.reshape(1, -1).astype(F32)


def kernel(x, p, positions, pre_mix_norm, w_in, rw_mu, rw_w0, rw_w2, rw_a0, rw_a2, rw_g2, rw_k_k, rw_k_a, rw_r_k, rw_lnx_w, rw_lnx_b, mla_q_norm, mla_w_q_up, mla_kv_norm, mla_w_kv_up, w_branch_rw, w_branch_mla, w_out, post_mix_norm, pre_ffn_norm, w_up, conv_w, conv_b, w_down, post_ffn_norm, w_ple, w_ple_gate, ple_norm):
    B, S, D = x.shape
    T = B * S
    depth = w_in.shape[0]

    inv_freq = ROPE_BASE ** (-jnp.arange(0, MLA_ROPE, 2, dtype=F32) / MLA_ROPE)
    ang = positions.astype(F32)[..., None] * inv_freq
    cos = jnp.cos(ang).reshape(T, MLA_ROPE // 2)
    sin = jnp.sin(ang).reshape(T, MLA_ROPE // 2)
    zpad = jnp.zeros((T, LANES - MLA_ROPE), F32)
    cos_t = jnp.concatenate([cos, cos, zpad], axis=1)
    sin_t = jnp.concatenate([-sin, sin, zpad], axis=1)

    h = x.reshape(T, D)
    for i in range(depth):
        w_in_p = _pack_w_in(jnp.swapaxes(w_in, 1, 2)[i])
        z_all, (w_kv_b, w_rw_b, w_mla_b, w_out_b, w_up_b, w_down_b, w_pg_b, w_ple_b) = _in_proj(
            h, _row(pre_mix_norm[i]), w_in_p,
            [mla_w_kv_up[i], w_branch_rw[i], w_branch_mla[i], w_out[i], w_up[i], w_down[i], w_ple_gate[i], w_ple[i]])

        mu = rw_mu[i]
        mu_l = jnp.concatenate([mu[3 * RW_WIDTH:], jnp.zeros((512 - 288,), F32)]).reshape(1, 512)
        g2p = jnp.concatenate([rw_g2[i], jnp.zeros((256 - rw_g2.shape[1], RW_WIDTH), F32)], axis=0)
        w2p = jnp.concatenate([rw_w2[i]], axis=0)
        y_rw = _rwkv(z_all, S, _row(mu[:RW_WIDTH]), _row(mu[RW_WIDTH:2 * RW_WIDTH]), _row(mu[2 * RW_WIDTH:3 * RW_WIDTH]),
                     mu_l, _row(rw_w0[i]), _row(rw_a0[i]), _row(rw_k_k[i]), _row(rw_k_a[i]), _row(rw_r_k[i]),
                     _row(rw_lnx_w[i]), _row(rw_lnx_b[i]), _bf(w2p), _bf(rw_a2[i]), _bf(g2p))

        wq = mla_w_q_up[i].reshape(MLA_RANK, MLA_HEADS, MLA_NOPE + MLA_ROPE)
        wqn = wq[:, :, :MLA_NOPE].reshape(MLA_RANK, MLA_HEADS * MLA_NOPE)
        wq_pe = wq[:, :, MLA_NOPE:]
        padh = jnp.zeros((MLA_RANK, MLA_HEADS, LANES - MLA_ROPE), F32)
        wqp = jnp.concatenate([wq_pe, padh], axis=-1).reshape(MLA_RANK, MLA_HEADS * LANES)
        wqs = jnp.concatenate([_swap_halves(wq_pe), padh], axis=-1).reshape(MLA_RANK, MLA_HEADS * LANES)
        q, kv, kr = _mla_proj(z_all, cos_t, sin_t, _row(mla_q_norm[i]), _row(mla_kv_norm[i]),
                              _bf(wqn), _bf(wqp), _bf(wqs), w_kv_b)
        y_mla = _attention(q, kv, kr, S)

        h = _mix(y_rw, y_mla, z_all, h, w_rw_b, w_mla_b, w_out_b, _row(post_mix_norm[i]))
        h = _ffn(h, S, _row(pre_ffn_norm[i]), w_up_b, conv_w[i], _row(conv_b[i]), w_down_b, _row(post_ffn_norm[i]))
        h = _ple(h, p[i].reshape(T, -1), w_pg_b, w_ple_b, _row(ple_norm[i]))
    return h.reshape(B, S, D)
```

```python
import functools

import jax
import jax.numpy as jnp
from jax import lax
from jax.experimental import pallas as pl
from jax.experimental.pallas import tpu as pltpu

NORM_EPS = 1e-6
GN_EPS = 64e-5
CHUNK = 64
ROPE_BASE = 10000.0

RW_HEADS = 16
RW_N = 64
RW_WIDTH = RW_HEADS * RW_N
RW_L = 64
LANES = 128
RW_UNIT_HEADS = 2
RW_UNIT = RW_UNIT_HEADS * RW_N
RW_UNITS_PER_STEP = 8
RW_CHUNKS_PER_STEP = 2

MLA_HEADS = 8
MLA_NOPE = 128
MLA_ROPE = 64
MLA_V = 128
MLA_RANK = 512

RW_LORA = 288
W_IN_TILE = 512
W_IN_ALIGN = 32
W_IN_PACKED = 3 * RW_WIDTH + 2 * MLA_RANK + 4096 + W_IN_TILE

VMEM_LIMIT_BYTES = 56 * 1024 * 1024

BF16 = jnp.bfloat16
F32 = jnp.float32


def _params(*sem):
    return pltpu.CompilerParams(dimension_semantics=sem, vmem_limit_bytes=VMEM_LIMIT_BYTES)


def _bf(x):
    return x.astype(BF16)


def _mm(a, b):
    return jnp.dot(_bf(a), _bf(b), preferred_element_type=F32)


def _mm_nt(a, b):
    return lax.dot_general(_bf(a), _bf(b), (((1,), (1,)), ((), ())), preferred_element_type=F32)


def _rms(x, g):
    return x * lax.rsqrt(jnp.mean(x * x, axis=-1, keepdims=True) + NORM_EPS) * g


def _inproj_kernel(*refs, n_cast):
    x_ref, g_ref, w_ref = refs[:3]
    cast_in = refs[3:3 + n_cast]
    o_ref = refs[3 + n_cast]
    cast_out = refs[4 + n_cast:4 + 2 * n_cast]
    xn_ref = refs[-1]

    @pl.when(pl.program_id(1) == 0)
    def _():
        xn_ref[...] = _bf(_rms(x_ref[...], g_ref[...]))

    o_ref[...] = lax.dot_general(xn_ref[...], w_ref[...], (((1,), (1,)), ((), ())), preferred_element_type=F32)
    for src, dst in zip(cast_in, cast_out):
        dst[...] = _bf(src[...])


def _cast_rows(n_rows, n_steps):
    rb = 16
    while n_rows % rb or n_rows // rb > n_steps:
        rb += 16
    return rb


def _in_proj(x2, g, w_t, cast, tm=2048, tn=W_IN_TILE):
    T, D = x2.shape
    N = w_t.shape[0]
    ni, nj = T // tm, N // tn
    cast_specs = []
    for a in cast:
        rb = _cast_rows(a.shape[0], ni * nj)
        last = a.shape[0] // rb - 1
        cast_specs.append(pl.BlockSpec((rb, a.shape[1]), lambda i, j, last=last: (jnp.minimum(i * nj + j, last), 0)))
    outs = pl.pallas_call(
        functools.partial(_inproj_kernel, n_cast=len(cast)),
        grid=(ni, nj),
        in_specs=[
            pl.BlockSpec((tm, D), lambda i, j: (i, 0), pipeline_mode=pl.Buffered(1)),
            pl.BlockSpec((1, D), lambda i, j: (0, 0)),
            pl.BlockSpec((tn, D), lambda i, j: (j, 0)),
        ] + cast_specs,
        out_specs=[pl.BlockSpec((tm, tn), lambda i, j: (i, j))] + cast_specs,
        out_shape=[jax.ShapeDtypeStruct((T, N), F32)] + [jax.ShapeDtypeStruct(a.shape, BF16) for a in cast],
        scratch_shapes=[pltpu.VMEM((tm, D), BF16)],
        compiler_params=_params("arbitrary", "arbitrary"),
        name="in_proj",
    )(x2, g, w_t, *cast)
    return outs[0], outs[1:]


def _split3(x):
    hi = _bf(x)
    r1 = x - hi.astype(F32)
    mid = _bf(r1)
    lo = _bf(r1 - mid.astype(F32))
    return hi, mid, lo


def _rwkv_kernel(zr_ref, zk_ref, zv_ref, zl_ref, mur_ref, muk_ref, muv_ref, mul_ref,
                 w0_ref, a0_ref, kk_ref, ka_ref, rk_ref, lnw_ref, lnb_ref,
                 w2_ref, a2_ref, g2_ref, o_ref,
                 st_ref, pr_ref, pk_ref, pv_ref, plr_ref, *, units, chunks):
    L = RW_L
    UW = RW_UNIT
    c = pl.program_id(2)

    @pl.when(c == 0)
    def _():
        st_ref[...] = jnp.zeros_like(st_ref)
        pr_ref[...] = jnp.zeros_like(pr_ref)
        pk_ref[...] = jnp.zeros_like(pk_ref)
        pv_ref[...] = jnp.zeros_like(pv_ref)
        plr_ref[...] = jnp.zeros_like(plr_ref)

    def shift_lerp(z_ref, prev_ref, mu_ref):
        z = z_ref[...]
        rolled = pltpu.roll(z, 1, axis=0)
        row = lax.broadcasted_iota(jnp.int32, z.shape, 0)
        zs = jnp.where(row == 0, prev_ref[...], rolled)
        prev_ref[...] = z[z.shape[0] - 1:, :]
        return z + (zs - z) * mu_ref[...]

    r_all = shift_lerp(zr_ref, pr_ref, mur_ref)
    k_all = shift_lerp(zk_ref, pk_ref, muk_ref)
    v_all = shift_lerp(zv_ref, pv_ref, muv_ref)
    lo_all = shift_lerp(zl_ref, plr_ref, mul_ref)
    wd = jnp.tanh(lo_all[:, 0:64])
    ad = lo_all[:, 64:128]
    gd = jax.nn.sigmoid(lo_all[:, 128:384])

    hshift = RW_N.bit_length() - 1
    head_of_lane = lax.broadcasted_iota(jnp.int32, (1, UW), 1) >> hshift
    head_masks = [head_of_lane == j for j in range(RW_UNIT_HEADS)]
    row_l = lax.broadcasted_iota(jnp.int32, (L, UW), 0)
    col_l = lax.broadcasted_iota(jnp.int32, (L, UW), 1) & (RW_N - 1)
    strict = col_l < row_l
    incl = col_l <= row_l
    eye_pair = (col_l == row_l).astype(F32)
    row_s = lax.broadcasted_iota(jnp.int32, (UW, UW), 0)
    col_s = lax.broadcasted_iota(jnp.int32, (UW, UW), 1)
    same_head = (row_s >> hshift) == (col_s >> hshift)
    diag_s = row_s == col_s
    ones_bd = same_head.astype(BF16)

    def stack(x):
        return jnp.concatenate([jnp.where(m, x, 0.0) for m in head_masks], axis=0)

    def head_sum(x):
        return jnp.dot(_bf(x), ones_bd, preferred_element_type=F32)

    def cat(xs, axis):
        return jnp.concatenate(xs, axis=axis)

    U = range(units)
    uls = [slice(u * UW, (u + 1) * UW) for u in U]
    xw = [w0_ref[:, sl] + _mm(wd, w2_ref[:, sl]) for sl in uls]
    ga = [a0_ref[:, sl] + _mm(ad, a2_ref[:, sl]) for sl in uls]
    g_out = [_mm(gd, g2_ref[:, sl]) for sl in uls]
    kkv = [k_all[:, uls[u]] * kk_ref[:, uls[u]] for u in U]
    ss = [head_sum(x * x) for x in kkv]
    logw_u = [-0.6065306597126334 * jax.nn.sigmoid(x) for x in xw]
    gate_u = [jax.nn.sigmoid(x) for x in ga]
    kkn_u = [kkv[u] / jnp.maximum(jnp.sqrt(ss[u]), 1e-12) for u in U]
    k2_u = [k_all[:, uls[u]] * (1.0 + (gate_u[u] - 1.0) * ka_ref[:, uls[u]]) for u in U]
    bonus_in = [head_sum(r_all[:, uls[u]] * k2_u[u] * rk_ref[:, uls[u]]) for u in U]

    items = [(ci, u) for ci in range(chunks) for u in U]
    P = range(len(items))
    rows = [slice(ci * L, (ci + 1) * L) for ci, _ in items]
    r = [r_all[rows[p], uls[items[p][1]]] for p in P]
    v = [v_all[rows[p], uls[items[p][1]]] for p in P]
    logw = [logw_u[items[p][1]][rows[p]] for p in P]
    gate = [gate_u[items[p][1]][rows[p]] for p in P]
    kkn = [kkn_u[items[p][1]][rows[p]] for p in P]
    k2 = [k2_u[items[p][1]][rows[p]] for p in P]
    parts = [_split3(x) for x in logw]
    tri3 = ((lax.broadcasted_iota(jnp.int32, (L, 3 * L), 1) & (L - 1))
            <= lax.broadcasted_iota(jnp.int32, (L, 3 * L), 0)).astype(BF16)
    cum = [jnp.dot(tri3, cat(list(parts[p]), 0), preferred_element_type=F32) for p in P]
    cum_l = [c_[L - 1:L, :] for c_ in cum]
    e_neg = [jnp.exp(-c_) for c_ in cum]
    e_end = [jnp.exp(cum_l[p] - cum[p]) for p in P]
    kka = [kkn[p] * gate[p] for p in P]
    rt = [r[p] * jnp.exp(cum[p]) for p in P]
    kt = [k2[p] * e_neg[p] for p in P]
    bt = [kka[p] * e_neg[p] for p in P]
    at = [-kkn[p] * jnp.exp(cum[p] - logw[p]) for p in P]
    bk_t = [cat([kka[p] * e_end[p], k2[p] * e_end[p]], 0).T for p in P]
    w_end = [jnp.exp(c_) for c_ in cum_l]

    a_all = [_mm_nt(cat([at[p], rt[p]], 0), cat([stack(bt[p]), stack(kt[p])], 0)) for p in P]
    a_ab = [jnp.where(strict, a[:L, :UW], 0.0) for a in a_all]
    a_ak = [jnp.where(strict, a[:L, UW:], 0.0) for a in a_all]
    a_rb = [jnp.where(incl, a[L:, :UW], 0.0) for a in a_all]
    a_rk = [jnp.where(incl, a[L:, UW:], 0.0) for a in a_all]
    av = [_mm(a_ak[p], stack(v[p])) for p in P]

    tinv = [eye_pair + a for a in a_ab]
    pw = a_ab
    for _ in range(5):
        pw = [_mm(x, stack(x)) for x in pw]
        tinv = [tinv[p] + _mm(tinv[p], stack(pw[p])) for p in P]

    pq = [_mm(tinv[p], cat([stack(at[p]), stack(av[p])], 1)) for p in P]
    pm = [x[:, :UW] for x in pq]
    qm = [x[:, UW:] for x in pq]
    rm = [rt[p] + _mm(a_rb[p], stack(pm[p])) for p in P]
    y0 = [_mm(cat([a_rb[p], a_rk[p]], 1), cat([stack(qm[p]), stack(v[p])], 0)) for p in P]
    mn = [_mm(bk_t[p], cat([cat([pm[p], qm[p]], 1), cat([jnp.zeros_like(v[p]), v[p]], 1)], 0)) for p in P]
    m_mat = [jnp.where(diag_s, w_end[p], 0.0) + jnp.where(same_head, mn[p][:, :UW], 0.0) for p in P]
    n_mat = [jnp.where(same_head, mn[p][:, UW:], 0.0) for p in P]

    state = [st_ref[u] for u in U]
    y_chunks = []
    for ci in range(chunks):
        ps = [ci * units + u for u in U]
        ys = [_mm(cat([rm[p], m_mat[p]], 0), state[u]) for u, p in zip(U, ps)]
        state = [ys[u][L:] + n_mat[p] for u, p in zip(U, ps)]
        y_chunks.append([ys[u][:L] + y0[p] for u, p in zip(U, ps)])
    for u in U:
        st_ref[u] = state[u]

    y = [cat([y_chunks[ci][u] for ci in range(chunks)], 0) for u in U]
    mean = [head_sum(x) * (1.0 / RW_N) for x in y]
    yc = [y[u] - mean[u] for u in U]
    var = [head_sum(x * x) * (1.0 / RW_N) for x in yc]
    for u in U:
        yn = yc[u] * lax.rsqrt(var[u] + GN_EPS) * lnw_ref[:, uls[u]] + lnb_ref[:, uls[u]]
        o_ref[:, uls[u]] = _bf((yn + bonus_in[u] * v_all[:, uls[u]]) * g_out[u])


def _rwkv(z_all, S, mu_r, mu_k, mu_v, mu_l, w0, a0, kk, ka, rk, lnw, lnb, w2, a2, g2):
    T = z_all.shape[0]
    B = T // S
    R = RW_CHUNKS_PER_STEP * RW_L
    nc = S // R
    G = RW_UNITS_PER_STEP
    W = G * RW_UNIT
    ngrp = RW_WIDTH // W
    zspec = lambda off: pl.BlockSpec((R, W), lambda b, g, c, off=off: (b * nc + c, off * ngrp + g))
    vspec = pl.BlockSpec((1, W), lambda b, g, c: (0, g))
    lora_col = (z_all.shape[1] - 512) // 512
    return pl.pallas_call(
        functools.partial(_rwkv_kernel, units=G, chunks=RW_CHUNKS_PER_STEP),
        grid=(B, ngrp, nc),
        in_specs=[zspec(0), zspec(1), zspec(2),
                  pl.BlockSpec((R, 512), lambda b, g, c: (b * nc + c, lora_col)),
                  vspec, vspec, vspec,
                  pl.BlockSpec((1, 512), lambda b, g, c: (0, 0)),
                  vspec, vspec, vspec, vspec, vspec, vspec, vspec,
                  pl.BlockSpec((64, W), lambda b, g, c: (0, g)),
                  pl.BlockSpec((64, W), lambda b, g, c: (0, g)),
                  pl.BlockSpec((256, W), lambda b, g, c: (0, g))],
        out_specs=pl.BlockSpec((R, W), lambda b, g, c: (b * nc + c, g)),
        out_shape=jax.ShapeDtypeStruct((T, RW_WIDTH), BF16),
        scratch_shapes=[pltpu.VMEM((G, RW_UNIT, RW_UNIT), F32),
                        pltpu.VMEM((1, W), F32), pltpu.VMEM((1, W), F32), pltpu.VMEM((1, W), F32),
                        pltpu.VMEM((1, 512), F32)],
        compiler_params=_params("parallel", "parallel", "arbitrary"),
        name="rwkv",
    )(z_all, z_all, z_all, z_all, mu_r, mu_k, mu_v, mu_l, w0, a0, kk, ka, rk, lnw, lnb, w2, a2, g2)


def _mla_proj_kernel(cq_ref, ckv_ref, kpe_ref, cos_ref, sin_ref, qn_ref, kvn_ref,
                     wqn_ref, wqp_ref, wkv_ref, q_ref, kv_ref, kr_ref, *, scale):
    cqn = _bf(_rms(cq_ref[...], qn_ref[...]))
    ckvn = _bf(_rms(ckv_ref[...], kvn_ref[...]))
    cos = cos_ref[...]
    sin = sin_ref[...]
    q_nope = jnp.dot(cqn, wqn_ref[...], preferred_element_type=F32)
    q_pe = jnp.dot(cqn, wqp_ref[...], preferred_element_type=F32)
    half = MLA_ROPE // 2
    first_half = lax.broadcasted_iota(jnp.int32, (1, LANES), 1) < half
    for h in range(MLA_HEADS):
        hs = slice(h * LANES, (h + 1) * LANES)
        pe = q_pe[:, hs]
        swapped = jnp.where(first_half, pltpu.roll(pe, LANES - half, axis=1), pltpu.roll(pe, half, axis=1))
        q_ref[:, 2 * h * LANES:(2 * h + 1) * LANES] = _bf(q_nope[:, hs] * scale)
        q_ref[:, (2 * h + 1) * LANES:(2 * h + 2) * LANES] = _bf((pe * cos + swapped * sin) * scale)
    kv_ref[...] = _bf(jnp.dot(ckvn, wkv_ref[...], preferred_element_type=F32))
    blk = kpe_ref[...]
    kr_ref[...] = _bf(blk * cos + pltpu.roll(blk, MLA_ROPE, axis=1) * sin)


def _mla_proj(z_all, cos_t, sin_t, qn, kvn, wqn, wqp, wkv, tm=512):
    T = z_all.shape[0]
    HW = MLA_HEADS * 2 * LANES
    scale = float((MLA_NOPE + MLA_ROPE) ** -0.5 * 1.4426950408889634)
    full = lambda a: pl.BlockSpec(a.shape, lambda i: (0, 0))
    kpe_blk = (z_all.shape[1] - LANES) // LANES
    return pl.pallas_call(
        functools.partial(_mla_proj_kernel, scale=scale),
        grid=(T // tm,),
        in_specs=[pl.BlockSpec((tm, MLA_RANK), lambda i: (i, 6)),
                  pl.BlockSpec((tm, MLA_RANK), lambda i: (i, 7)),
                  pl.BlockSpec((tm, LANES), lambda i: (i, kpe_blk)),
                  pl.BlockSpec((tm, LANES), lambda i: (i, 0)),
                  pl.BlockSpec((tm, LANES), lambda i: (i, 0)),
                  full(qn), full(kvn), full(wqn), full(wqp), full(wkv)],
        out_specs=[pl.BlockSpec((tm, HW), lambda i: (i, 0)),
                   pl.BlockSpec((tm, HW), lambda i: (i, 0)),
                   pl.BlockSpec((tm, LANES), lambda i: (i, 0))],
        out_shape=[jax.ShapeDtypeStruct((T, HW), BF16),
                   jax.ShapeDtypeStruct((T, HW), BF16),
                   jax.ShapeDtypeStruct((T, LANES), BF16)],
        compiler_params=_params("parallel"),
        name="mla_proj",
    )(z_all, z_all, z_all, cos_t, sin_t, qn, kvn, wqn, wqp, wkv)


ATT_HEADS_PER_STEP = 8


def _attn_kernel(qi_ref, ki_ref, q_ref, kv_ref, kr_ref, o_ref, m_ref, l_ref, acc_ref, *, tq, tk, heads):
    t = pl.program_id(2)
    qi = qi_ref[t]
    ki = ki_ref[t]
    H = range(heads)

    @pl.when(ki == 0)
    def _():
        m_ref[...] = jnp.full_like(m_ref, -1e30)
        l_ref[...] = jnp.zeros_like(l_ref)
        acc_ref[...] = jnp.zeros_like(acc_ref)

    def update(masked):
        kr = kr_ref[...]
        s = [lax.dot_general(q_ref[:, 2 * h * LANES:(2 * h + 2) * LANES],
                             jnp.concatenate([kv_ref[:, 2 * h * LANES:(2 * h + 1) * LANES], kr], axis=1),
                             (((1,), (1,)), ((), ())), preferred_element_type=F32) for h in H]
        if masked:
            q_last = (qi * tq + lax.broadcasted_iota(jnp.int32, (tq, tk), 0)) | (CHUNK - 1)
            k_pos = ki * tk + lax.broadcasted_iota(jnp.int32, (tq, tk), 1)
            vis = k_pos <= q_last
            s = [jnp.where(vis, x, -1e30) for x in s]
        m_prev = [m_ref[h] for h in H]
        m_new = [jnp.maximum(m_prev[h], jnp.max(s[h], axis=-1, keepdims=True)) for h in H]
        p = [jnp.exp2(s[h] - jnp.concatenate([m_new[h]] * (tk // LANES), axis=1)) for h in H]
        alpha = [jnp.exp2(m_prev[h] - m_new[h]) for h in H]
        pv = [jnp.dot(_bf(p[h]), kv_ref[:, (2 * h + 1) * LANES:(2 * h + 2) * LANES], preferred_element_type=F32) for h in H]
        for h in H:
            psum = sum(p[h][:, c:c + LANES] for c in range(0, tk, LANES))
            l_ref[h] = alpha[h] * l_ref[h] + psum
            acc_ref[h] = alpha[h] * acc_ref[h] + pv[h]
            m_ref[h] = m_new[h]

    @pl.when(ki < qi)
    def _():
        update(False)

    @pl.when(ki == qi)
    def _():
        update(True)
        for h in H:
            o_ref[:, h * LANES:(h + 1) * LANES] = _bf(acc_ref[h] / jnp.sum(l_ref[h], axis=-1, keepdims=True))


def _attention(q, kv, kr, S, tq=512):
    T = q.shape[0]
    B = T // S
    nq = S // tq
    G = ATT_HEADS_PER_STEP
    pairs = [(a, b) for a in range(nq) for b in range(a + 1)]
    qi_tab = jnp.asarray([a for a, _ in pairs], jnp.int32)
    ki_tab = jnp.asarray([b for _, b in pairs], jnp.int32)
    grid_spec = pltpu.PrefetchScalarGridSpec(
        num_scalar_prefetch=2,
        grid=(B, MLA_HEADS // G, len(pairs)),
        in_specs=[pl.BlockSpec((tq, 2 * G * LANES), lambda b, g, t, qt, kt: (b * nq + qt[t], g)),
                  pl.BlockSpec((tq, 2 * G * LANES), lambda b, g, t, qt, kt: (b * nq + kt[t], g)),
                  pl.BlockSpec((tq, LANES), lambda b, g, t, qt, kt: (b * nq + kt[t], 0))],
        out_specs=pl.BlockSpec((tq, G * LANES), lambda b, g, t, qt, kt: (b * nq + qt[t], g)),
        scratch_shapes=[pltpu.VMEM((G, tq, LANES), F32), pltpu.VMEM((G, tq, LANES), F32), pltpu.VMEM((G, tq, LANES), F32)])
    return pl.pallas_call(
        functools.partial(_attn_kernel, tq=tq, tk=tq, heads=G),
        grid_spec=grid_spec,
        out_shape=jax.ShapeDtypeStruct((T, MLA_HEADS * MLA_V), BF16),
        compiler_params=_params("parallel", "parallel", "arbitrary"),
        name="attention",
    )(qi_tab, ki_tab, q, kv, kr)


def _mix_kernel(yrw_ref, ymla_ref, gate_ref, x_ref, wa_ref, wb_ref, wo_ref, g_ref, o_ref):
    D = x_ref.shape[1]
    t = (jax.nn.sigmoid(gate_ref[:, :D]) * jnp.dot(yrw_ref[...], wa_ref[...], preferred_element_type=F32)
         + jax.nn.sigmoid(gate_ref[:, D:]) * jnp.dot(ymla_ref[...], wb_ref[...], preferred_element_type=F32))
    mix = jnp.dot(_bf(t), wo_ref[...], preferred_element_type=F32)
    o_ref[...] = x_ref[...] + _rms(mix, g_ref[...])


def _resident(a):
    return pl.BlockSpec(a.shape, lambda *_: (0,) * a.ndim, pipeline_mode=pl.Buffered(1))


def _mix(y_rw, y_mla, z_all, x2, wa, wb, wo, g, tm=256):
    T, D = x2.shape
    return pl.pallas_call(
        _mix_kernel,
        grid=(T // tm,),
        in_specs=[pl.BlockSpec((tm, y_rw.shape[1]), lambda i: (i, 0)),
                  pl.BlockSpec((tm, y_mla.shape[1]), lambda i: (i, 0)),
                  pl.BlockSpec((tm, 2 * D), lambda i: (i, 1)),
                  pl.BlockSpec((tm, D), lambda i: (i, 0)),
                  _resident(wa), _resident(wb), _resident(wo), _resident(g)],
        out_specs=pl.BlockSpec((tm, D), lambda i: (i, 0)),
        out_shape=jax.ShapeDtypeStruct((T, D), F32),
        compiler_params=_params("parallel"),
        name="mix",
    )(y_rw, y_mla, z_all, x2, wa, wb, wo, g)


HALO = 16
FFN_SUB = 256
FFN_ROW_BLOCKS = (256, 256, 256, 256)


def _gelu_tanh(x):
    return 0.5 * x * (1.0 + jnp.tanh(0.7978845608028654 * (x + 0.044715 * x * x * x)))


def _ffn_kernel(h_ref, halo_ref, gin_ref, wug_ref, wuv_ref, cwg_ref, cwv_ref, cbg_ref, cbv_ref,
                wd_ref, gout_ref, o_ref, xn_ref, upg_ref, upv_ref, *, tm, tiles_per_seq):
    i = pl.program_id(0)
    j = pl.program_id(1)

    @pl.when(j == 0)
    def _():
        xn_ref[0:HALO, :] = _bf(_rms(halo_ref[...], gin_ref[...]))
        xn_ref[HALO:, :] = _bf(_rms(h_ref[...], gin_ref[...]))
        o_ref[...] = jnp.zeros_like(o_ref)

    keep = jnp.where((i % tiles_per_seq) == 0, 0.0, 1.0).astype(F32)
    tf = upg_ref.shape[1]
    subs = [slice(c, c + FFN_SUB) for c in range(0, tf, FFN_SUB)]

    for w_ref, up_ref in ((wug_ref, upg_ref), (wuv_ref, upv_ref)):
        for cs in subs:
            up = jnp.dot(xn_ref[...], w_ref[:, cs], preferred_element_type=F32)
            up_ref[0:HALO, cs] = up[0:HALO] * keep
            up_ref[HALO:, cs] = up[HALO:]

    def conv(up_ref, cw_ref, cb_ref, cs, r0, nr):
        cw = cw_ref[:, cs]
        return (cw[0:1, :] * up_ref[HALO - 2 + r0:HALO - 2 + r0 + nr, cs]
                + cw[1:2, :] * up_ref[HALO - 1 + r0:HALO - 1 + r0 + nr, cs]
                + cw[2:3, :] * up_ref[HALO + r0:HALO + r0 + nr, cs] + cb_ref[:, cs])

    def act(cs, r0, nr):
        return _bf(_gelu_tanh(conv(upg_ref, cwg_ref, cbg_ref, cs, r0, nr)) * conv(upv_ref, cwv_ref, cbv_ref, cs, r0, nr))

    head = [act(cs, 0, tm) for cs in subs[:-1]]
    r0 = 0
    for nr in FFN_ROW_BLOCKS:
        a = jnp.concatenate([x[r0:r0 + nr] for x in head] + [act(subs[-1], r0, nr)], axis=1)
        o_ref[r0:r0 + nr, :] += jnp.dot(a, wd_ref[...], preferred_element_type=F32)
        r0 += nr
    assert r0 == tm

    @pl.when(j == pl.num_programs(1) - 1)
    def _():
        o_ref[...] = h_ref[...] + _rms(o_ref[...], gout_ref[...])


def _ffn(h1, S, gin, w_up, conv_w, conv_b, w_down, gout, tm=1024, tf=512):
    T, D = h1.shape
    F = w_down.shape[0]
    nf = F // tf
    hb = tm // HALO
    return pl.pallas_call(
        functools.partial(_ffn_kernel, tm=tm, tiles_per_seq=S // tm),
        grid=(T // tm, nf),
        in_specs=[pl.BlockSpec((tm, D), lambda i, j: (i, 0), pipeline_mode=pl.Buffered(1)),
                  pl.BlockSpec((HALO, D), lambda i, j: (jnp.maximum(i * hb - 1, 0), 0)),
                  pl.BlockSpec((1, D), lambda i, j: (0, 0)),
                  pl.BlockSpec((D, tf), lambda i, j: (0, j)),
                  pl.BlockSpec((D, tf), lambda i, j: (0, nf + j)),
                  pl.BlockSpec((3, tf), lambda i, j: (0, j)),
                  pl.BlockSpec((3, tf), lambda i, j: (0, nf + j)),
                  pl.BlockSpec((1, tf), lambda i, j: (0, j)),
                  pl.BlockSpec((1, tf), lambda i, j: (0, nf + j)),
                  pl.BlockSpec((tf, D), lambda i, j: (j, 0)),
                  pl.BlockSpec((1, D), lambda i, j: (0, 0))],
        out_specs=pl.BlockSpec((tm, D), lambda i, j: (i, 0)),
        out_shape=jax.ShapeDtypeStruct((T, D), F32),
        scratch_shapes=[pltpu.VMEM((tm + HALO, D), BF16),
                        pltpu.VMEM((tm + HALO, tf), F32), pltpu.VMEM((tm + HALO, tf), F32)],
        compiler_params=_params("parallel", "arbitrary"),
        name="ffn",
    )(h1, h1, gin, w_up, w_up, conv_w, conv_w, conv_b, conv_b, w_down, gout)


def _ple_kernel(h_ref, p_ref, wpg_ref, wple_ref, g_ref, o_ref):
    h = h_ref[...]
    gate = jax.nn.sigmoid(jnp.dot(_bf(h), wpg_ref[...], preferred_element_type=F32))
    e = jnp.dot(_bf(p_ref[...]), wple_ref[...], preferred_element_type=F32)
    o_ref[...] = h + _rms(gate * e, g_ref[...])


def _ple(h2, p2, wpg, wple, g, tm=512):
    T, D = h2.shape
    full = lambda a: pl.BlockSpec(a.shape, lambda i: (0, 0))
    return pl.pallas_call(
        _ple_kernel,
        grid=(T // tm,),
        in_specs=[pl.BlockSpec((tm, D), lambda i: (i, 0)),
                  pl.BlockSpec((tm, p2.shape[1]), lambda i: (i, 0)),
                  full(wpg), full(wple), full(g)],
        out_specs=pl.BlockSpec((tm, D), lambda i: (i, 0)),
        out_shape=jax.ShapeDtypeStruct((T, D), F32),
        compiler_params=_params("parallel"),
        name="ple",
    )(h2, p2, wpg, wple, g)


def _swap_halves(w):
    half = w.shape[-1] // 2
    return jnp.concatenate([w[..., half:], w[..., :half]], axis=-1)


def _pack_w_in_kernel(off_ref, w_ref, kpe_ref, o_ref):
    j = pl.program_id(0)
    last = pl.num_programs(0) - 1

    @pl.when(j < last)
    def _():
        o_ref[...] = _bf(w_ref[...])

    @pl.when(j == last)
    def _():
        kpe = kpe_ref[...]
        half = MLA_ROPE // 2
        pad = jnp.zeros((W_IN_TILE - RW_LORA - 2 * MLA_ROPE, kpe.shape[1]), F32)
        o_ref[...] = _bf(jnp.concatenate([w_ref[0:RW_LORA, :], pad, kpe, kpe[half:], kpe[:half]], axis=0))


def _pack_w_in(w_in_t):
    N, D = w_in_t.shape
    rkv_end = 3 * RW_WIDTH
    m0 = rkv_end + RW_LORA
    g0 = m0 + 2 * MLA_RANK + MLA_ROPE
    starts = (list(range(0, rkv_end, W_IN_TILE)) + [m0, m0 + MLA_RANK]
              + list(range(g0, g0 + 4096, W_IN_TILE)) + [rkv_end])
    assert len(starts) * W_IN_TILE == W_IN_PACKED and N == g0 + 4096
    assert all(s % W_IN_ALIGN == 0 for s in starts)
    starts = [s // W_IN_ALIGN for s in starts]
    grid_spec = pltpu.PrefetchScalarGridSpec(
        num_scalar_prefetch=1,
        grid=(len(starts),),
        in_specs=[pl.BlockSpec((pl.Element(W_IN_TILE), pl.Element(D)),
                               lambda j, off: (pl.multiple_of(off[j] * W_IN_ALIGN, W_IN_ALIGN), 0)),
                  pl.BlockSpec((pl.Element(MLA_ROPE), pl.Element(D)), lambda j, off: (m0 + 2 * MLA_RANK, 0))],
        out_specs=pl.BlockSpec((W_IN_TILE, D), lambda j, off: (j, 0)))
    return pl.pallas_call(
        _pack_w_in_kernel,
        grid_spec=grid_spec,
        out_shape=jax.ShapeDtypeStruct((W_IN_PACKED, D), BF16),
        compiler_params=_params("arbitrary"),
        name="pack_w_in",
    )(jnp.asarray(starts, jnp.int32), w_in_t, w_in_t)


def _row(v):
    return v.reshape(1, -1).astype(F32)


def kernel(x, p, positions, pre_mix_norm, w_in, rw_mu, rw_w0, rw_w2, rw_a0, rw_a2, rw_g2, rw_k_k, rw_k_a, rw_r_k, rw_lnx_w, rw_lnx_b, mla_q_norm, mla_w_q_up, mla_kv_norm, mla_w_kv_up, w_branch_rw, w_branch_mla, w_out, post_mix_norm, pre_ffn_norm, w_up, conv_w, conv_b, w_down, post_ffn_norm, w_ple, w_ple_gate, ple_norm):
    B, S, D = x.shape
    T = B * S
    depth = w_in.shape[0]

    inv_freq = ROPE_BASE ** (-jnp.arange(0, MLA_ROPE, 2, dtype=F32) / MLA_ROPE)
    ang = positions.astype(F32)[..., None] * inv_freq
    cos = jnp.cos(ang).reshape(T, MLA_ROPE // 2)
    sin = jnp.sin(ang).reshape(T, MLA_ROPE // 2)
    zpad = jnp.zeros((T, LANES - MLA_ROPE), F32)
    cos_t = jnp.concatenate([cos, cos, zpad], axis=1)
    sin_t = jnp.concatenate([-sin, sin, zpad], axis=1)

    h = x.reshape(T, D)
    for i in range(depth):
        w_in_p = _pack_w_in(jnp.swapaxes(w_in, 1, 2)[i])
        z_all, (w_kv_b, w_rw_b, w_mla_b, w_out_b, w_up_b, w_down_b, w_pg_b, w_ple_b) = _in_proj(
            h, _row(pre_mix_norm[i]), w_in_p,
            [mla_w_kv_up[i], w_branch_rw[i], w_branch_mla[i], w_out[i], w_up[i], w_down[i], w_ple_gate[i], w_ple[i]])

        mu = rw_mu[i]
        mu_l = jnp.concatenate([mu[3 * RW_WIDTH:], jnp.zeros((512 - 288,), F32)]).reshape(1, 512)
        g2p = jnp.concatenate([rw_g2[i], jnp.zeros((256 - rw_g2.shape[1], RW_WIDTH), F32)], axis=0)
        w2p = jnp.concatenate([rw_w2[i]], axis=0)
        y_rw = _rwkv(z_all, S, _row(mu[:RW_WIDTH]), _row(mu[RW_WIDTH:2 * RW_WIDTH]), _row(mu[2 * RW_WIDTH:3 * RW_WIDTH]),
                     mu_l, _row(rw_w0[i]), _row(rw_a0[i]), _row(rw_k_k[i]), _row(rw_k_a[i]), _row(rw_r_k[i]),
                     _row(rw_lnx_w[i]), _row(rw_lnx_b[i]), _bf(w2p), _bf(rw_a2[i]), _bf(g2p))

        wq = mla_w_q_up[i].reshape(MLA_RANK, MLA_HEADS, MLA_NOPE + MLA_ROPE)
        wqn = wq[:, :, :MLA_NOPE].reshape(MLA_RANK, MLA_HEADS * MLA_NOPE)
        wq_pe = wq[:, :, MLA_NOPE:]
        padh = jnp.zeros((MLA_RANK, MLA_HEADS, LANES - MLA_ROPE), F32)
        wqp = jnp.concatenate([wq_pe, padh], axis=-1).reshape(MLA_RANK, MLA_HEADS * LANES)
        q, kv, kr = _mla_proj(z_all, cos_t, sin_t, _row(mla_q_norm[i]), _row(mla_kv_norm[i]),
                              _bf(wqn), _bf(wqp), w_kv_b)
        y_mla = _attention(q, kv, kr, S)

        h = _mix(y_rw, y_mla, z_all, h, w_rw_b, w_mla_b, w_out_b, _row(post_mix_norm[i]))
        h = _ffn(h, S, _row(pre_ffn_norm[i]), w_up_b, conv_w[i], _row(conv_b[i]), w_down_b, _row(post_ffn_norm[i]))
        h = _ple(h, p[i].reshape(T, -1), w_pg_b, w_ple_b, _row(ple_norm[i]))
    return h.reshape(B, S, D)
```

```python
import functools

import jax
import jax.numpy as jnp
from jax import lax
from jax.experimental import pallas as pl
from jax.experimental.pallas import tpu as pltpu

NORM_EPS = 1e-6
GN_EPS = 64e-5
CHUNK = 64
ROPE_BASE = 10000.0

RW_HEADS = 16
RW_N = 64
RW_WIDTH = RW_HEADS * RW_N
RW_L = 64
LANES = 128
RW_UNIT_HEADS = 2
RW_UNIT = RW_UNIT_HEADS * RW_N
RW_UNITS_PER_STEP = 8
RW_CHUNKS_PER_STEP = 2

MLA_HEADS = 8
MLA_NOPE = 128
MLA_ROPE = 64
MLA_V = 128
MLA_RANK = 512

RW_LORA = 288
W_IN_TILE = 512
W_IN_ALIGN = 32
W_IN_PACKED = 3 * RW_WIDTH + 2 * MLA_RANK + 4096 + W_IN_TILE

VMEM_LIMIT_BYTES = 56 * 1024 * 1024

BF16 = jnp.bfloat16
F32 = jnp.float32
Z_DTYPE = BF16


def _params(*sem):
    return pltpu.CompilerParams(dimension_semantics=sem, vmem_limit_bytes=VMEM_LIMIT_BYTES)


def _bf(x):
    return x.astype(BF16)


def _mm(a, b):
    return jnp.dot(_bf(a), _bf(b), preferred_element_type=F32)


def _mm_nt(a, b):
    return lax.dot_general(_bf(a), _bf(b), (((1,), (1,)), ((), ())), preferred_element_type=F32)


def _rms(x, g):
    return x * lax.rsqrt(jnp.mean(x * x, axis=-1, keepdims=True) + NORM_EPS) * g


def _inproj_kernel(*refs, n_cast):
    x_ref, g_ref, w_ref = refs[:3]
    cast_in = refs[3:3 + n_cast]
    o_ref = refs[3 + n_cast]
    cast_out = refs[4 + n_cast:4 + 2 * n_cast]
    xn_ref = refs[-1]

    @pl.when(pl.program_id(1) == 0)
    def _():
        xn_ref[...] = _bf(_rms(x_ref[...], g_ref[...]))

    o_ref[...] = lax.dot_general(xn_ref[...], w_ref[...], (((1,), (1,)), ((), ())),
                                 preferred_element_type=F32).astype(o_ref.dtype)
    for src, dst in zip(cast_in, cast_out):
        dst[...] = _bf(src[...])


def _cast_rows(n_rows, n_steps):
    rb = 16
    while n_rows % rb or n_rows // rb > n_steps:
        rb += 16
    return rb


def _in_proj(x2, g, w_t, cast, tm=2048, tn=W_IN_TILE):
    T, D = x2.shape
    N = w_t.shape[0]
    ni, nj = T // tm, N // tn
    cast_specs = []
    for a in cast:
        rb = _cast_rows(a.shape[0], ni * nj)
        last = a.shape[0] // rb - 1
        cast_specs.append(pl.BlockSpec((rb, a.shape[1]), lambda i, j, last=last: (jnp.minimum(i * nj + j, last), 0)))
    outs = pl.pallas_call(
        functools.partial(_inproj_kernel, n_cast=len(cast)),
        grid=(ni, nj),
        in_specs=[
            pl.BlockSpec((tm, D), lambda i, j: (i, 0), pipeline_mode=pl.Buffered(1)),
            pl.BlockSpec((1, D), lambda i, j: (0, 0)),
            pl.BlockSpec((tn, D), lambda i, j: (j, 0)),
        ] + cast_specs,
        out_specs=[pl.BlockSpec((tm, tn), lambda i, j: (i, j))] + cast_specs,
        out_shape=[jax.ShapeDtypeStruct((T, N), Z_DTYPE)] + [jax.ShapeDtypeStruct(a.shape, BF16) for a in cast],
        scratch_shapes=[pltpu.VMEM((tm, D), BF16)],
        compiler_params=_params("arbitrary", "arbitrary"),
        name="in_proj",
    )(x2, g, w_t, *cast)
    return outs[0], outs[1:]


def _split3(x):
    hi = _bf(x)
    r1 = x - hi.astype(F32)
    mid = _bf(r1)
    lo = _bf(r1 - mid.astype(F32))
    return hi, mid, lo


def _rwkv_kernel(zr_ref, zk_ref, zv_ref, zl_ref, mur_ref, muk_ref, muv_ref, mul_ref,
                 w0_ref, a0_ref, kk_ref, ka_ref, rk_ref, lnw_ref, lnb_ref,
                 w2_ref, a2_ref, g2_ref, o_ref,
                 st_ref, pr_ref, pk_ref, pv_ref, plr_ref, *, units, chunks):
    L = RW_L
    UW = RW_UNIT
    c = pl.program_id(2)

    @pl.when(c == 0)
    def _():
        st_ref[...] = jnp.zeros_like(st_ref)
        pr_ref[...] = jnp.zeros_like(pr_ref)
        pk_ref[...] = jnp.zeros_like(pk_ref)
        pv_ref[...] = jnp.zeros_like(pv_ref)
        plr_ref[...] = jnp.zeros_like(plr_ref)

    def shift_lerp(z_ref, prev_ref, mu_ref):
        z = z_ref[...].astype(F32)
        rolled = pltpu.roll(z, 1, axis=0)
        row = lax.broadcasted_iota(jnp.int32, z.shape, 0)
        zs = jnp.where(row == 0, prev_ref[...], rolled)
        prev_ref[...] = z[z.shape[0] - 1:, :]
        return z + (zs - z) * mu_ref[...]

    r_all = shift_lerp(zr_ref, pr_ref, mur_ref)
    k_all = shift_lerp(zk_ref, pk_ref, muk_ref)
    v_all = shift_lerp(zv_ref, pv_ref, muv_ref)
    lo_all = shift_lerp(zl_ref, plr_ref, mul_ref)
    wd = jnp.tanh(lo_all[:, 0:64])
    ad = lo_all[:, 64:128]
    gd = jax.nn.sigmoid(lo_all[:, 128:384])

    hshift = RW_N.bit_length() - 1
    head_of_lane = lax.broadcasted_iota(jnp.int32, (1, UW), 1) >> hshift
    head_masks = [head_of_lane == j for j in range(RW_UNIT_HEADS)]
    row_l = lax.broadcasted_iota(jnp.int32, (L, UW), 0)
    col_l = lax.broadcasted_iota(jnp.int32, (L, UW), 1) & (RW_N - 1)
    strict = col_l < row_l
    incl = col_l <= row_l
    eye_pair = (col_l == row_l).astype(F32)
    row_s = lax.broadcasted_iota(jnp.int32, (UW, UW), 0)
    col_s = lax.broadcasted_iota(jnp.int32, (UW, UW), 1)
    same_head = (row_s >> hshift) == (col_s >> hshift)
    diag_s = row_s == col_s
    ones_bd = same_head.astype(BF16)

    def stack(x):
        return jnp.concatenate([jnp.where(m, x, 0.0) for m in head_masks], axis=0)

    def head_sum(x):
        return jnp.dot(_bf(x), ones_bd, preferred_element_type=F32)

    def cat(xs, axis):
        return jnp.concatenate(xs, axis=axis)

    U = range(units)
    uls = [slice(u * UW, (u + 1) * UW) for u in U]
    xw = [w0_ref[:, sl] + _mm(wd, w2_ref[:, sl]) for sl in uls]
    ga = [a0_ref[:, sl] + _mm(ad, a2_ref[:, sl]) for sl in uls]
    g_out = [_mm(gd, g2_ref[:, sl]) for sl in uls]
    kkv = [k_all[:, uls[u]] * kk_ref[:, uls[u]] for u in U]
    ss = [head_sum(x * x) for x in kkv]
    logw_u = [-0.6065306597126334 * jax.nn.sigmoid(x) for x in xw]
    gate_u = [jax.nn.sigmoid(x) for x in ga]
    kkn_u = [kkv[u] / jnp.maximum(jnp.sqrt(ss[u]), 1e-12) for u in U]
    k2_u = [k_all[:, uls[u]] * (1.0 + (gate_u[u] - 1.0) * ka_ref[:, uls[u]]) for u in U]
    bonus_in = [head_sum(r_all[:, uls[u]] * k2_u[u] * rk_ref[:, uls[u]]) for u in U]

    items = [(ci, u) for ci in range(chunks) for u in U]
    P = range(len(items))
    rows = [slice(ci * L, (ci + 1) * L) for ci, _ in items]
    r = [r_all[rows[p], uls[items[p][1]]] for p in P]
    v = [v_all[rows[p], uls[items[p][1]]] for p in P]
    logw = [logw_u[items[p][1]][rows[p]] for p in P]
    gate = [gate_u[items[p][1]][rows[p]] for p in P]
    kkn = [kkn_u[items[p][1]][rows[p]] for p in P]
    k2 = [k2_u[items[p][1]][rows[p]] for p in P]
    parts = [_split3(x) for x in logw]
    tri3 = ((lax.broadcasted_iota(jnp.int32, (L, 3 * L), 1) & (L - 1))
            <= lax.broadcasted_iota(jnp.int32, (L, 3 * L), 0)).astype(BF16)
    cum = [jnp.dot(tri3, cat(list(parts[p]), 0), preferred_element_type=F32) for p in P]
    cum_l = [c_[L - 1:L, :] for c_ in cum]
    e_neg = [jnp.exp(-c_) for c_ in cum]
    e_end = [jnp.exp(cum_l[p] - cum[p]) for p in P]
    kka = [kkn[p] * gate[p] for p in P]
    rt = [r[p] * jnp.exp(cum[p]) for p in P]
    kt = [k2[p] * e_neg[p] for p in P]
    bt = [kka[p] * e_neg[p] for p in P]
    at = [-kkn[p] * jnp.exp(cum[p] - logw[p]) for p in P]
    bk_t = [cat([kka[p] * e_end[p], k2[p] * e_end[p]], 0).T for p in P]
    w_end = [jnp.exp(c_) for c_ in cum_l]

    a_all = [_mm_nt(cat([at[p], rt[p]], 0), cat([stack(bt[p]), stack(kt[p])], 0)) for p in P]
    a_ab = [jnp.where(strict, a[:L, :UW], 0.0) for a in a_all]
    a_ak = [jnp.where(strict, a[:L, UW:], 0.0) for a in a_all]
    a_rb = [jnp.where(incl, a[L:, :UW], 0.0) for a in a_all]
    a_rk = [jnp.where(incl, a[L:, UW:], 0.0) for a in a_all]
    av = [_mm(a_ak[p], stack(v[p])) for p in P]

    tinv = [eye_pair + a for a in a_ab]
    pw = a_ab
    for _ in range(5):
        pw = [_mm(x, stack(x)) for x in pw]
        tinv = [tinv[p] + _mm(tinv[p], stack(pw[p])) for p in P]

    pq = [_mm(tinv[p], cat([stack(at[p]), stack(av[p])], 1)) for p in P]
    pm = [x[:, :UW] for x in pq]
    qm = [x[:, UW:] for x in pq]
    rm = [rt[p] + _mm(a_rb[p], stack(pm[p])) for p in P]
    y0 = [_mm(cat([a_rb[p], a_rk[p]], 1), cat([stack(qm[p]), stack(v[p])], 0)) for p in P]
    mn = [_mm(bk_t[p], cat([cat([pm[p], qm[p]], 1), cat([jnp.zeros_like(v[p]), v[p]], 1)], 0)) for p in P]
    m_mat = [jnp.where(diag_s, w_end[p], 0.0) + jnp.where(same_head, mn[p][:, :UW], 0.0) for p in P]
    n_mat = [jnp.where(same_head, mn[p][:, UW:], 0.0) for p in P]

    state = [st_ref[u] for u in U]
    y_chunks = []
    for ci in range(chunks):
        ps = [ci * units + u for u in U]
        ys = [_mm(cat([rm[p], m_mat[p]], 0), state[u]) for u, p in zip(U, ps)]
        state = [ys[u][L:] + n_mat[p] for u, p in zip(U, ps)]
        y_chunks.append([ys[u][:L] + y0[p] for u, p in zip(U, ps)])
    for u in U:
        st_ref[u] = state[u]

    y = [cat([y_chunks[ci][u] for ci in range(chunks)], 0) for u in U]
    mean = [head_sum(x) * (1.0 / RW_N) for x in y]
    yc = [y[u] - mean[u] for u in U]
    var = [head_sum(x * x) * (1.0 / RW_N) for x in yc]
    for u in U:
        yn = yc[u] * lax.rsqrt(var[u] + GN_EPS) * lnw_ref[:, uls[u]] + lnb_ref[:, uls[u]]
        o_ref[:, uls[u]] = _bf((yn + bonus_in[u] * v_all[:, uls[u]]) * g_out[u])


def _rwkv(z_all, S, mu_r, mu_k, mu_v, mu_l, w0, a0, kk, ka, rk, lnw, lnb, w2, a2, g2):
    T = z_all.shape[0]
    B = T // S
    R = RW_CHUNKS_PER_STEP * RW_L
    nc = S // R
    G = RW_UNITS_PER_STEP
    W = G * RW_UNIT
    ngrp = RW_WIDTH // W
    zspec = lambda off: pl.BlockSpec((R, W), lambda b, g, c, off=off: (b * nc + c, off * ngrp + g))
    vspec = pl.BlockSpec((1, W), lambda b, g, c: (0, g))
    lora_col = (z_all.shape[1] - 512) // 512
    return pl.pallas_call(
        functools.partial(_rwkv_kernel, units=G, chunks=RW_CHUNKS_PER_STEP),
        grid=(B, ngrp, nc),
        in_specs=[zspec(0), zspec(1), zspec(2),
                  pl.BlockSpec((R, 512), lambda b, g, c: (b * nc + c, lora_col)),
                  vspec, vspec, vspec,
                  pl.BlockSpec((1, 512), lambda b, g, c: (0, 0)),
                  vspec, vspec, vspec, vspec, vspec, vspec, vspec,
                  pl.BlockSpec((64, W), lambda b, g, c: (0, g)),
                  pl.BlockSpec((64, W), lambda b, g, c: (0, g)),
                  pl.BlockSpec((256, W), lambda b, g, c: (0, g))],
        out_specs=pl.BlockSpec((R, W), lambda b, g, c: (b * nc + c, g)),
        out_shape=jax.ShapeDtypeStruct((T, RW_WIDTH), BF16),
        scratch_shapes=[pltpu.VMEM((G, RW_UNIT, RW_UNIT), F32),
                        pltpu.VMEM((1, W), F32), pltpu.VMEM((1, W), F32), pltpu.VMEM((1, W), F32),
                        pltpu.VMEM((1, 512), F32)],
        compiler_params=_params("parallel", "parallel", "arbitrary"),
        name="rwkv",
    )(z_all, z_all, z_all, z_all, mu_r, mu_k, mu_v, mu_l, w0, a0, kk, ka, rk, lnw, lnb, w2, a2, g2)


def _mla_proj_kernel(cq_ref, ckv_ref, kpe_ref, cos_ref, sin_ref, qn_ref, kvn_ref,
                     wqn_ref, wqp_ref, wkv_ref, q_ref, kv_ref, kr_ref, *, scale):
    cqn = _bf(_rms(cq_ref[...].astype(F32), qn_ref[...]))
    ckvn = _bf(_rms(ckv_ref[...].astype(F32), kvn_ref[...]))
    cos = cos_ref[...]
    sin = sin_ref[...]
    q_nope = jnp.dot(cqn, wqn_ref[...], preferred_element_type=F32)
    q_pe = jnp.dot(cqn, wqp_ref[...], preferred_element_type=F32)
    half = MLA_ROPE // 2
    first_half = lax.broadcasted_iota(jnp.int32, (1, LANES), 1) < half
    for h in range(MLA_HEADS):
        hs = slice(h * LANES, (h + 1) * LANES)
        pe = q_pe[:, hs]
        swapped = jnp.where(first_half, pltpu.roll(pe, LANES - half, axis=1), pltpu.roll(pe, half, axis=1))
        q_ref[:, 2 * h * LANES:(2 * h + 1) * LANES] = _bf(q_nope[:, hs] * scale)
        q_ref[:, (2 * h + 1) * LANES:(2 * h + 2) * LANES] = _bf((pe * cos + swapped * sin) * scale)
    kv_ref[...] = _bf(jnp.dot(ckvn, wkv_ref[...], preferred_element_type=F32))
    blk = kpe_ref[...].astype(F32)
    kr_ref[...] = _bf(blk * cos + pltpu.roll(blk, MLA_ROPE, axis=1) * sin)


def _mla_proj(z_all, cos_t, sin_t, qn, kvn, wqn, wqp, wkv, tm=512):
    T = z_all.shape[0]
    HW = MLA_HEADS * 2 * LANES
    scale = float((MLA_NOPE + MLA_ROPE) ** -0.5 * 1.4426950408889634)
    full = lambda a: pl.BlockSpec(a.shape, lambda i: (0, 0))
    kpe_blk = (z_all.shape[1] - LANES) // LANES
    return pl.pallas_call(
        functools.partial(_mla_proj_kernel, scale=scale),
        grid=(T // tm,),
        in_specs=[pl.BlockSpec((tm, MLA_RANK), lambda i: (i, 6)),
                  pl.BlockSpec((tm, MLA_RANK), lambda i: (i, 7)),
                  pl.BlockSpec((tm, LANES), lambda i: (i, kpe_blk)),
                  pl.BlockSpec((tm, LANES), lambda i: (i, 0)),
                  pl.BlockSpec((tm, LANES), lambda i: (i, 0)),
                  full(qn), full(kvn), full(wqn), full(wqp), full(wkv)],
        out_specs=[pl.BlockSpec((tm, HW), lambda i: (i, 0)),
                   pl.BlockSpec((tm, HW), lambda i: (i, 0)),
                   pl.BlockSpec((tm, LANES), lambda i: (i, 0))],
        out_shape=[jax.ShapeDtypeStruct((T, HW), BF16),
                   jax.ShapeDtypeStruct((T, HW), BF16),
                   jax.ShapeDtypeStruct((T, LANES), BF16)],
        compiler_params=_params("parallel"),
        name="mla_proj",
    )(z_all, z_all, z_all, cos_t, sin_t, qn, kvn, wqn, wqp, wkv)


ATT_HEADS_PER_STEP = 8


def _attn_kernel(qi_ref, ki_ref, q_ref, kv_ref, kr_ref, o_ref, m_ref, l_ref, acc_ref, *, tq, tk, heads):
    t = pl.program_id(2)
    qi = qi_ref[t]
    ki = ki_ref[t]
    H = range(heads)

    @pl.when(ki == 0)
    def _():
        m_ref[...] = jnp.full_like(m_ref, -1e30)
        l_ref[...] = jnp.zeros_like(l_ref)
        acc_ref[...] = jnp.zeros_like(acc_ref)

    def update(masked):
        kr = kr_ref[...]
        s = [lax.dot_general(q_ref[:, 2 * h * LANES:(2 * h + 2) * LANES],
                             jnp.concatenate([kv_ref[:, 2 * h * LANES:(2 * h + 1) * LANES], kr], axis=1),
                             (((1,), (1,)), ((), ())), preferred_element_type=F32) for h in H]
        if masked:
            q_last = (qi * tq + lax.broadcasted_iota(jnp.int32, (tq, tk), 0)) | (CHUNK - 1)
            k_pos = ki * tk + lax.broadcasted_iota(jnp.int32, (tq, tk), 1)
            vis = k_pos <= q_last
            s = [jnp.where(vis, x, -1e30) for x in s]
        m_prev = [m_ref[h] for h in H]
        m_new = [jnp.maximum(m_prev[h], jnp.max(s[h], axis=-1, keepdims=True)) for h in H]
        p = [jnp.exp2(s[h] - jnp.concatenate([m_new[h]] * (tk // LANES), axis=1)) for h in H]
        alpha = [jnp.exp2(m_prev[h] - m_new[h]) for h in H]
        pv = [jnp.dot(_bf(p[h]), kv_ref[:, (2 * h + 1) * LANES:(2 * h + 2) * LANES], preferred_element_type=F32) for h in H]
        for h in H:
            psum = sum(p[h][:, c:c + LANES] for c in range(0, tk, LANES))
            l_ref[h] = alpha[h] * l_ref[h] + psum
            acc_ref[h] = alpha[h] * acc_ref[h] + pv[h]
            m_ref[h] = m_new[h]

    @pl.when(ki < qi)
    def _():
        update(False)

    @pl.when(ki == qi)
    def _():
        update(True)
        for h in H:
            o_ref[:, h * LANES:(h + 1) * LANES] = _bf(acc_ref[h] / jnp.sum(l_ref[h], axis=-1, keepdims=True))


def _attention(q, kv, kr, S, tq=512):
    T = q.shape[0]
    B = T // S
    nq = S // tq
    G = ATT_HEADS_PER_STEP
    pairs = [(a, b) for a in range(nq) for b in range(a + 1)]
    qi_tab = jnp.asarray([a for a, _ in pairs], jnp.int32)
    ki_tab = jnp.asarray([b for _, b in pairs], jnp.int32)
    grid_spec = pltpu.PrefetchScalarGridSpec(
        num_scalar_prefetch=2,
        grid=(B, MLA_HEADS // G, len(pairs)),
        in_specs=[pl.BlockSpec((tq, 2 * G * LANES), lambda b, g, t, qt, kt: (b * nq + qt[t], g)),
                  pl.BlockSpec((tq, 2 * G * LANES), lambda b, g, t, qt, kt: (b * nq + kt[t], g)),
                  pl.BlockSpec((tq, LANES), lambda b, g, t, qt, kt: (b * nq + kt[t], 0))],
        out_specs=pl.BlockSpec((tq, G * LANES), lambda b, g, t, qt, kt: (b * nq + qt[t], g)),
        scratch_shapes=[pltpu.VMEM((G, tq, LANES), F32), pltpu.VMEM((G, tq, LANES), F32), pltpu.VMEM((G, tq, LANES), F32)])
    return pl.pallas_call(
        functools.partial(_attn_kernel, tq=tq, tk=tq, heads=G),
        grid_spec=grid_spec,
        out_shape=jax.ShapeDtypeStruct((T, MLA_HEADS * MLA_V), BF16),
        compiler_params=_params("parallel", "parallel", "arbitrary"),
        name="attention",
    )(qi_tab, ki_tab, q, kv, kr)


def _mix_kernel(yrw_ref, ymla_ref, gate_ref, x_ref, wa_ref, wb_ref, wo_ref, g_ref, o_ref):
    D = x_ref.shape[1]
    t = (jax.nn.sigmoid(gate_ref[:, :D].astype(F32)) * jnp.dot(yrw_ref[...], wa_ref[...], preferred_element_type=F32)
         + jax.nn.sigmoid(gate_ref[:, D:].astype(F32)) * jnp.dot(ymla_ref[...], wb_ref[...], preferred_element_type=F32))
    mix = jnp.dot(_bf(t), wo_ref[...], preferred_element_type=F32)
    o_ref[...] = x_ref[...] + _rms(mix, g_ref[...])


def _resident(a):
    return pl.BlockSpec(a.shape, lambda *_: (0,) * a.ndim, pipeline_mode=pl.Buffered(1))


def _mix(y_rw, y_mla, z_all, x2, wa, wb, wo, g, tm=256):
    T, D = x2.shape
    return pl.pallas_call(
        _mix_kernel,
        grid=(T // tm,),
        in_specs=[pl.BlockSpec((tm, y_rw.shape[1]), lambda i: (i, 0)),
                  pl.BlockSpec((tm, y_mla.shape[1]), lambda i: (i, 0)),
                  pl.BlockSpec((tm, 2 * D), lambda i: (i, 1)),
                  pl.BlockSpec((tm, D), lambda i: (i, 0)),
                  _resident(wa), _resident(wb), _resident(wo), _resident(g)],
        out_specs=pl.BlockSpec((tm, D), lambda i: (i, 0)),
        out_shape=jax.ShapeDtypeStruct((T, D), F32),
        compiler_params=_params("parallel"),
        name="mix",
    )(y_rw, y_mla, z_all, x2, wa, wb, wo, g)


HALO = 16
FFN_SUB = 256
FFN_ROW_BLOCKS = (256, 256, 256, 256)


def _gelu_tanh(x):
    return 0.5 * x * (1.0 + jnp.tanh(0.7978845608028654 * (x + 0.044715 * x * x * x)))


def _ffn_kernel(h_ref, halo_ref, gin_ref, wug_ref, wuv_ref, cwg_ref, cwv_ref, cbg_ref, cbv_ref,
                wd_ref, gout_ref, o_ref, xn_ref, upg_ref, upv_ref, *, tm, tiles_per_seq):
    i = pl.program_id(0)
    j = pl.program_id(1)

    @pl.when(j == 0)
    def _():
        xn_ref[0:HALO, :] = _bf(_rms(halo_ref[...], gin_ref[...]))
        xn_ref[HALO:, :] = _bf(_rms(h_ref[...], gin_ref[...]))
        o_ref[...] = jnp.zeros_like(o_ref)

    keep = jnp.where((i % tiles_per_seq) == 0, 0.0, 1.0).astype(F32)
    tf = upg_ref.shape[1]
    subs = [slice(c, c + FFN_SUB) for c in range(0, tf, FFN_SUB)]

    for w_ref, up_ref in ((wug_ref, upg_ref), (wuv_ref, upv_ref)):
        for cs in subs:
            up = jnp.dot(xn_ref[...], w_ref[:, cs], preferred_element_type=F32)
            up_ref[0:HALO, cs] = up[0:HALO] * keep
            up_ref[HALO:, cs] = up[HALO:]

    def conv(up_ref, cw_ref, cb_ref, cs, r0, nr):
        cw = cw_ref[:, cs]
        return (cw[0:1, :] * up_ref[HALO - 2 + r0:HALO - 2 + r0 + nr, cs]
                + cw[1:2, :] * up_ref[HALO - 1 + r0:HALO - 1 + r0 + nr, cs]
                + cw[2:3, :] * up_ref[HALO + r0:HALO + r0 + nr, cs] + cb_ref[:, cs])

    def act(cs, r0, nr):
        return _bf(_gelu_tanh(conv(upg_ref, cwg_ref, cbg_ref, cs, r0, nr)) * conv(upv_ref, cwv_ref, cbv_ref, cs, r0, nr))

    head = [act(cs, 0, tm) for cs in subs[:-1]]
    r0 = 0
    for nr in FFN_ROW_BLOCKS:
        a = jnp.concatenate([x[r0:r0 + nr] for x in head] + [act(subs[-1], r0, nr)], axis=1)
        o_ref[r0:r0 + nr, :] += jnp.dot(a, wd_ref[...], preferred_element_type=F32)
        r0 += nr
    assert r0 == tm

    @pl.when(j == pl.num_programs(1) - 1)
    def _():
        o_ref[...] = h_ref[...] + _rms(o_ref[...], gout_ref[...])


def _ffn(h1, S, gin, w_up, conv_w, conv_b, w_down, gout, tm=1024, tf=512):
    T, D = h1.shape
    F = w_down.shape[0]
    nf = F // tf
    hb = tm // HALO
    return pl.pallas_call(
        functools.partial(_ffn_kernel, tm=tm, tiles_per_seq=S // tm),
        grid=(T // tm, nf),
        in_specs=[pl.BlockSpec((tm, D), lambda i, j: (i, 0), pipeline_mode=pl.Buffered(1)),
                  pl.BlockSpec((HALO, D), lambda i, j: (jnp.maximum(i * hb - 1, 0), 0)),
                  pl.BlockSpec((1, D), lambda i, j: (0, 0)),
                  pl.BlockSpec((D, tf), lambda i, j: (0, j)),
                  pl.BlockSpec((D, tf), lambda i, j: (0, nf + j)),
                  pl.BlockSpec((3, tf), lambda i, j: (0, j)),
                  pl.BlockSpec((3, tf), lambda i, j: (0, nf + j)),
                  pl.BlockSpec((1, tf), lambda i, j: (0, j)),
                  pl.BlockSpec((1, tf), lambda i, j: (0, nf + j)),
                  pl.BlockSpec((tf, D), lambda i, j: (j, 0)),
                  pl.BlockSpec((1, D), lambda i, j: (0, 0))],
        out_specs=pl.BlockSpec((tm, D), lambda i, j: (i, 0)),
        out_shape=jax.ShapeDtypeStruct((T, D), F32),
        scratch_shapes=[pltpu.VMEM((tm + HALO, D), BF16),
                        pltpu.VMEM((tm + HALO, tf), F32), pltpu.VMEM((tm + HALO, tf), F32)],
        compiler_params=_params("parallel", "arbitrary"),
        name="ffn",
    )(h1, h1, gin, w_up, w_up, conv_w, conv_w, conv_b, conv_b, w_down, gout)


def _ple_kernel(h_ref, p_ref, wpg_ref, wple_ref, g_ref, o_ref):
    h = h_ref[...]
    gate = jax.nn.sigmoid(jnp.dot(_bf(h), wpg_ref[...], preferred_element_type=F32))
    e = jnp.dot(_bf(p_ref[...]), wple_ref[...], preferred_element_type=F32)
    o_ref[...] = h + _rms(gate * e, g_ref[...])


def _ple(h2, p2, wpg, wple, g, tm=512):
    T, D = h2.shape
    full = lambda a: pl.BlockSpec(a.shape, lambda i: (0, 0))
    return pl.pallas_call(
        _ple_kernel,
        grid=(T // tm,),
        in_specs=[pl.BlockSpec((tm, D), lambda i: (i, 0)),
                  pl.BlockSpec((tm, p2.shape[1]), lambda i: (i, 0)),
                  full(wpg), full(wple), full(g)],
        out_specs=pl.BlockSpec((tm, D), lambda i: (i, 0)),
        out_shape=jax.ShapeDtypeStruct((T, D), F32),
        compiler_params=_params("parallel"),
        name="ple",
    )(h2, p2, wpg, wple, g)


def _swap_halves(w):
    half = w.shape[-1] // 2
    return jnp.concatenate([w[..., half:], w[..., :half]], axis=-1)


def _pack_w_in_kernel(off_ref, w_ref, kpe_ref, o_ref):
    j = pl.program_id(0)
    last = pl.num_programs(0) - 1

    @pl.when(j < last)
    def _():
        o_ref[...] = _bf(w_ref[...])

    @pl.when(j == last)
    def _():
        kpe = kpe_ref[...]
        half = MLA_ROPE // 2
        pad = jnp.zeros((W_IN_TILE - RW_LORA - 2 * MLA_ROPE, kpe.shape[1]), F32)
        o_ref[...] = _bf(jnp.concatenate([w_ref[0:RW_LORA, :], pad, kpe, kpe[half:], kpe[:half]], axis=0))


def _pack_w_in(w_in_t):
    N, D = w_in_t.shape
    rkv_end = 3 * RW_WIDTH
    m0 = rkv_end + RW_LORA
    g0 = m0 + 2 * MLA_RANK + MLA_ROPE
    starts = (list(range(0, rkv_end, W_IN_TILE)) + [m0, m0 + MLA_RANK]
              + list(range(g0, g0 + 4096, W_IN_TILE)) + [rkv_end])
    assert len(starts) * W_IN_TILE == W_IN_PACKED and N == g0 + 4096
    assert all(s % W_IN_ALIGN == 0 for s in starts)
    starts = [s // W_IN_ALIGN for s in starts]
    grid_spec = pltpu.PrefetchScalarGridSpec(
        num_scalar_prefetch=1,
        grid=(len(starts),),
        in_specs=[pl.BlockSpec((pl.Element(W_IN_TILE), pl.Element(D)),
                               lambda j, off: (pl.multiple_of(off[j] * W_IN_ALIGN, W_IN_ALIGN), 0)),
                  pl.BlockSpec((pl.Element(MLA_ROPE), pl.Element(D)), lambda j, off: (m0 + 2 * MLA_RANK, 0))],
        out_specs=pl.BlockSpec((W_IN_TILE, D), lambda j, off: (j, 0)))
    return pl.pallas_call(
        _pack_w_in_kernel,
        grid_spec=grid_spec,
        out_shape=jax.ShapeDtypeStruct((W_IN_PACKED, D), BF16),
        compiler_params=_params("arbitrary"),
        name="pack_w_in",
    )(jnp.asarray(starts, jnp.int32), w_in_t, w_in_t)


def _row(v):
    return v.reshape(1, -1).astype(F32)


def kernel(x, p, positions, pre_mix_norm, w_in, rw_mu, rw_w0, rw_w2, rw_a0, rw_a2, rw_g2, rw_k_k, rw_k_a, rw_r_k, rw_lnx_w, rw_lnx_b, mla_q_norm, mla_w_q_up, mla_kv_norm, mla_w_kv_up, w_branch_rw, w_branch_mla, w_out, post_mix_norm, pre_ffn_norm, w_up, conv_w, conv_b, w_down, post_ffn_norm, w_ple, w_ple_gate, ple_norm):
    B, S, D = x.shape
    T = B * S
    depth = w_in.shape[0]

    inv_freq = ROPE_BASE ** (-jnp.arange(0, MLA_ROPE, 2, dtype=F32) / MLA_ROPE)
    ang = positions.astype(F32)[..., None] * inv_freq
    cos = jnp.cos(ang).reshape(T, MLA_ROPE // 2)
    sin = jnp.sin(ang).reshape(T, MLA_ROPE // 2)
    zpad = jnp.zeros((T, LANES - MLA_ROPE), F32)
    cos_t = jnp.concatenate([cos, cos, zpad], axis=1)
    sin_t = jnp.concatenate([-sin, sin, zpad], axis=1)

    h = x.reshape(T, D)
    for i in range(depth):
        w_in_p = _pack_w_in(jnp.swapaxes(w_in, 1, 2)[i])
        z_all, (w_kv_b, w_rw_b, w_mla_b, w_out_b, w_up_b, w_down_b, w_pg_b, w_ple_b) = _in_proj(
            h, _row(pre_mix_norm[i]), w_in_p,
            [mla_w_kv_up[i], w_branch_rw[i], w_branch_mla[i], w_out[i], w_up[i], w_down[i], w_ple_gate[i], w_ple[i]])

        mu = rw_mu[i]
        mu_l = jnp.concatenate([mu[3 * RW_WIDTH:], jnp.zeros((512 - 288,), F32)]).reshape(1, 512)
        g2p = jnp.concatenate([rw_g2[i], jnp.zeros((256 - rw_g2.shape[1], RW_WIDTH), F32)], axis=0)
        w2p = jnp.concatenate([rw_w2[i]], axis=0)
        y_rw = _rwkv(z_all, S, _row(mu[:RW_WIDTH]), _row(mu[RW_WIDTH:2 * RW_WIDTH]), _row(mu[2 * RW_WIDTH:3 * RW_WIDTH]),
                     mu_l, _row(rw_w0[i]), _row(rw_a0[i]), _row(rw_k_k[i]), _row(rw_k_a[i]), _row(rw_r_k[i]),
                     _row(rw_lnx_w[i]), _row(rw_lnx_b[i]), _bf(w2p), _bf(rw_a2[i]), _bf(g2p))

        wq = mla_w_q_up[i].reshape(MLA_RANK, MLA_HEADS, MLA_NOPE + MLA_ROPE)
        wqn = wq[:, :, :MLA_NOPE].reshape(MLA_RANK, MLA_HEADS * MLA_NOPE)
        wq_pe = wq[:, :, MLA_NOPE:]
        padh = jnp.zeros((MLA_RANK, MLA_HEADS, LANES - MLA_ROPE), F32)
        wqp = jnp.concatenate([wq_pe, padh], axis=-1).reshape(MLA_RANK, MLA_HEADS * LANES)
        q, kv, kr = _mla_proj(z_all, cos_t, sin_t, _row(mla_q_norm[i]), _row(mla_kv_norm[i]),
                              _bf(wqn), _bf(wqp), w_kv_b)
        y_mla = _attention(q, kv, kr, S)

        h = _mix(y_rw, y_mla, z_all, h, w_rw_b, w_mla_b, w_out_b, _row(post_mix_norm[i]))
        h = _ffn(h, S, _row(pre_ffn_norm[i]), w_up_b, conv_w[i], _row(conv_b[i]), w_down_b, _row(post_ffn_norm[i]))
        h = _ple(h, p[i].reshape(T, -1), w_pg_b, w_ple_b, _row(ple_norm[i]))
    return h.reshape(B, S, D)
```

```python
import functools

import jax
import jax.numpy as jnp
from jax import lax
from jax.experimental import pallas as pl
from jax.experimental.pallas import tpu as pltpu

NORM_EPS = 1e-6
GN_EPS = 64e-5
CHUNK = 64
ROPE_BASE = 10000.0

RW_HEADS = 16
RW_N = 64
RW_WIDTH = RW_HEADS * RW_N
RW_L = 64
LANES = 128
RW_UNIT_HEADS = 2
RW_UNIT = RW_UNIT_HEADS * RW_N
RW_UNITS_PER_STEP = 8
RW_CHUNKS_PER_STEP = 4

MLA_HEADS = 8
MLA_NOPE = 128
MLA_ROPE = 64
MLA_V = 128
MLA_RANK = 512

RW_LORA = 288
W_IN_TILE = 512
W_IN_ALIGN = 32
W_IN_PACKED = 3 * RW_WIDTH + 2 * MLA_RANK + 4096 + W_IN_TILE

VMEM_LIMIT_BYTES = 56 * 1024 * 1024

BF16 = jnp.bfloat16
F32 = jnp.float32


def _params(*sem):
    return pltpu.CompilerParams(dimension_semantics=sem, vmem_limit_bytes=VMEM_LIMIT_BYTES)


def _bf(x):
    return x.astype(BF16)


def _mm(a, b):
    return jnp.dot(_bf(a), _bf(b), preferred_element_type=F32)


def _mm_nt(a, b):
    return lax.dot_general(_bf(a), _bf(b), (((1,), (1,)), ((), ())), preferred_element_type=F32)


def _rms(x, g):
    return x * lax.rsqrt(jnp.mean(x * x, axis=-1, keepdims=True) + NORM_EPS) * g


def _inproj_kernel(*refs, n_cast):
    x_ref, g_ref, w_ref = refs[:3]
    cast_in = refs[3:3 + n_cast]
    o_ref = refs[3 + n_cast]
    cast_out = refs[4 + n_cast:4 + 2 * n_cast]
    xn_ref = refs[-1]

    @pl.when(pl.program_id(1) == 0)
    def _():
        xn_ref[...] = _bf(_rms(x_ref[...], g_ref[...]))

    o_ref[...] = lax.dot_general(xn_ref[...], w_ref[...], (((1,), (1,)), ((), ())), preferred_element_type=F32)
    _cast_blocks(cast_in, cast_out)


def _cast_rows(n_rows, n_steps):
    rb = 16
    while n_rows % rb or n_rows // rb > n_steps:
        rb += 16
    return rb


def _cast_specs(cast, n_steps, step_of):
    specs = []
    for a in cast:
        rb = _cast_rows(a.shape[0], n_steps)
        last = a.shape[0] // rb - 1
        specs.append(pl.BlockSpec((rb, a.shape[1]), lambda *idx, last=last: (jnp.minimum(step_of(*idx), last), 0)))
    return specs


def _cast_blocks(cast_in, cast_out):
    for src, dst in zip(cast_in, cast_out):
        dst[...] = _bf(src[...])


def _in_proj(x2, g, w_t, cast, tm=2048, tn=W_IN_TILE):
    T, D = x2.shape
    N = w_t.shape[0]
    ni, nj = T // tm, N // tn
    cast_specs = _cast_specs(cast, ni * nj, lambda i, j: i * nj + j)
    outs = pl.pallas_call(
        functools.partial(_inproj_kernel, n_cast=len(cast)),
        grid=(ni, nj),
        in_specs=[
            pl.BlockSpec((tm, D), lambda i, j: (i, 0), pipeline_mode=pl.Buffered(1)),
            pl.BlockSpec((1, D), lambda i, j: (0, 0)),
            pl.BlockSpec((tn, D), lambda i, j: (j, 0)),
        ] + cast_specs,
        out_specs=[pl.BlockSpec((tm, tn), lambda i, j: (i, j))] + cast_specs,
        out_shape=[jax.ShapeDtypeStruct((T, N), F32)] + [jax.ShapeDtypeStruct(a.shape, BF16) for a in cast],
        scratch_shapes=[pltpu.VMEM((tm, D), BF16)],
        compiler_params=_params("arbitrary", "arbitrary"),
        name="in_proj",
    )(x2, g, w_t, *cast)
    return outs[0], outs[1:]


def _split3(x):
    hi = _bf(x)
    r1 = x - hi.astype(F32)
    mid = _bf(r1)
    lo = _bf(r1 - mid.astype(F32))
    return hi, mid, lo


def _rwkv_kernel(zr_ref, zk_ref, zv_ref, zl_ref, mur_ref, muk_ref, muv_ref, mul_ref,
                 w0_ref, a0_ref, kk_ref, ka_ref, rk_ref, lnw_ref, lnb_ref,
                 w2_ref, a2_ref, g2_ref, o_ref,
                 st_ref, pr_ref, pk_ref, pv_ref, plr_ref, *, units, chunks):
    L = RW_L
    UW = RW_UNIT
    c = pl.program_id(2)

    @pl.when(c == 0)
    def _():
        st_ref[...] = jnp.zeros_like(st_ref)
        pr_ref[...] = jnp.zeros_like(pr_ref)
        pk_ref[...] = jnp.zeros_like(pk_ref)
        pv_ref[...] = jnp.zeros_like(pv_ref)
        plr_ref[...] = jnp.zeros_like(plr_ref)

    def shift_lerp(z_ref, prev_ref, mu_ref):
        z = z_ref[...]
        rolled = pltpu.roll(z, 1, axis=0)
        row = lax.broadcasted_iota(jnp.int32, z.shape, 0)
        zs = jnp.where(row == 0, prev_ref[...], rolled)
        prev_ref[...] = z[z.shape[0] - 1:, :]
        return z + (zs - z) * mu_ref[...]

    r_all = shift_lerp(zr_ref, pr_ref, mur_ref)
    k_all = shift_lerp(zk_ref, pk_ref, muk_ref)
    v_all = shift_lerp(zv_ref, pv_ref, muv_ref)
    lo_all = shift_lerp(zl_ref, plr_ref, mul_ref)
    wd = jnp.tanh(lo_all[:, 0:64])
    ad = lo_all[:, 64:128]
    gd = jax.nn.sigmoid(lo_all[:, 128:384])

    hshift = RW_N.bit_length() - 1
    head_of_lane = lax.broadcasted_iota(jnp.int32, (1, UW), 1) >> hshift
    head_masks = [head_of_lane == j for j in range(RW_UNIT_HEADS)]
    row_l = lax.broadcasted_iota(jnp.int32, (L, UW), 0)
    col_l = lax.broadcasted_iota(jnp.int32, (L, UW), 1) & (RW_N - 1)
    strict = col_l < row_l
    incl = col_l <= row_l
    eye_pair = (col_l == row_l).astype(F32)
    row_s = lax.broadcasted_iota(jnp.int32, (UW, UW), 0)
    col_s = lax.broadcasted_iota(jnp.int32, (UW, UW), 1)
    same_head = (row_s >> hshift) == (col_s >> hshift)
    diag_s = row_s == col_s
    ones_bd = same_head.astype(BF16)

    def stack(x):
        return jnp.concatenate([jnp.where(m, x, 0.0) for m in head_masks], axis=0)

    def head_sum(x):
        return jnp.dot(_bf(x), ones_bd, preferred_element_type=F32)

    def cat(xs, axis):
        return jnp.concatenate(xs, axis=axis)

    U = range(units)
    uls = [slice(u * UW, (u + 1) * UW) for u in U]
    xw = [w0_ref[:, sl] + _mm(wd, w2_ref[:, sl]) for sl in uls]
    ga = [a0_ref[:, sl] + _mm(ad, a2_ref[:, sl]) for sl in uls]
    g_out = [_mm(gd, g2_ref[:, sl]) for sl in uls]
    kkv = [k_all[:, uls[u]] * kk_ref[:, uls[u]] for u in U]
    ss = [head_sum(x * x) for x in kkv]
    logw_u = [-0.6065306597126334 * jax.nn.sigmoid(x) for x in xw]
    gate_u = [jax.nn.sigmoid(x) for x in ga]
    kkn_u = [kkv[u] / jnp.maximum(jnp.sqrt(ss[u]), 1e-12) for u in U]
    k2_u = [k_all[:, uls[u]] * (1.0 + (gate_u[u] - 1.0) * ka_ref[:, uls[u]]) for u in U]
    bonus_in = [head_sum(r_all[:, uls[u]] * k2_u[u] * rk_ref[:, uls[u]]) for u in U]

    items = [(ci, u) for ci in range(chunks) for u in U]
    P = range(len(items))
    rows = [slice(ci * L, (ci + 1) * L) for ci, _ in items]
    r = [r_all[rows[p], uls[items[p][1]]] for p in P]
    v = [v_all[rows[p], uls[items[p][1]]] for p in P]
    logw = [logw_u[items[p][1]][rows[p]] for p in P]
    gate = [gate_u[items[p][1]][rows[p]] for p in P]
    kkn = [kkn_u[items[p][1]][rows[p]] for p in P]
    k2 = [k2_u[items[p][1]][rows[p]] for p in P]
    parts = [_split3(x) for x in logw]
    tri3 = ((lax.broadcasted_iota(jnp.int32, (L, 3 * L), 1) & (L - 1))
            <= lax.broadcasted_iota(jnp.int32, (L, 3 * L), 0)).astype(BF16)
    cum = [jnp.dot(tri3, cat(list(parts[p]), 0), preferred_element_type=F32) for p in P]
    cum_l = [c_[L - 1:L, :] for c_ in cum]
    e_neg = [jnp.exp(-c_) for c_ in cum]
    e_end = [jnp.exp(cum_l[p] - cum[p]) for p in P]
    kka = [kkn[p] * gate[p] for p in P]
    rt = [r[p] * jnp.exp(cum[p]) for p in P]
    kt = [k2[p] * e_neg[p] for p in P]
    bt = [kka[p] * e_neg[p] for p in P]
    at = [-kkn[p] * jnp.exp(cum[p] - logw[p]) for p in P]
    bk_t = [cat([kka[p] * e_end[p], k2[p] * e_end[p]], 0).T for p in P]
    w_end = [jnp.exp(c_) for c_ in cum_l]

    a_all = [_mm_nt(cat([at[p], rt[p]], 0), cat([stack(bt[p]), stack(kt[p])], 0)) for p in P]
    a_ab = [jnp.where(strict, a[:L, :UW], 0.0) for a in a_all]
    a_ak = [jnp.where(strict, a[:L, UW:], 0.0) for a in a_all]
    a_rb = [jnp.where(incl, a[L:, :UW], 0.0) for a in a_all]
    a_rk = [jnp.where(incl, a[L:, UW:], 0.0) for a in a_all]
    av = [_mm(a_ak[p], stack(v[p])) for p in P]

    tinv = [eye_pair + a for a in a_ab]
    pw = a_ab
    for _ in range(5):
        pw = [_mm(x, stack(x)) for x in pw]
        tinv = [tinv[p] + _mm(tinv[p], stack(pw[p])) for p in P]

    pq = [_mm(tinv[p], cat([stack(at[p]), stack(av[p])], 1)) for p in P]
    pm = [x[:, :UW] for x in pq]
    qm = [x[:, UW:] for x in pq]
    rm = [rt[p] + _mm(a_rb[p], stack(pm[p])) for p in P]
    y0 = [_mm(cat([a_rb[p], a_rk[p]], 1), cat([stack(qm[p]), stack(v[p])], 0)) for p in P]
    mn = [_mm(bk_t[p], cat([cat([pm[p], qm[p]], 1), cat([jnp.zeros_like(v[p]), v[p]], 1)], 0)) for p in P]
    m_mat = [jnp.where(diag_s, w_end[p], 0.0) + jnp.where(same_head, mn[p][:, :UW], 0.0) for p in P]
    n_mat = [jnp.where(same_head, mn[p][:, UW:], 0.0) for p in P]

    state = [st_ref[u] for u in U]
    y_chunks = []
    for ci in range(chunks):
        ps = [ci * units + u for u in U]
        ys = [_mm(cat([rm[p], m_mat[p]], 0), state[u]) for u, p in zip(U, ps)]
        state = [ys[u][L:] + n_mat[p] for u, p in zip(U, ps)]
        y_chunks.append([ys[u][:L] + y0[p] for u, p in zip(U, ps)])
    for u in U:
        st_ref[u] = state[u]

    y = [cat([y_chunks[ci][u] for ci in range(chunks)], 0) for u in U]
    mean = [head_sum(x) * (1.0 / RW_N) for x in y]
    yc = [y[u] - mean[u] for u in U]
    var = [head_sum(x * x) * (1.0 / RW_N) for x in yc]
    for u in U:
        yn = yc[u] * lax.rsqrt(var[u] + GN_EPS) * lnw_ref[:, uls[u]] + lnb_ref[:, uls[u]]
        o_ref[:, uls[u]] = _bf((yn + bonus_in[u] * v_all[:, uls[u]]) * g_out[u])


def _rwkv(z_all, S, mu_r, mu_k, mu_v, mu_l, w0, a0, kk, ka, rk, lnw, lnb, w2, a2, g2):
    T = z_all.shape[0]
    B = T // S
    R = RW_CHUNKS_PER_STEP * RW_L
    nc = S // R
    G = RW_UNITS_PER_STEP
    W = G * RW_UNIT
    ngrp = RW_WIDTH // W
    zspec = lambda off: pl.BlockSpec((R, W), lambda b, g, c, off=off: (b * nc + c, off * ngrp + g))
    vspec = pl.BlockSpec((1, W), lambda b, g, c: (0, g))
    lora_col = (z_all.shape[1] - 512) // 512
    return pl.pallas_call(
        functools.partial(_rwkv_kernel, units=G, chunks=RW_CHUNKS_PER_STEP),
        grid=(B, ngrp, nc),
        in_specs=[zspec(0), zspec(1), zspec(2),
                  pl.BlockSpec((R, 512), lambda b, g, c: (b * nc + c, lora_col)),
                  vspec, vspec, vspec,
                  pl.BlockSpec((1, 512), lambda b, g, c: (0, 0)),
                  vspec, vspec, vspec, vspec, vspec, vspec, vspec,
                  pl.BlockSpec((64, W), lambda b, g, c: (0, g)),
                  pl.BlockSpec((64, W), lambda b, g, c: (0, g)),
                  pl.BlockSpec((256, W), lambda b, g, c: (0, g))],
        out_specs=pl.BlockSpec((R, W), lambda b, g, c: (b * nc + c, g)),
        out_shape=jax.ShapeDtypeStruct((T, RW_WIDTH), BF16),
        scratch_shapes=[pltpu.VMEM((G, RW_UNIT, RW_UNIT), F32),
                        pltpu.VMEM((1, W), F32), pltpu.VMEM((1, W), F32), pltpu.VMEM((1, W), F32),
                        pltpu.VMEM((1, 512), F32)],
        compiler_params=_params("parallel", "parallel", "arbitrary"),
        name="rwkv",
    )(z_all, z_all, z_all, z_all, mu_r, mu_k, mu_v, mu_l, w0, a0, kk, ka, rk, lnw, lnb, w2, a2, g2)


def _mla_proj_kernel(cq_ref, ckv_ref, kpe_ref, cos_ref, sin_ref, qn_ref, kvn_ref,
                     wqn_ref, wqp_ref, wkv_ref, q_ref, kv_ref, kr_ref, *, scale):
    cqn = _bf(_rms(cq_ref[...], qn_ref[...]))
    ckvn = _bf(_rms(ckv_ref[...], kvn_ref[...]))
    cos = cos_ref[...]
    sin = sin_ref[...]
    q_nope = jnp.dot(cqn, wqn_ref[...], preferred_element_type=F32)
    q_pe = jnp.dot(cqn, wqp_ref[...], preferred_element_type=F32)
    half = MLA_ROPE // 2
    first_half = lax.broadcasted_iota(jnp.int32, (1, LANES), 1) < half
    for h in range(MLA_HEADS):
        hs = slice(h * LANES, (h + 1) * LANES)
        pe = q_pe[:, hs]
        swapped = jnp.where(first_half, pltpu.roll(pe, LANES - half, axis=1), pltpu.roll(pe, half, axis=1))
        q_ref[:, 2 * h * LANES:(2 * h + 1) * LANES] = _bf(q_nope[:, hs] * scale)
        q_ref[:, (2 * h + 1) * LANES:(2 * h + 2) * LANES] = _bf((pe * cos + swapped * sin) * scale)
    kv_ref[...] = _bf(jnp.dot(ckvn, wkv_ref[...], preferred_element_type=F32))
    blk = kpe_ref[...]
    kr_ref[...] = _bf(blk * cos + pltpu.roll(blk, MLA_ROPE, axis=1) * sin)


def _mla_proj(z_all, cos_t, sin_t, qn, kvn, wqn, wqp, wkv, tm=512):
    T = z_all.shape[0]
    HW = MLA_HEADS * 2 * LANES
    scale = float((MLA_NOPE + MLA_ROPE) ** -0.5 * 1.4426950408889634)
    full = lambda a: pl.BlockSpec(a.shape, lambda i: (0, 0))
    kpe_blk = (z_all.shape[1] - LANES) // LANES
    return pl.pallas_call(
        functools.partial(_mla_proj_kernel, scale=scale),
        grid=(T // tm,),
        in_specs=[pl.BlockSpec((tm, MLA_RANK), lambda i: (i, 6)),
                  pl.BlockSpec((tm, MLA_RANK), lambda i: (i, 7)),
                  pl.BlockSpec((tm, LANES), lambda i: (i, kpe_blk)),
                  pl.BlockSpec((tm, LANES), lambda i: (i, 0)),
                  pl.BlockSpec((tm, LANES), lambda i: (i, 0)),
                  full(qn), full(kvn), full(wqn), full(wqp), full(wkv)],
        out_specs=[pl.BlockSpec((tm, HW), lambda i: (i, 0)),
                   pl.BlockSpec((tm, HW), lambda i: (i, 0)),
                   pl.BlockSpec((tm, LANES), lambda i: (i, 0))],
        out_shape=[jax.ShapeDtypeStruct((T, HW), BF16),
                   jax.ShapeDtypeStruct((T, HW), BF16),
                   jax.ShapeDtypeStruct((T, LANES), BF16)],
        compiler_params=_params("parallel"),
        name="mla_proj",
    )(z_all, z_all, z_all, cos_t, sin_t, qn, kvn, wqn, wqp, wkv)


ATT_HEADS_PER_STEP = 8


def _attn_kernel(qi_ref, ki_ref, q_ref, kv_ref, kr_ref, o_ref, m_ref, l_ref, acc_ref, *, tq, tk, heads):
    t = pl.program_id(2)
    qi = qi_ref[t]
    ki = ki_ref[t]
    H = range(heads)

    @pl.when(ki == 0)
    def _():
        m_ref[...] = jnp.full_like(m_ref, -1e30)
        l_ref[...] = jnp.zeros_like(l_ref)
        acc_ref[...] = jnp.zeros_like(acc_ref)

    def update(r0, nr, nk, masked):
        rows = slice(r0, r0 + nr)
        kr = kr_ref[0:nk, :]
        s = [lax.dot_general(q_ref[rows, 2 * h * LANES:(2 * h + 2) * LANES],
                             jnp.concatenate([kv_ref[0:nk, 2 * h * LANES:(2 * h + 1) * LANES], kr], axis=1),
                             (((1,), (1,)), ((), ())), preferred_element_type=F32) for h in H]
        if masked:
            q_last = (qi * tq + r0 + lax.broadcasted_iota(jnp.int32, (nr, nk), 0)) | (CHUNK - 1)
            k_pos = ki * tk + lax.broadcasted_iota(jnp.int32, (nr, nk), 1)
            vis = k_pos <= q_last
            s = [jnp.where(vis, x, -1e30) for x in s]
        m_prev = [m_ref[h, rows] for h in H]
        m_new = [jnp.maximum(m_prev[h], jnp.max(s[h], axis=-1, keepdims=True)) for h in H]
        p = [jnp.exp2(s[h] - jnp.concatenate([m_new[h]] * (nk // LANES), axis=1)) for h in H]
        alpha = [jnp.exp2(m_prev[h] - m_new[h]) for h in H]
        pv = [jnp.dot(_bf(p[h]), kv_ref[0:nk, (2 * h + 1) * LANES:(2 * h + 2) * LANES], preferred_element_type=F32) for h in H]
        for h in H:
            psum = sum(p[h][:, c:c + LANES] for c in range(0, nk, LANES))
            l_ref[h, rows] = alpha[h] * l_ref[h, rows] + psum
            acc_ref[h, rows] = alpha[h] * acc_ref[h, rows] + pv[h]
            m_ref[h, rows] = m_new[h]

    @pl.when(ki < qi)
    def _():
        update(0, tq, tk, False)

    @pl.when(ki == qi)
    def _():
        update(0, tq // 2, tk // 2, True)
        update(tq // 2, tq // 2, tk, True)
        for h in H:
            o_ref[:, h * LANES:(h + 1) * LANES] = _bf(acc_ref[h] / jnp.sum(l_ref[h], axis=-1, keepdims=True))


def _attention(q, kv, kr, S, tq=512):
    T = q.shape[0]
    B = T // S
    nq = S // tq
    G = ATT_HEADS_PER_STEP
    pairs = [(a, b) for a in range(nq) for b in range(a + 1)]
    qi_tab = jnp.asarray([a for a, _ in pairs], jnp.int32)
    ki_tab = jnp.asarray([b for _, b in pairs], jnp.int32)
    grid_spec = pltpu.PrefetchScalarGridSpec(
        num_scalar_prefetch=2,
        grid=(B, MLA_HEADS // G, len(pairs)),
        in_specs=[pl.BlockSpec((tq, 2 * G * LANES), lambda b, g, t, qt, kt: (b * nq + qt[t], g)),
                  pl.BlockSpec((tq, 2 * G * LANES), lambda b, g, t, qt, kt: (b * nq + kt[t], g)),
                  pl.BlockSpec((tq, LANES), lambda b, g, t, qt, kt: (b * nq + kt[t], 0))],
        out_specs=pl.BlockSpec((tq, G * LANES), lambda b, g, t, qt, kt: (b * nq + qt[t], g)),
        scratch_shapes=[pltpu.VMEM((G, tq, LANES), F32), pltpu.VMEM((G, tq, LANES), F32), pltpu.VMEM((G, tq, LANES), F32)])
    return pl.pallas_call(
        functools.partial(_attn_kernel, tq=tq, tk=tq, heads=G),
        grid_spec=grid_spec,
        out_shape=jax.ShapeDtypeStruct((T, MLA_HEADS * MLA_V), BF16),
        compiler_params=_params("parallel", "parallel", "arbitrary"),
        name="attention",
    )(qi_tab, ki_tab, q, kv, kr)


def _mix_kernel(yrw_ref, ymla_ref, gate_ref, x_ref, wa_ref, wb_ref, wo_ref, g_ref, o_ref):
    D = x_ref.shape[1]
    t = (jax.nn.sigmoid(gate_ref[:, :D]) * jnp.dot(yrw_ref[...], wa_ref[...], preferred_element_type=F32)
         + jax.nn.sigmoid(gate_ref[:, D:]) * jnp.dot(ymla_ref[...], wb_ref[...], preferred_element_type=F32))
    mix = jnp.dot(_bf(t), wo_ref[...], preferred_element_type=F32)
    o_ref[...] = x_ref[...] + _rms(mix, g_ref[...])


def _resident(a):
    return pl.BlockSpec(a.shape, lambda *_: (0,) * a.ndim, pipeline_mode=pl.Buffered(1))


def _mix(y_rw, y_mla, z_all, x2, wa, wb, wo, g, tm=256):
    T, D = x2.shape
    return pl.pallas_call(
        _mix_kernel,
        grid=(T // tm,),
        in_specs=[pl.BlockSpec((tm, y_rw.shape[1]), lambda i: (i, 0)),
                  pl.BlockSpec((tm, y_mla.shape[1]), lambda i: (i, 0)),
                  pl.BlockSpec((tm, 2 * D), lambda i: (i, 1)),
                  pl.BlockSpec((tm, D), lambda i: (i, 0)),
                  _resident(wa), _resident(wb), _resident(wo), _resident(g)],
        out_specs=pl.BlockSpec((tm, D), lambda i: (i, 0)),
        out_shape=jax.ShapeDtypeStruct((T, D), F32),
        compiler_params=_params("parallel"),
        name="mix",
    )(y_rw, y_mla, z_all, x2, wa, wb, wo, g)


HALO = 16
FFN_SUB = 256
FFN_ROW_BLOCKS = (256, 256, 256, 256)


def _gelu_tanh(x):
    return 0.5 * x * (1.0 + jnp.tanh(0.7978845608028654 * (x + 0.044715 * x * x * x)))


def _ffn_kernel(h_ref, halo_ref, gin_ref, wug_ref, wuv_ref, cwg_ref, cwv_ref, cbg_ref, cbv_ref,
                wd_ref, gout_ref, o_ref, xn_ref, upg_ref, upv_ref, *, tm, tiles_per_seq):
    i = pl.program_id(0)
    j = pl.program_id(1)

    @pl.when(j == 0)
    def _():
        xn_ref[0:HALO, :] = _bf(_rms(halo_ref[...], gin_ref[...]))
        xn_ref[HALO:, :] = _bf(_rms(h_ref[...], gin_ref[...]))
        o_ref[...] = jnp.zeros_like(o_ref)

    keep = jnp.where((i % tiles_per_seq) == 0, 0.0, 1.0).astype(F32)
    tf = upg_ref.shape[1]
    subs = [slice(c, c + FFN_SUB) for c in range(0, tf, FFN_SUB)]

    for w_ref, up_ref in ((wug_ref, upg_ref), (wuv_ref, upv_ref)):
        for cs in subs:
            up = jnp.dot(xn_ref[...], w_ref[:, cs], preferred_element_type=F32)
            up_ref[0:HALO, cs] = up[0:HALO] * keep
            up_ref[HALO:, cs] = up[HALO:]

    def conv(up_ref, cw_ref, cb_ref, cs, r0, nr):
        cw = cw_ref[:, cs]
        return (cw[0:1, :] * up_ref[HALO - 2 + r0:HALO - 2 + r0 + nr, cs]
                + cw[1:2, :] * up_ref[HALO - 1 + r0:HALO - 1 + r0 + nr, cs]
                + cw[2:3, :] * up_ref[HALO + r0:HALO + r0 + nr, cs] + cb_ref[:, cs])

    def act(cs, r0, nr):
        return _bf(_gelu_tanh(conv(upg_ref, cwg_ref, cbg_ref, cs, r0, nr)) * conv(upv_ref, cwv_ref, cbv_ref, cs, r0, nr))

    head = [act(cs, 0, tm) for cs in subs[:-1]]
    r0 = 0
    for nr in FFN_ROW_BLOCKS:
        a = jnp.concatenate([x[r0:r0 + nr] for x in head] + [act(subs[-1], r0, nr)], axis=1)
        o_ref[r0:r0 + nr, :] += jnp.dot(a, wd_ref[...], preferred_element_type=F32)
        r0 += nr
    assert r0 == tm

    @pl.when(j == pl.num_programs(1) - 1)
    def _():
        o_ref[...] = h_ref[...] + _rms(o_ref[...], gout_ref[...])


def _ffn(h1, S, gin, w_up, conv_w, conv_b, w_down, gout, tm=1024, tf=512):
    T, D = h1.shape
    F = w_down.shape[0]
    nf = F // tf
    hb = tm // HALO
    return pl.pallas_call(
        functools.partial(_ffn_kernel, tm=tm, tiles_per_seq=S // tm),
        grid=(T // tm, nf),
        in_specs=[pl.BlockSpec((tm, D), lambda i, j: (i, 0), pipeline_mode=pl.Buffered(1)),
                  pl.BlockSpec((HALO, D), lambda i, j: (jnp.maximum(i * hb - 1, 0), 0)),
                  pl.BlockSpec((1, D), lambda i, j: (0, 0)),
                  pl.BlockSpec((D, tf), lambda i, j: (0, j)),
                  pl.BlockSpec((D, tf), lambda i, j: (0, nf + j)),
                  pl.BlockSpec((3, tf), lambda i, j: (0, j)),
                  pl.BlockSpec((3, tf), lambda i, j: (0, nf + j)),
                  pl.BlockSpec((1, tf), lambda i, j: (0, j)),
                  pl.BlockSpec((1, tf), lambda i, j: (0, nf + j)),
                  pl.BlockSpec((tf, D), lambda i, j: (j, 0)),
                  pl.BlockSpec((1, D), lambda i, j: (0, 0))],
        out_specs=pl.BlockSpec((tm, D), lambda i, j: (i, 0)),
        out_shape=jax.ShapeDtypeStruct((T, D), F32),
        scratch_shapes=[pltpu.VMEM((tm + HALO, D), BF16),
                        pltpu.VMEM((tm + HALO, tf), F32), pltpu.VMEM((tm + HALO, tf), F32)],
        compiler_params=_params("parallel", "arbitrary"),
        name="ffn",
    )(h1, h1, gin, w_up, w_up, conv_w, conv_w, conv_b, conv_b, w_down, gout)


def _ple_kernel(h_ref, p_ref, wpg_ref, wple_ref, g_ref, o_ref):
    h = h_ref[...]
    gate = jax.nn.sigmoid(jnp.dot(_bf(h), wpg_ref[...], preferred_element_type=F32))
    e = jnp.dot(_bf(p_ref[...]), wple_ref[...], preferred_element_type=F32)
    o_ref[...] = h + _rms(gate * e, g_ref[...])


def _ple(h2, p2, wpg, wple, g, tm=512):
    T, D = h2.shape
    full = lambda a: pl.BlockSpec(a.shape, lambda i: (0, 0))
    return pl.pallas_call(
        _ple_kernel,
        grid=(T // tm,),
        in_specs=[pl.BlockSpec((tm, D), lambda i: (i, 0)),
                  pl.BlockSpec((tm, p2.shape[1]), lambda i: (i, 0)),
                  full(wpg), full(wple), full(g)],
        out_specs=pl.BlockSpec((tm, D), lambda i: (i, 0)),
        out_shape=jax.ShapeDtypeStruct((T, D), F32),
        compiler_params=_params("parallel"),
        name="ple",
    )(h2, p2, wpg, wple, g)


def _swap_halves(w):
    half = w.shape[-1] // 2
    return jnp.concatenate([w[..., half:], w[..., :half]], axis=-1)


def _pack_w_in_kernel(off_ref, w_ref, kpe_ref, o_ref):
    j = pl.program_id(0)
    last = pl.num_programs(0) - 1

    @pl.when(j < last)
    def _():
        o_ref[...] = _bf(w_ref[...])

    @pl.when(j == last)
    def _():
        kpe = kpe_ref[...]
        half = MLA_ROPE // 2
        pad = jnp.zeros((W_IN_TILE - RW_LORA - 2 * MLA_ROPE, kpe.shape[1]), F32)
        o_ref[...] = _bf(jnp.concatenate([w_ref[0:RW_LORA, :], pad, kpe, kpe[half:], kpe[:half]], axis=0))


def _pack_w_in(w_in_t):
    N, D = w_in_t.shape
    rkv_end = 3 * RW_WIDTH
    m0 = rkv_end + RW_LORA
    g0 = m0 + 2 * MLA_RANK + MLA_ROPE
    starts = (list(range(0, rkv_end, W_IN_TILE)) + [m0, m0 + MLA_RANK]
              + list(range(g0, g0 + 4096, W_IN_TILE)) + [rkv_end])
    assert len(starts) * W_IN_TILE == W_IN_PACKED and N == g0 + 4096
    assert all(s % W_IN_ALIGN == 0 for s in starts)
    starts = [s // W_IN_ALIGN for s in starts]
    grid_spec = pltpu.PrefetchScalarGridSpec(
        num_scalar_prefetch=1,
        grid=(len(starts),),
        in_specs=[pl.BlockSpec((pl.Element(W_IN_TILE), pl.Element(D)),
                               lambda j, off: (pl.multiple_of(off[j] * W_IN_ALIGN, W_IN_ALIGN), 0)),
                  pl.BlockSpec((pl.Element(MLA_ROPE), pl.Element(D)), lambda j, off: (m0 + 2 * MLA_RANK, 0))],
        out_specs=pl.BlockSpec((W_IN_TILE, D), lambda j, off: (j, 0)))
    return pl.pallas_call(
        _pack_w_in_kernel,
        grid_spec=grid_spec,
        out_shape=jax.ShapeDtypeStruct((W_IN_PACKED, D), BF16),
        compiler_params=_params("arbitrary"),
        name="pack_w_in",
    )(jnp.asarray(starts, jnp.int32), w_in_t, w_in_t)


def _row(v):
    return v.reshape(1, -1).astype(F32)


def kernel(x, p, positions, pre_mix_norm, w_in, rw_mu, rw_w0, rw_w2, rw_a0, rw_a2, rw_g2, rw_k_k, rw_k_a, rw_r_k, rw_lnx_w, rw_lnx_b, mla_q_norm, mla_w_q_up, mla_kv_norm, mla_w_kv_up, w_branch_rw, w_branch_mla, w_out, post_mix_norm, pre_ffn_norm, w_up, conv_w, conv_b, w_down, post_ffn_norm, w_ple, w_ple_gate, ple_norm):
    B, S, D = x.shape
    T = B * S
    depth = w_in.shape[0]

    inv_freq = ROPE_BASE ** (-jnp.arange(0, MLA_ROPE, 2, dtype=F32) / MLA_ROPE)
    ang = positions.astype(F32)[..., None] * inv_freq
    cos = jnp.cos(ang).reshape(T, MLA_ROPE // 2)
    sin = jnp.sin(ang).reshape(T, MLA_ROPE // 2)
    zpad = jnp.zeros((T, LANES - MLA_ROPE), F32)
    cos_t = jnp.concatenate([cos, cos, zpad], axis=1)
    sin_t = jnp.concatenate([-sin, sin, zpad], axis=1)

    h = x.reshape(T, D)
    for i in range(depth):
        w_in_p = _pack_w_in(jnp.swapaxes(w_in, 1, 2)[i])
        z_all, (w_kv_b, w_rw_b, w_mla_b, w_out_b, w_up_b, w_down_b, w_pg_b, w_ple_b) = _in_proj(
            h, _row(pre_mix_norm[i]), w_in_p,
            [mla_w_kv_up[i], w_branch_rw[i], w_branch_mla[i], w_out[i], w_up[i], w_down[i], w_ple_gate[i], w_ple[i]])

        mu = rw_mu[i]
        mu_l = jnp.concatenate([mu[3 * RW_WIDTH:], jnp.zeros((512 - 288,), F32)]).reshape(1, 512)
        g2p = jnp.concatenate([rw_g2[i], jnp.zeros((256 - rw_g2.shape[1], RW_WIDTH), F32)], axis=0)
        w2p = jnp.concatenate([rw_w2[i]], axis=0)
        y_rw = _rwkv(z_all, S, _row(mu[:RW_WIDTH]), _row(mu[RW_WIDTH:2 * RW_WIDTH]), _row(mu[2 * RW_WIDTH:3 * RW_WIDTH]),
                     mu_l, _row(rw_w0[i]), _row(rw_a0[i]), _row(rw_k_k[i]), _row(rw_k_a[i]), _row(rw_r_k[i]),
                     _row(rw_lnx_w[i]), _row(rw_lnx_b[i]), _bf(w2p), _bf(rw_a2[i]), _bf(g2p))

        wq = mla_w_q_up[i].reshape(MLA_RANK, MLA_HEADS, MLA_NOPE + MLA_ROPE)
        wqn = wq[:, :, :MLA_NOPE].reshape(MLA_RANK, MLA_HEADS * MLA_NOPE)
        wq_pe = wq[:, :, MLA_NOPE:]
        padh = jnp.zeros((MLA_RANK, MLA_HEADS, LANES - MLA_ROPE), F32)
        wqp = jnp.concatenate([wq_pe, padh], axis=-1).reshape(MLA_RANK, MLA_HEADS * LANES)
        q, kv, kr = _mla_proj(z_all, cos_t, sin_t, _row(mla_q_norm[i]), _row(mla_kv_norm[i]),
                              _bf(wqn), _bf(wqp), w_kv_b)
        y_mla = _attention(q, kv, kr, S)

        h = _mix(y_rw, y_mla, z_all, h, w_rw_b, w_mla_b, w_out_b, _row(post_mix_norm[i]))
        h = _ffn(h, S, _row(pre_ffn_norm[i]), w_up_b, conv_w[i], _row(conv_b[i]), w_down_b, _row(post_ffn_norm[i]))
        h = _ple(h, p[i].reshape(T, -1), w_pg_b, w_ple_b, _row(ple_norm[i]))
    return h.reshape(B, S, D)
```

```python
import functools

import jax
import jax.numpy as jnp
from jax import lax
from jax.experimental import pallas as pl
from jax.experimental.pallas import tpu as pltpu

NORM_EPS = 1e-6
GN_EPS = 64e-5
CHUNK = 64
ROPE_BASE = 10000.0

RW_HEADS = 16
RW_N = 64
RW_WIDTH = RW_HEADS * RW_N
RW_L = 64
LANES = 128
RW_UNIT_HEADS = 2
RW_UNIT = RW_UNIT_HEADS * RW_N
RW_UNITS_PER_STEP = 8
RW_CHUNKS_PER_STEP = 4

MLA_HEADS = 8
MLA_NOPE = 128
MLA_ROPE = 64
MLA_V = 128
MLA_RANK = 512

RW_LORA = 288
W_IN_TILE = 512
W_IN_ALIGN = 32
W_IN_PACKED = 3 * RW_WIDTH + 2 * MLA_RANK + 4096 + W_IN_TILE

VMEM_LIMIT_BYTES = 56 * 1024 * 1024

BF16 = jnp.bfloat16
F32 = jnp.float32


def _params(*sem):
    return pltpu.CompilerParams(dimension_semantics=sem, vmem_limit_bytes=VMEM_LIMIT_BYTES)


def _bf(x):
    return x.astype(BF16)


def _mm(a, b):
    return jnp.dot(_bf(a), _bf(b), preferred_element_type=F32)


def _mm_nt(a, b):
    return lax.dot_general(_bf(a), _bf(b), (((1,), (1,)), ((), ())), preferred_element_type=F32)


def _rms(x, g):
    return x * lax.rsqrt(jnp.mean(x * x, axis=-1, keepdims=True) + NORM_EPS) * g


def _inproj_kernel(*refs, n_cast, tm):
    x_hbm, g_ref, w_ref = refs[:3]
    cast_in = refs[3:3 + n_cast]
    o_ref = refs[3 + n_cast]
    cast_out = refs[4 + n_cast:4 + 2 * n_cast]
    xbuf_ref, xn_ref, sem = refs[-3:]
    i, j = pl.program_id(0), pl.program_id(1)
    ni, nj = pl.num_programs(0), pl.num_programs(1)
    slot = i % 2

    def x_copy(tile):
        return pltpu.make_async_copy(x_hbm.at[pl.ds(tile * tm, tm), :], xbuf_ref, sem)

    def normalise(dst_slot):
        xn_ref[dst_slot] = _bf(_rms(xbuf_ref[...], g_ref[...]))

    def project():
        o_ref[...] = lax.dot_general(xn_ref[slot], w_ref[...], (((1,), (1,)), ((), ())), preferred_element_type=F32)
        _cast_blocks(cast_in, cast_out)

    @pl.when((i == 0) & (j == 0))
    def _():
        first = x_copy(0)
        first.start()
        first.wait()
        normalise(0)

    has_next = i + 1 < ni

    @pl.when((j == 0) & has_next)
    def _():
        x_copy(i + 1).start()

    @pl.when(j < nj - 1)
    def _():
        project()

    @pl.when((j == nj - 1) & has_next)
    def _():
        x_copy(i + 1).wait()
        normalise(1 - slot)
        project()

    @pl.when((j == nj - 1) & jnp.logical_not(has_next))
    def _():
        project()


def _cast_rows(n_rows, n_steps):
    rb = 16
    while n_rows % rb or n_rows // rb > n_steps:
        rb += 16
    return rb


def _cast_specs(cast, n_steps, step_of):
    specs = []
    for a in cast:
        rb = _cast_rows(a.shape[0], n_steps)
        last = a.shape[0] // rb - 1
        specs.append(pl.BlockSpec((rb, a.shape[1]), lambda *idx, last=last: (jnp.minimum(step_of(*idx), last), 0)))
    return specs


def _cast_blocks(cast_in, cast_out):
    for src, dst in zip(cast_in, cast_out):
        dst[...] = _bf(src[...])


def _in_proj(x2, g, w_t, cast, tm=2048, tn=W_IN_TILE):
    T, D = x2.shape
    N = w_t.shape[0]
    ni, nj = T // tm, N // tn
    cast_specs = _cast_specs(cast, ni * nj, lambda i, j: i * nj + j)
    outs = pl.pallas_call(
        functools.partial(_inproj_kernel, n_cast=len(cast), tm=tm),
        grid=(ni, nj),
        in_specs=[
            pl.BlockSpec(memory_space=pl.ANY),
            pl.BlockSpec((1, D), lambda i, j: (0, 0)),
            pl.BlockSpec((tn, D), lambda i, j: (j, 0)),
        ] + cast_specs,
        out_specs=[pl.BlockSpec((tm, tn), lambda i, j: (i, j))] + cast_specs,
        out_shape=[jax.ShapeDtypeStruct((T, N), F32)] + [jax.ShapeDtypeStruct(a.shape, BF16) for a in cast],
        scratch_shapes=[pltpu.VMEM((tm, D), F32), pltpu.VMEM((2, tm, D), BF16), pltpu.SemaphoreType.DMA(())],
        compiler_params=_params("arbitrary", "arbitrary"),
        name="in_proj",
    )(x2, g, w_t, *cast)
    return outs[0], outs[1:]


def _split3(x):
    hi = _bf(x)
    r1 = x - hi.astype(F32)
    mid = _bf(r1)
    lo = _bf(r1 - mid.astype(F32))
    return hi, mid, lo


def _rwkv_kernel(zr_ref, zk_ref, zv_ref, zl_ref, mur_ref, muk_ref, muv_ref, mul_ref,
                 w0_ref, a0_ref, kk_ref, ka_ref, rk_ref, lnw_ref, lnb_ref,
                 w2_ref, a2_ref, g2_ref, o_ref,
                 st_ref, pr_ref, pk_ref, pv_ref, plr_ref, *, units, chunks):
    L = RW_L
    UW = RW_UNIT
    c = pl.program_id(2)

    @pl.when(c == 0)
    def _():
        st_ref[...] = jnp.zeros_like(st_ref)
        pr_ref[...] = jnp.zeros_like(pr_ref)
        pk_ref[...] = jnp.zeros_like(pk_ref)
        pv_ref[...] = jnp.zeros_like(pv_ref)
        plr_ref[...] = jnp.zeros_like(plr_ref)

    def shift_lerp(z_ref, prev_ref, mu_ref):
        z = z_ref[...]
        rolled = pltpu.roll(z, 1, axis=0)
        row = lax.broadcasted_iota(jnp.int32, z.shape, 0)
        zs = jnp.where(row == 0, prev_ref[...], rolled)
        prev_ref[...] = z[z.shape[0] - 1:, :]
        return z + (zs - z) * mu_ref[...]

    r_all = shift_lerp(zr_ref, pr_ref, mur_ref)
    k_all = shift_lerp(zk_ref, pk_ref, muk_ref)
    v_all = shift_lerp(zv_ref, pv_ref, muv_ref)
    lo_all = shift_lerp(zl_ref, plr_ref, mul_ref)
    wd = jnp.tanh(lo_all[:, 0:64])
    ad = lo_all[:, 64:128]
    gd = jax.nn.sigmoid(lo_all[:, 128:384])

    hshift = RW_N.bit_length() - 1
    head_of_lane = lax.broadcasted_iota(jnp.int32, (1, UW), 1) >> hshift
    head_masks = [head_of_lane == j for j in range(RW_UNIT_HEADS)]
    row_l = lax.broadcasted_iota(jnp.int32, (L, UW), 0)
    col_l = lax.broadcasted_iota(jnp.int32, (L, UW), 1) & (RW_N - 1)
    strict = col_l < row_l
    incl = col_l <= row_l
    eye_pair = (col_l == row_l).astype(F32)
    row_s = lax.broadcasted_iota(jnp.int32, (UW, UW), 0)
    col_s = lax.broadcasted_iota(jnp.int32, (UW, UW), 1)
    same_head = (row_s >> hshift) == (col_s >> hshift)
    diag_s = row_s == col_s
    ones_bd = same_head.astype(BF16)

    def stack(x):
        return jnp.concatenate([jnp.where(m, x, 0.0) for m in head_masks], axis=0)

    def head_sum(x):
        return jnp.dot(_bf(x), ones_bd, preferred_element_type=F32)

    def cat(xs, axis):
        return jnp.concatenate(xs, axis=axis)

    U = range(units)
    uls = [slice(u * UW, (u + 1) * UW) for u in U]
    xw = [w0_ref[:, sl] + _mm(wd, w2_ref[:, sl]) for sl in uls]
    ga = [a0_ref[:, sl] + _mm(ad, a2_ref[:, sl]) for sl in uls]
    g_out = [_mm(gd, g2_ref[:, sl]) for sl in uls]
    kkv = [k_all[:, uls[u]] * kk_ref[:, uls[u]] for u in U]
    ss = [head_sum(x * x) for x in kkv]
    logw_u = [-0.6065306597126334 * jax.nn.sigmoid(x) for x in xw]
    gate_u = [jax.nn.sigmoid(x) for x in ga]
    kkn_u = [kkv[u] / jnp.maximum(jnp.sqrt(ss[u]), 1e-12) for u in U]
    k2_u = [k_all[:, uls[u]] * (1.0 + (gate_u[u] - 1.0) * ka_ref[:, uls[u]]) for u in U]
    bonus_in = [head_sum(r_all[:, uls[u]] * k2_u[u] * rk_ref[:, uls[u]]) for u in U]

    items = [(ci, u) for ci in range(chunks) for u in U]
    P = range(len(items))
    rows = [slice(ci * L, (ci + 1) * L) for ci, _ in items]
    r = [r_all[rows[p], uls[items[p][1]]] for p in P]
    v = [v_all[rows[p], uls[items[p][1]]] for p in P]
    logw = [logw_u[items[p][1]][rows[p]] for p in P]
    gate = [gate_u[items[p][1]][rows[p]] for p in P]
    kkn = [kkn_u[items[p][1]][rows[p]] for p in P]
    k2 = [k2_u[items[p][1]][rows[p]] for p in P]
    parts = [_split3(x) for x in logw]
    tri3 = ((lax.broadcasted_iota(jnp.int32, (L, 3 * L), 1) & (L - 1))
            <= lax.broadcasted_iota(jnp.int32, (L, 3 * L), 0)).astype(BF16)
    cum = [jnp.dot(tri3, cat(list(parts[p]), 0), preferred_element_type=F32) for p in P]
    cum_l = [c_[L - 1:L, :] for c_ in cum]
    e_neg = [jnp.exp(-c_) for c_ in cum]
    e_end = [jnp.exp(cum_l[p] - cum[p]) for p in P]
    kka = [kkn[p] * gate[p] for p in P]
    rt = [r[p] * jnp.exp(cum[p]) for p in P]
    kt = [k2[p] * e_neg[p] for p in P]
    bt = [kka[p] * e_neg[p] for p in P]
    at = [-kkn[p] * jnp.exp(cum[p] - logw[p]) for p in P]
    bk_t = [cat([kka[p] * e_end[p], k2[p] * e_end[p]], 0).T for p in P]
    w_end = [jnp.exp(c_) for c_ in cum_l]

    a_all = [_mm_nt(cat([at[p], rt[p]], 0), cat([stack(bt[p]), stack(kt[p])], 0)) for p in P]
    a_ab = [jnp.where(strict, a[:L, :UW], 0.0) for a in a_all]
    a_ak = [jnp.where(strict, a[:L, UW:], 0.0) for a in a_all]
    a_rb = [jnp.where(incl, a[L:, :UW], 0.0) for a in a_all]
    a_rk = [jnp.where(incl, a[L:, UW:], 0.0) for a in a_all]
    av = [_mm(a_ak[p], stack(v[p])) for p in P]

    tinv = [eye_pair + a for a in a_ab]
    pw = a_ab
    for _ in range(5):
        pw = [_mm(x, stack(x)) for x in pw]
        tinv = [tinv[p] + _mm(tinv[p], stack(pw[p])) for p in P]

    pq = [_mm(tinv[p], cat([stack(at[p]), stack(av[p])], 1)) for p in P]
    pm = [x[:, :UW] for x in pq]
    qm = [x[:, UW:] for x in pq]
    rm = [rt[p] + _mm(a_rb[p], stack(pm[p])) for p in P]
    y0 = [_mm(cat([a_rb[p], a_rk[p]], 1), cat([stack(qm[p]), stack(v[p])], 0)) for p in P]
    mn = [_mm(bk_t[p], cat([cat([pm[p], qm[p]], 1), cat([jnp.zeros_like(v[p]), v[p]], 1)], 0)) for p in P]
    m_mat = [jnp.where(diag_s, w_end[p], 0.0) + jnp.where(same_head, mn[p][:, :UW], 0.0) for p in P]
    n_mat = [jnp.where(same_head, mn[p][:, UW:], 0.0) for p in P]

    state = [st_ref[u] for u in U]
    y_chunks = []
    for ci in range(chunks):
        ps = [ci * units + u for u in U]
        ys = [_mm(cat([rm[p], m_mat[p]], 0), state[u]) for u, p in zip(U, ps)]
        state = [ys[u][L:] + n_mat[p] for u, p in zip(U, ps)]
        y_chunks.append([ys[u][:L] + y0[p] for u, p in zip(U, ps)])
    for u in U:
        st_ref[u] = state[u]

    y = [cat([y_chunks[ci][u] for ci in range(chunks)], 0) for u in U]
    mean = [head_sum(x) * (1.0 / RW_N) for x in y]
    yc = [y[u] - mean[u] for u in U]
    var = [head_sum(x * x) * (1.0 / RW_N) for x in yc]
    for u in U:
        yn = yc[u] * lax.rsqrt(var[u] + GN_EPS) * lnw_ref[:, uls[u]] + lnb_ref[:, uls[u]]
        o_ref[:, uls[u]] = _bf((yn + bonus_in[u] * v_all[:, uls[u]]) * g_out[u])


def _rwkv(z_all, S, mu_r, mu_k, mu_v, mu_l, w0, a0, kk, ka, rk, lnw, lnb, w2, a2, g2):
    T = z_all.shape[0]
    B = T // S
    R = RW_CHUNKS_PER_STEP * RW_L
    nc = S // R
    G = RW_UNITS_PER_STEP
    W = G * RW_UNIT
    ngrp = RW_WIDTH // W
    zspec = lambda off: pl.BlockSpec((R, W), lambda b, g, c, off=off: (b * nc + c, off * ngrp + g))
    vspec = pl.BlockSpec((1, W), lambda b, g, c: (0, g))
    lora_col = (z_all.shape[1] - 512) // 512
    return pl.pallas_call(
        functools.partial(_rwkv_kernel, units=G, chunks=RW_CHUNKS_PER_STEP),
        grid=(B, ngrp, nc),
        in_specs=[zspec(0), zspec(1), zspec(2),
                  pl.BlockSpec((R, 512), lambda b, g, c: (b * nc + c, lora_col)),
                  vspec, vspec, vspec,
                  pl.BlockSpec((1, 512), lambda b, g, c: (0, 0)),
                  vspec, vspec, vspec, vspec, vspec, vspec, vspec,
                  pl.BlockSpec((64, W), lambda b, g, c: (0, g)),
                  pl.BlockSpec((64, W), lambda b, g, c: (0, g)),
                  pl.BlockSpec((256, W), lambda b, g, c: (0, g))],
        out_specs=pl.BlockSpec((R, W), lambda b, g, c: (b * nc + c, g)),
        out_shape=jax.ShapeDtypeStruct((T, RW_WIDTH), BF16),
        scratch_shapes=[pltpu.VMEM((G, RW_UNIT, RW_UNIT), F32),
                        pltpu.VMEM((1, W), F32), pltpu.VMEM((1, W), F32), pltpu.VMEM((1, W), F32),
                        pltpu.VMEM((1, 512), F32)],
        compiler_params=_params("parallel", "parallel", "arbitrary"),
        name="rwkv",
    )(z_all, z_all, z_all, z_all, mu_r, mu_k, mu_v, mu_l, w0, a0, kk, ka, rk, lnw, lnb, w2, a2, g2)


def _mla_proj_kernel(cq_ref, ckv_ref, kpe_ref, cos_ref, sin_ref, qn_ref, kvn_ref,
                     wqn_ref, wqp_ref, wkv_ref, q_ref, kv_ref, kr_ref, *, scale):
    cqn = _bf(_rms(cq_ref[...], qn_ref[...]))
    ckvn = _bf(_rms(ckv_ref[...], kvn_ref[...]))
    cos = cos_ref[...]
    sin = sin_ref[...]
    q_nope = jnp.dot(cqn, wqn_ref[...], preferred_element_type=F32)
    q_pe = jnp.dot(cqn, wqp_ref[...], preferred_element_type=F32)
    half = MLA_ROPE // 2
    first_half = lax.broadcasted_iota(jnp.int32, (1, LANES), 1) < half
    for h in range(MLA_HEADS):
        hs = slice(h * LANES, (h + 1) * LANES)
        pe = q_pe[:, hs]
        swapped = jnp.where(first_half, pltpu.roll(pe, LANES - half, axis=1), pltpu.roll(pe, half, axis=1))
        q_ref[:, 2 * h * LANES:(2 * h + 1) * LANES] = _bf(q_nope[:, hs] * scale)
        q_ref[:, (2 * h + 1) * LANES:(2 * h + 2) * LANES] = _bf((pe * cos + swapped * sin) * scale)
    kv_ref[...] = _bf(jnp.dot(ckvn, wkv_ref[...], preferred_element_type=F32))
    blk = kpe_ref[...]
    kr_ref[...] = _bf(blk * cos + pltpu.roll(blk, MLA_ROPE, axis=1) * sin)


def _mla_proj(z_all, cos_t, sin_t, qn, kvn, wqn, wqp, wkv, tm=512):
    T = z_all.shape[0]
    HW = MLA_HEADS * 2 * LANES
    scale = float((MLA_NOPE + MLA_ROPE) ** -0.5 * 1.4426950408889634)
    full = lambda a: pl.BlockSpec(a.shape, lambda i: (0, 0))
    kpe_blk = (z_all.shape[1] - LANES) // LANES
    return pl.pallas_call(
        functools.partial(_mla_proj_kernel, scale=scale),
        grid=(T // tm,),
        in_specs=[pl.BlockSpec((tm, MLA_RANK), lambda i: (i, 6)),
                  pl.BlockSpec((tm, MLA_RANK), lambda i: (i, 7)),
                  pl.BlockSpec((tm, LANES), lambda i: (i, kpe_blk)),
                  pl.BlockSpec((tm, LANES), lambda i: (i, 0)),
                  pl.BlockSpec((tm, LANES), lambda i: (i, 0)),
                  full(qn), full(kvn), full(wqn), full(wqp), full(wkv)],
        out_specs=[pl.BlockSpec((tm, HW), lambda i: (i, 0)),
                   pl.BlockSpec((tm, HW), lambda i: (i, 0)),
                   pl.BlockSpec((tm, LANES), lambda i: (i, 0))],
        out_shape=[jax.ShapeDtypeStruct((T, HW), BF16),
                   jax.ShapeDtypeStruct((T, HW), BF16),
                   jax.ShapeDtypeStruct((T, LANES), BF16)],
        compiler_params=_params("parallel"),
        name="mla_proj",
    )(z_all, z_all, z_all, cos_t, sin_t, qn, kvn, wqn, wqp, wkv)


ATT_HEADS_PER_STEP = 8


def _attn_kernel(qi_ref, ki_ref, q_ref, kv_ref, kr_ref, o_ref, m_ref, l_ref, acc_ref, *, tq, tk, heads):
    t = pl.program_id(2)
    qi = qi_ref[t]
    ki = ki_ref[t]
    H = range(heads)

    @pl.when(ki == 0)
    def _():
        m_ref[...] = jnp.full_like(m_ref, -1e30)
        l_ref[...] = jnp.zeros_like(l_ref)
        acc_ref[...] = jnp.zeros_like(acc_ref)

    def update(r0, nr, nk, masked):
        rows = slice(r0, r0 + nr)
        kr = kr_ref[0:nk, :]
        s = [lax.dot_general(q_ref[rows, 2 * h * LANES:(2 * h + 2) * LANES],
                             jnp.concatenate([kv_ref[0:nk, 2 * h * LANES:(2 * h + 1) * LANES], kr], axis=1),
                             (((1,), (1,)), ((), ())), preferred_element_type=F32) for h in H]
        if masked:
            q_last = (qi * tq + r0 + lax.broadcasted_iota(jnp.int32, (nr, nk), 0)) | (CHUNK - 1)
            k_pos = ki * tk + lax.broadcasted_iota(jnp.int32, (nr, nk), 1)
            vis = k_pos <= q_last
            s = [jnp.where(vis, x, -1e30) for x in s]
        m_prev = [m_ref[h, rows] for h in H]
        m_new = [jnp.maximum(m_prev[h], jnp.max(s[h], axis=-1, keepdims=True)) for h in H]
        p = [jnp.exp2(s[h] - jnp.concatenate([m_new[h]] * (nk // LANES), axis=1)) for h in H]
        alpha = [jnp.exp2(m_prev[h] - m_new[h]) for h in H]
        pv = [jnp.dot(_bf(p[h]), kv_ref[0:nk, (2 * h + 1) * LANES:(2 * h + 2) * LANES], preferred_element_type=F32) for h in H]
        for h in H:
            psum = sum(p[h][:, c:c + LANES] for c in range(0, nk, LANES))
            l_ref[h, rows] = alpha[h] * l_ref[h, rows] + psum
            acc_ref[h, rows] = alpha[h] * acc_ref[h, rows] + pv[h]
            m_ref[h, rows] = m_new[h]

    @pl.when(ki < qi)
    def _():
        update(0, tq, tk, False)

    @pl.when(ki == qi)
    def _():
        update(0, tq // 2, tk // 2, True)
        update(tq // 2, tq // 2, tk, True)
        for h in H:
            o_ref[:, h * LANES:(h + 1) * LANES] = _bf(acc_ref[h] / jnp.sum(l_ref[h], axis=-1, keepdims=True))


def _attention(q, kv, kr, S, tq=512):
    T = q.shape[0]
    B = T // S
    nq = S // tq
    G = ATT_HEADS_PER_STEP
    pairs = [(a, b) for a in range(nq) for b in range(a + 1)]
    qi_tab = jnp.asarray([a for a, _ in pairs], jnp.int32)
    ki_tab = jnp.asarray([b for _, b in pairs], jnp.int32)
    grid_spec = pltpu.PrefetchScalarGridSpec(
        num_scalar_prefetch=2,
        grid=(B, MLA_HEADS // G, len(pairs)),
        in_specs=[pl.BlockSpec((tq, 2 * G * LANES), lambda b, g, t, qt, kt: (b * nq + qt[t], g)),
                  pl.BlockSpec((tq, 2 * G * LANES), lambda b, g, t, qt, kt: (b * nq + kt[t], g)),
                  pl.BlockSpec((tq, LANES), lambda b, g, t, qt, kt: (b * nq + kt[t], 0))],
        out_specs=pl.BlockSpec((tq, G * LANES), lambda b, g, t, qt, kt: (b * nq + qt[t], g)),
        scratch_shapes=[pltpu.VMEM((G, tq, LANES), F32), pltpu.VMEM((G, tq, LANES), F32), pltpu.VMEM((G, tq, LANES), F32)])
    return pl.pallas_call(
        functools.partial(_attn_kernel, tq=tq, tk=tq, heads=G),
        grid_spec=grid_spec,
        out_shape=jax.ShapeDtypeStruct((T, MLA_HEADS * MLA_V), BF16),
        compiler_params=_params("parallel", "parallel", "arbitrary"),
        name="attention",
    )(qi_tab, ki_tab, q, kv, kr)


def _mix_kernel(*refs, n_cast):
    yrw_ref, ymla_ref, gate_ref, x_ref, wa_ref, wb_ref, wo_ref, g_ref = refs[:8]
    cast_in = refs[8:8 + n_cast]
    o_ref = refs[8 + n_cast]
    cast_out = refs[9 + n_cast:]
    D = x_ref.shape[1]
    t = (jax.nn.sigmoid(gate_ref[:, :D]) * jnp.dot(yrw_ref[...], wa_ref[...], preferred_element_type=F32)
         + jax.nn.sigmoid(gate_ref[:, D:]) * jnp.dot(ymla_ref[...], wb_ref[...], preferred_element_type=F32))
    mix = jnp.dot(_bf(t), wo_ref[...], preferred_element_type=F32)
    o_ref[...] = x_ref[...] + _rms(mix, g_ref[...])
    _cast_blocks(cast_in, cast_out)


def _resident(a):
    return pl.BlockSpec(a.shape, lambda *_: (0,) * a.ndim, pipeline_mode=pl.Buffered(1))


def _mix(y_rw, y_mla, z_all, x2, wa, wb, wo, g, cast, tm=256):
    T, D = x2.shape
    cast_specs = _cast_specs(cast, T // tm, lambda i: i)
    outs = pl.pallas_call(
        functools.partial(_mix_kernel, n_cast=len(cast)),
        grid=(T // tm,),
        in_specs=[pl.BlockSpec((tm, y_rw.shape[1]), lambda i: (i, 0)),
                  pl.BlockSpec((tm, y_mla.shape[1]), lambda i: (i, 0)),
                  pl.BlockSpec((tm, 2 * D), lambda i: (i, 1)),
                  pl.BlockSpec((tm, D), lambda i: (i, 0)),
                  _resident(wa), _resident(wb), _resident(wo), _resident(g)] + cast_specs,
        out_specs=[pl.BlockSpec((tm, D), lambda i: (i, 0))] + cast_specs,
        out_shape=[jax.ShapeDtypeStruct((T, D), F32)] + [jax.ShapeDtypeStruct(a.shape, BF16) for a in cast],
        compiler_params=_params("arbitrary"),
        name="mix",
    )(y_rw, y_mla, z_all, x2, wa, wb, wo, g, *cast)
    return outs[0], outs[1:]


HALO = 16
FFN_SUB = 256
FFN_ROW_BLOCKS = (256, 256, 256, 256)


def _gelu_tanh(x):
    return 0.5 * x * (1.0 + jnp.tanh(0.7978845608028654 * (x + 0.044715 * x * x * x)))


def _ffn_kernel(h_ref, halo_ref, gin_ref, wug_ref, wuv_ref, cwg_ref, cwv_ref, cbg_ref, cbv_ref,
                wd_ref, gout_ref, o_ref, xn_ref, upg_ref, upv_ref, *, tm, tiles_per_seq):
    i = pl.program_id(0)
    j = pl.program_id(1)

    @pl.when(j == 0)
    def _():
        xn_ref[0:HALO, :] = _bf(_rms(halo_ref[...], gin_ref[...]))
        xn_ref[HALO:, :] = _bf(_rms(h_ref[...], gin_ref[...]))
        o_ref[...] = jnp.zeros_like(o_ref)

    keep = jnp.where((i % tiles_per_seq) == 0, 0.0, 1.0).astype(F32)
    tf = upg_ref.shape[1]
    subs = [slice(c, c + FFN_SUB) for c in range(0, tf, FFN_SUB)]

    for w_ref, up_ref in ((wug_ref, upg_ref), (wuv_ref, upv_ref)):
        for cs in subs:
            up = jnp.dot(xn_ref[...], w_ref[:, cs], preferred_element_type=F32)
            up_ref[0:HALO, cs] = up[0:HALO] * keep
            up_ref[HALO:, cs] = up[HALO:]

    def conv(up_ref, cw_ref, cb_ref, cs, r0, nr):
        cw = cw_ref[:, cs]
        return (cw[0:1, :] * up_ref[HALO - 2 + r0:HALO - 2 + r0 + nr, cs]
                + cw[1:2, :] * up_ref[HALO - 1 + r0:HALO - 1 + r0 + nr, cs]
                + cw[2:3, :] * up_ref[HALO + r0:HALO + r0 + nr, cs] + cb_ref[:, cs])

    def act(cs, r0, nr):
        return _bf(_gelu_tanh(conv(upg_ref, cwg_ref, cbg_ref, cs, r0, nr)) * conv(upv_ref, cwv_ref, cbv_ref, cs, r0, nr))

    head = [act(cs, 0, tm) for cs in subs[:-1]]
    r0 = 0
    for nr in FFN_ROW_BLOCKS:
        a = jnp.concatenate([x[r0:r0 + nr] for x in head] + [act(subs[-1], r0, nr)], axis=1)
        o_ref[r0:r0 + nr, :] += jnp.dot(a, wd_ref[...], preferred_element_type=F32)
        r0 += nr
    assert r0 == tm

    @pl.when(j == pl.num_programs(1) - 1)
    def _():
        o_ref[...] = h_ref[...] + _rms(o_ref[...], gout_ref[...])


def _ffn(h1, S, gin, w_up, conv_w, conv_b, w_down, gout, tm=1024, tf=512):
    T, D = h1.shape
    F = w_down.shape[0]
    nf = F // tf
    hb = tm // HALO
    return pl.pallas_call(
        functools.partial(_ffn_kernel, tm=tm, tiles_per_seq=S // tm),
        grid=(T // tm, nf),
        in_specs=[pl.BlockSpec((tm, D), lambda i, j: (i, 0), pipeline_mode=pl.Buffered(1)),
                  pl.BlockSpec((HALO, D), lambda i, j: (jnp.maximum(i * hb - 1, 0), 0)),
                  pl.BlockSpec((1, D), lambda i, j: (0, 0)),
                  pl.BlockSpec((D, tf), lambda i, j: (0, j)),
                  pl.BlockSpec((D, tf), lambda i, j: (0, nf + j)),
                  pl.BlockSpec((3, tf), lambda i, j: (0, j)),
                  pl.BlockSpec((3, tf), lambda i, j: (0, nf + j)),
                  pl.BlockSpec((1, tf), lambda i, j: (0, j)),
                  pl.BlockSpec((1, tf), lambda i, j: (0, nf + j)),
                  pl.BlockSpec((tf, D), lambda i, j: (j, 0)),
                  pl.BlockSpec((1, D), lambda i, j: (0, 0))],
        out_specs=pl.BlockSpec((tm, D), lambda i, j: (i, 0)),
        out_shape=jax.ShapeDtypeStruct((T, D), F32),
        scratch_shapes=[pltpu.VMEM((tm + HALO, D), BF16),
                        pltpu.VMEM((tm + HALO, tf), F32), pltpu.VMEM((tm + HALO, tf), F32)],
        compiler_params=_params("parallel", "arbitrary"),
        name="ffn",
    )(h1, h1, gin, w_up, w_up, conv_w, conv_w, conv_b, conv_b, w_down, gout)


def _ple_kernel(h_ref, p_ref, wpg_ref, wple_ref, g_ref, o_ref):
    h = h_ref[...]
    gate = jax.nn.sigmoid(jnp.dot(_bf(h), wpg_ref[...], preferred_element_type=F32))
    e = jnp.dot(_bf(p_ref[...]), wple_ref[...], preferred_element_type=F32)
    o_ref[...] = h + _rms(gate * e, g_ref[...])


def _ple(h2, p2, wpg, wple, g, tm=512):
    T, D = h2.shape
    return pl.pallas_call(
        _ple_kernel,
        grid=(T // tm,),
        in_specs=[pl.BlockSpec((tm, D), lambda i: (i, 0)),
                  pl.BlockSpec((tm, p2.shape[1]), lambda i: (i, 0)),
                  _resident(wpg), _resident(wple), _resident(g)],
        out_specs=pl.BlockSpec((tm, D), lambda i: (i, 0)),
        out_shape=jax.ShapeDtypeStruct((T, D), F32),
        compiler_params=_params("parallel"),
        name="ple",
    )(h2, p2, wpg, wple, g)


def _swap_halves(w):
    half = w.shape[-1] // 2
    return jnp.concatenate([w[..., half:], w[..., :half]], axis=-1)


def _pack_w_in_kernel(off_ref, w_ref, kpe_ref, o_ref):
    j = pl.program_id(0)
    last = pl.num_programs(0) - 1

    @pl.when(j < last)
    def _():
        o_ref[...] = _bf(w_ref[...])

    @pl.when(j == last)
    def _():
        kpe = kpe_ref[...]
        half = MLA_ROPE // 2
        pad = jnp.zeros((W_IN_TILE - RW_LORA - 2 * MLA_ROPE, kpe.shape[1]), F32)
        o_ref[...] = _bf(jnp.concatenate([w_ref[0:RW_LORA, :], pad, kpe, kpe[half:], kpe[:half]], axis=0))


def _pack_w_in(w_in_t):
    N, D = w_in_t.shape
    rkv_end = 3 * RW_WIDTH
    m0 = rkv_end + RW_LORA
    g0 = m0 + 2 * MLA_RANK + MLA_ROPE
    starts = (list(range(0, rkv_end, W_IN_TILE)) + [m0, m0 + MLA_RANK]
              + list(range(g0, g0 + 4096, W_IN_TILE)) + [rkv_end])
    assert len(starts) * W_IN_TILE == W_IN_PACKED and N == g0 + 4096
    assert all(s % W_IN_ALIGN == 0 for s in starts)
    starts = [s // W_IN_ALIGN for s in starts]
    grid_spec = pltpu.PrefetchScalarGridSpec(
        num_scalar_prefetch=1,
        grid=(len(starts),),
        in_specs=[pl.BlockSpec((pl.Element(W_IN_TILE), pl.Element(D)),
                               lambda j, off: (pl.multiple_of(off[j] * W_IN_ALIGN, W_IN_ALIGN), 0)),
                  pl.BlockSpec((pl.Element(MLA_ROPE), pl.Element(D)), lambda j, off: (m0 + 2 * MLA_RANK, 0))],
        out_specs=pl.BlockSpec((W_IN_TILE, D), lambda j, off: (j, 0)))
    return pl.pallas_call(
        _pack_w_in_kernel,
        grid_spec=grid_spec,
        out_shape=jax.ShapeDtypeStruct((W_IN_PACKED, D), BF16),
        compiler_params=_params("arbitrary"),
        name="pack_w_in",
    )(jnp.asarray(starts, jnp.int32), w_in_t, w_in_t)


def _row(v):
    return v.reshape(1, -1).astype(F32)


def kernel(x, p, positions, pre_mix_norm, w_in, rw_mu, rw_w0, rw_w2, rw_a0, rw_a2, rw_g2, rw_k_k, rw_k_a, rw_r_k, rw_lnx_w, rw_lnx_b, mla_q_norm, mla_w_q_up, mla_kv_norm, mla_w_kv_up, w_branch_rw, w_branch_mla, w_out, post_mix_norm, pre_ffn_norm, w_up, conv_w, conv_b, w_down, post_ffn_norm, w_ple, w_ple_gate, ple_norm):
    B, S, D = x.shape
    T = B * S
    depth = w_in.shape[0]

    inv_freq = ROPE_BASE ** (-jnp.arange(0, MLA_ROPE, 2, dtype=F32) / MLA_ROPE)
    ang = positions.astype(F32)[..., None] * inv_freq
    cos = jnp.cos(ang).reshape(T, MLA_ROPE // 2)
    sin = jnp.sin(ang).reshape(T, MLA_ROPE // 2)
    zpad = jnp.zeros((T, LANES - MLA_ROPE), F32)
    cos_t = jnp.concatenate([cos, cos, zpad], axis=1)
    sin_t = jnp.concatenate([-sin, sin, zpad], axis=1)

    h = x.reshape(T, D)
    for i in range(depth):
        w_in_p = _pack_w_in(jnp.swapaxes(w_in, 1, 2)[i])
        z_all, (w_kv_b, w_rw_b, w_mla_b, w_out_b, w_pg_b, w_ple_b) = _in_proj(
            h, _row(pre_mix_norm[i]), w_in_p,
            [mla_w_kv_up[i], w_branch_rw[i], w_branch_mla[i], w_out[i], w_ple_gate[i], w_ple[i]])

        mu = rw_mu[i]
        mu_l = jnp.concatenate([mu[3 * RW_WIDTH:], jnp.zeros((512 - 288,), F32)]).reshape(1, 512)
        g2p = jnp.concatenate([rw_g2[i], jnp.zeros((256 - rw_g2.shape[1], RW_WIDTH), F32)], axis=0)
        w2p = jnp.concatenate([rw_w2[i]], axis=0)
        y_rw = _rwkv(z_all, S, _row(mu[:RW_WIDTH]), _row(mu[RW_WIDTH:2 * RW_WIDTH]), _row(mu[2 * RW_WIDTH:3 * RW_WIDTH]),
                     mu_l, _row(rw_w0[i]), _row(rw_a0[i]), _row(rw_k_k[i]), _row(rw_k_a[i]), _row(rw_r_k[i]),
                     _row(rw_lnx_w[i]), _row(rw_lnx_b[i]), _bf(w2p), _bf(rw_a2[i]), _bf(g2p))

        wq = mla_w_q_up[i].reshape(MLA_RANK, MLA_HEADS, MLA_NOPE + MLA_ROPE)
        wqn = wq[:, :, :MLA_NOPE].reshape(MLA_RANK, MLA_HEADS * MLA_NOPE)
        wq_pe = wq[:, :, MLA_NOPE:]
        padh = jnp.zeros((MLA_RANK, MLA_HEADS, LANES - MLA_ROPE), F32)
        wqp = jnp.concatenate([wq_pe, padh], axis=-1).reshape(MLA_RANK, MLA_HEADS * LANES)
        q, kv, kr = _mla_proj(z_all, cos_t, sin_t, _row(mla_q_norm[i]), _row(mla_kv_norm[i]),
                              _bf(wqn), _bf(wqp), w_kv_b)
        y_mla = _attention(q, kv, kr, S)

        h, (w_up_b, w_down_b) = _mix(y_rw, y_mla, z_all, h, w_rw_b, w_mla_b, w_out_b, _row(post_mix_norm[i]),
                                     [w_up[i], w_down[i]])
        h = _ffn(h, S, _row(pre_ffn_norm[i]), w_up_b, conv_w[i], _row(conv_b[i]), w_down_b, _row(post_ffn_norm[i]))
        h = _ple(h, p[i].reshape(T, -1), w_pg_b, w_ple_b, _row(ple_norm[i]))
    return h.reshape(B, S, D)
```

```python
import functools

import jax
import jax.numpy as jnp
from jax import lax
from jax.experimental import pallas as pl
from jax.experimental.pallas import tpu as pltpu

NORM_EPS = 1e-6
GN_EPS = 64e-5
CHUNK = 64
ROPE_BASE = 10000.0

RW_HEADS = 16
RW_N = 64
RW_WIDTH = RW_HEADS * RW_N
RW_L = 64
LANES = 128
RW_UNIT_HEADS = 2
RW_UNIT = RW_UNIT_HEADS * RW_N
RW_UNITS_PER_STEP = 8
RW_CHUNKS_PER_STEP = 4

MLA_HEADS = 8
MLA_NOPE = 128
MLA_ROPE = 64
MLA_V = 128
MLA_RANK = 512

RW_LORA = 288
W_IN_TILE = 512
W_IN_ALIGN = 32
W_IN_PACKED = 3 * RW_WIDTH + 2 * MLA_RANK + 4096 + W_IN_TILE

VMEM_LIMIT_BYTES = 56 * 1024 * 1024

BF16 = jnp.bfloat16
F32 = jnp.float32


def _params(*sem):
    return pltpu.CompilerParams(dimension_semantics=sem, vmem_limit_bytes=VMEM_LIMIT_BYTES)


def _bf(x):
    return x.astype(BF16)


def _mm(a, b):
    return jnp.dot(_bf(a), _bf(b), preferred_element_type=F32)


def _mm_nt(a, b):
    return lax.dot_general(_bf(a), _bf(b), (((1,), (1,)), ((), ())), preferred_element_type=F32)


def _rms(x, g):
    return x * lax.rsqrt(jnp.mean(x * x, axis=-1, keepdims=True) + NORM_EPS) * g


def _inproj_kernel(*refs, n_cast, tm):
    x_hbm, g_ref, w_ref = refs[:3]
    cast_in = refs[3:3 + n_cast]
    o_ref = refs[3 + n_cast]
    cast_out = refs[4 + n_cast:4 + 2 * n_cast]
    xbuf_ref, xn_ref, sem = refs[-3:]
    i, j = pl.program_id(0), pl.program_id(1)
    ni, nj = pl.num_programs(0), pl.num_programs(1)
    slot = i % 2

    def x_copy(tile):
        return pltpu.make_async_copy(x_hbm.at[pl.ds(tile * tm, tm), :], xbuf_ref, sem)

    def normalise(dst_slot):
        xn_ref[dst_slot] = _bf(_rms(xbuf_ref[...], g_ref[...]))

    def project():
        o_ref[...] = lax.dot_general(xn_ref[slot], w_ref[...], (((1,), (1,)), ((), ())), preferred_element_type=F32)
        _cast_blocks(cast_in, cast_out)

    @pl.when((i == 0) & (j == 0))
    def _():
        first = x_copy(0)
        first.start()
        first.wait()
        normalise(0)

    has_next = i + 1 < ni

    @pl.when((j == 0) & has_next)
    def _():
        x_copy(i + 1).start()

    @pl.when(j < nj - 1)
    def _():
        project()

    @pl.when((j == nj - 1) & has_next)
    def _():
        x_copy(i + 1).wait()
        normalise(1 - slot)
        project()

    @pl.when((j == nj - 1) & jnp.logical_not(has_next))
    def _():
        project()


def _cast_rows(n_rows, n_steps):
    rb = 16
    while n_rows % rb or n_rows // rb > n_steps:
        rb += 16
    return rb


def _cast_specs(cast, n_steps, step_of):
    specs = []
    for a in cast:
        rb = _cast_rows(a.shape[0], n_steps)
        last = a.shape[0] // rb - 1
        specs.append(pl.BlockSpec((rb, a.shape[1]), lambda *idx, last=last: (jnp.minimum(step_of(*idx), last), 0)))
    return specs


def _cast_blocks(cast_in, cast_out):
    for src, dst in zip(cast_in, cast_out):
        dst[...] = _bf(src[...])


def _in_proj(x2, g, w_t, cast, tm=2048, tn=W_IN_TILE):
    T, D = x2.shape
    N = w_t.shape[0]
    ni, nj = T // tm, N // tn
    cast_specs = _cast_specs(cast, ni * nj, lambda i, j: i * nj + j)
    outs = pl.pallas_call(
        functools.partial(_inproj_kernel, n_cast=len(cast), tm=tm),
        grid=(ni, nj),
        in_specs=[
            pl.BlockSpec(memory_space=pl.ANY),
            pl.BlockSpec((1, D), lambda i, j: (0, 0)),
            pl.BlockSpec((tn, D), lambda i, j: (j, 0)),
        ] + cast_specs,
        out_specs=[pl.BlockSpec((tm, tn), lambda i, j: (i, j))] + cast_specs,
        out_shape=[jax.ShapeDtypeStruct((T, N), F32)] + [jax.ShapeDtypeStruct(a.shape, BF16) for a in cast],
        scratch_shapes=[pltpu.VMEM((tm, D), F32), pltpu.VMEM((2, tm, D), BF16), pltpu.SemaphoreType.DMA(())],
        compiler_params=_params("arbitrary", "arbitrary"),
        name="in_proj",
    )(x2, g, w_t, *cast)
    return outs[0], outs[1:]


def _split3(x):
    hi = _bf(x)
    r1 = x - hi.astype(F32)
    mid = _bf(r1)
    lo = _bf(r1 - mid.astype(F32))
    return hi, mid, lo


def _rwkv_kernel(zr_ref, zk_ref, zv_ref, zl_ref, mur_ref, muk_ref, muv_ref, mul_ref,
                 w0_ref, a0_ref, kk_ref, ka_ref, rk_ref, lnw_ref, lnb_ref,
                 w2_ref, a2_ref, g2_ref, o_ref,
                 st_ref, pr_ref, pk_ref, pv_ref, plr_ref, *, units, chunks):
    L = RW_L
    UW = RW_UNIT
    c = pl.program_id(2)

    @pl.when(c == 0)
    def _():
        st_ref[...] = jnp.zeros_like(st_ref)
        pr_ref[...] = jnp.zeros_like(pr_ref)
        pk_ref[...] = jnp.zeros_like(pk_ref)
        pv_ref[...] = jnp.zeros_like(pv_ref)
        plr_ref[...] = jnp.zeros_like(plr_ref)

    def shift_lerp(z_ref, prev_ref, mu_ref):
        z = z_ref[...]
        rolled = pltpu.roll(z, 1, axis=0)
        row = lax.broadcasted_iota(jnp.int32, z.shape, 0)
        zs = jnp.where(row == 0, prev_ref[...], rolled)
        prev_ref[...] = z[z.shape[0] - 1:, :]
        return z + (zs - z) * mu_ref[...]

    r_all = shift_lerp(zr_ref, pr_ref, mur_ref)
    k_all = shift_lerp(zk_ref, pk_ref, muk_ref)
    v_all = shift_lerp(zv_ref, pv_ref, muv_ref)
    lo_all = shift_lerp(zl_ref, plr_ref, mul_ref)
    wd = jnp.tanh(lo_all[:, 0:64])
    ad = lo_all[:, 64:128]
    gd = jax.nn.sigmoid(lo_all[:, 128:384])

    hshift = RW_N.bit_length() - 1
    head_of_lane = lax.broadcasted_iota(jnp.int32, (1, UW), 1) >> hshift
    head_masks = [head_of_lane == j for j in range(RW_UNIT_HEADS)]
    row_l = lax.broadcasted_iota(jnp.int32, (L, UW), 0)
    col_l = lax.broadcasted_iota(jnp.int32, (L, UW), 1) & (RW_N - 1)
    strict = col_l < row_l
    incl = col_l <= row_l
    eye_pair = (col_l == row_l).astype(F32)
    row_s = lax.broadcasted_iota(jnp.int32, (UW, UW), 0)
    col_s = lax.broadcasted_iota(jnp.int32, (UW, UW), 1)
    same_head = (row_s >> hshift) == (col_s >> hshift)
    diag_s = row_s == col_s
    ones_bd = same_head.astype(BF16)

    def stack(x):
        return jnp.concatenate([jnp.where(m, x, 0.0) for m in head_masks], axis=0)

    def head_sum(x):
        return jnp.dot(_bf(x), ones_bd, preferred_element_type=F32)

    def cat(xs, axis):
        return jnp.concatenate(xs, axis=axis)

    U = range(units)
    uls = [slice(u * UW, (u + 1) * UW) for u in U]
    xw = [w0_ref[:, sl] + _mm(wd, w2_ref[:, sl]) for sl in uls]
    ga = [a0_ref[:, sl] + _mm(ad, a2_ref[:, sl]) for sl in uls]
    g_out = [_mm(gd, g2_ref[:, sl]) for sl in uls]
    kkv = [k_all[:, uls[u]] * kk_ref[:, uls[u]] for u in U]
    ss = [head_sum(x * x) for x in kkv]
    logw_u = [-0.6065306597126334 * jax.nn.sigmoid(x) for x in xw]
    gate_u = [jax.nn.sigmoid(x) for x in ga]
    kkn_u = [kkv[u] / jnp.maximum(jnp.sqrt(ss[u]), 1e-12) for u in U]
    k2_u = [k_all[:, uls[u]] * (1.0 + (gate_u[u] - 1.0) * ka_ref[:, uls[u]]) for u in U]
    bonus_in = [head_sum(r_all[:, uls[u]] * k2_u[u] * rk_ref[:, uls[u]]) for u in U]

    items = [(ci, u) for ci in range(chunks) for u in U]
    P = range(len(items))
    rows = [slice(ci * L, (ci + 1) * L) for ci, _ in items]
    r = [r_all[rows[p], uls[items[p][1]]] for p in P]
    v = [v_all[rows[p], uls[items[p][1]]] for p in P]
    logw = [logw_u[items[p][1]][rows[p]] for p in P]
    gate = [gate_u[items[p][1]][rows[p]] for p in P]
    kkn = [kkn_u[items[p][1]][rows[p]] for p in P]
    k2 = [k2_u[items[p][1]][rows[p]] for p in P]
    parts = [_split3(x) for x in logw]
    tri3 = ((lax.broadcasted_iota(jnp.int32, (L, 3 * L), 1) & (L - 1))
            <= lax.broadcasted_iota(jnp.int32, (L, 3 * L), 0)).astype(BF16)
    cum = [jnp.dot(tri3, cat(list(parts[p]), 0), preferred_element_type=F32) for p in P]
    cum_l = [c_[L - 1:L, :] for c_ in cum]
    e_neg = [jnp.exp(-c_) for c_ in cum]
    e_end = [jnp.exp(cum_l[p] - cum[p]) for p in P]
    kka = [kkn[p] * gate[p] for p in P]
    rt = [r[p] * jnp.exp(cum[p]) for p in P]
    kt = [k2[p] * e_neg[p] for p in P]
    bt = [kka[p] * e_neg[p] for p in P]
    at = [-kkn[p] * jnp.exp(cum[p] - logw[p]) for p in P]
    bk_t = [cat([kka[p] * e_end[p], k2[p] * e_end[p]], 0).T for p in P]
    w_end = [jnp.exp(c_) for c_ in cum_l]

    a_all = [_mm_nt(cat([at[p], rt[p]], 0), cat([stack(bt[p]), stack(kt[p])], 0)) for p in P]
    a_ab = [jnp.where(strict, a[:L, :UW], 0.0) for a in a_all]
    a_ak = [jnp.where(strict, a[:L, UW:], 0.0) for a in a_all]
    a_rb = [jnp.where(incl, a[L:, :UW], 0.0) for a in a_all]
    a_rk = [jnp.where(incl, a[L:, UW:], 0.0) for a in a_all]
    av = [_mm(a_ak[p], stack(v[p])) for p in P]

    tinv = [eye_pair + a for a in a_ab]
    pw = a_ab
    for _ in range(5):
        pw = [_mm(x, stack(x)) for x in pw]
        tinv = [tinv[p] + _mm(tinv[p], stack(pw[p])) for p in P]

    pq = [_mm(tinv[p], cat([stack(at[p]), stack(av[p])], 1)) for p in P]
    pm = [x[:, :UW] for x in pq]
    qm = [x[:, UW:] for x in pq]
    rm = [rt[p] + _mm(a_rb[p], stack(pm[p])) for p in P]
    y0 = [_mm(cat([a_rb[p], a_rk[p]], 1), cat([stack(qm[p]), stack(v[p])], 0)) for p in P]
    mn = [_mm(bk_t[p], cat([cat([pm[p], qm[p]], 1), cat([jnp.zeros_like(v[p]), v[p]], 1)], 0)) for p in P]
    m_mat = [jnp.where(diag_s, w_end[p], 0.0) + jnp.where(same_head, mn[p][:, :UW], 0.0) for p in P]
    n_mat = [jnp.where(same_head, mn[p][:, UW:], 0.0) for p in P]

    state = [st_ref[u] for u in U]
    y_chunks = []
    for ci in range(chunks):
        ps = [ci * units + u for u in U]
        ys = [_mm(cat([rm[p], m_mat[p]], 0), state[u]) for u, p in zip(U, ps)]
        state = [ys[u][L:] + n_mat[p] for u, p in zip(U, ps)]
        y_chunks.append([ys[u][:L] + y0[p] for u, p in zip(U, ps)])
    for u in U:
        st_ref[u] = state[u]

    y = [cat([y_chunks[ci][u] for ci in range(chunks)], 0) for u in U]
    mean = [head_sum(x) * (1.0 / RW_N) for x in y]
    yc = [y[u] - mean[u] for u in U]
    var = [head_sum(x * x) * (1.0 / RW_N) for x in yc]
    for u in U:
        yn = yc[u] * lax.rsqrt(var[u] + GN_EPS) * lnw_ref[:, uls[u]] + lnb_ref[:, uls[u]]
        o_ref[:, uls[u]] = _bf((yn + bonus_in[u] * v_all[:, uls[u]]) * g_out[u])


RW_N_INPUTS = 18
RW_N_SCRATCH = 5


def _rwkv_cast_kernel(*refs, n_cast, units, chunks):
    ins = refs[:RW_N_INPUTS]
    cast_in = refs[RW_N_INPUTS:RW_N_INPUTS + n_cast]
    o_ref = refs[RW_N_INPUTS + n_cast]
    cast_out = refs[RW_N_INPUTS + n_cast + 1:RW_N_INPUTS + 2 * n_cast + 1]
    scratch = refs[-RW_N_SCRATCH:]
    _rwkv_kernel(*ins, o_ref, *scratch, units=units, chunks=chunks)
    _cast_blocks(cast_in, cast_out)


def _rwkv(z_all, S, mu_r, mu_k, mu_v, mu_l, w0, a0, kk, ka, rk, lnw, lnb, w2, a2, g2, cast):
    T = z_all.shape[0]
    B = T // S
    R = RW_CHUNKS_PER_STEP * RW_L
    nc = S // R
    G = RW_UNITS_PER_STEP
    W = G * RW_UNIT
    ngrp = RW_WIDTH // W
    zspec = lambda off: pl.BlockSpec((R, W), lambda b, g, c, off=off: (b * nc + c, off * ngrp + g))
    vspec = pl.BlockSpec((1, W), lambda b, g, c: (0, g))
    lora_col = (z_all.shape[1] - 512) // 512
    cast_specs = _cast_specs(cast, B * ngrp * nc, lambda b, g, c: (b * ngrp + g) * nc + c)
    outs = pl.pallas_call(
        functools.partial(_rwkv_cast_kernel, n_cast=len(cast), units=G, chunks=RW_CHUNKS_PER_STEP),
        grid=(B, ngrp, nc),
        in_specs=[zspec(0), zspec(1), zspec(2),
                  pl.BlockSpec((R, 512), lambda b, g, c: (b * nc + c, lora_col)),
                  vspec, vspec, vspec,
                  pl.BlockSpec((1, 512), lambda b, g, c: (0, 0)),
                  vspec, vspec, vspec, vspec, vspec, vspec, vspec,
                  pl.BlockSpec((64, W), lambda b, g, c: (0, g)),
                  pl.BlockSpec((64, W), lambda b, g, c: (0, g)),
                  pl.BlockSpec((256, W), lambda b, g, c: (0, g))] + cast_specs,
        out_specs=[pl.BlockSpec((R, W), lambda b, g, c: (b * nc + c, g))] + cast_specs,
        out_shape=[jax.ShapeDtypeStruct((T, RW_WIDTH), BF16)] + [jax.ShapeDtypeStruct(a.shape, BF16) for a in cast],
        scratch_shapes=[pltpu.VMEM((G, RW_UNIT, RW_UNIT), F32),
                        pltpu.VMEM((1, W), F32), pltpu.VMEM((1, W), F32), pltpu.VMEM((1, W), F32),
                        pltpu.VMEM((1, 512), F32)],
        compiler_params=_params("arbitrary", "arbitrary", "arbitrary"),
        name="rwkv",
    )(z_all, z_all, z_all, z_all, mu_r, mu_k, mu_v, mu_l, w0, a0, kk, ka, rk, lnw, lnb, w2, a2, g2, *cast)
    return outs[0], outs[1:]


def _mla_proj_kernel(cq_ref, ckv_ref, kpe_ref, cos_ref, sin_ref, qn_ref, kvn_ref,
                     wqn_ref, wqp_ref, wkv_ref, q_ref, kv_ref, kr_ref, *, scale):
    cqn = _bf(_rms(cq_ref[...], qn_ref[...]))
    ckvn = _bf(_rms(ckv_ref[...], kvn_ref[...]))
    cos = cos_ref[...]
    sin = sin_ref[...]
    q_nope = jnp.dot(cqn, wqn_ref[...], preferred_element_type=F32)
    q_pe = jnp.dot(cqn, wqp_ref[...], preferred_element_type=F32)
    half = MLA_ROPE // 2
    first_half = lax.broadcasted_iota(jnp.int32, (1, LANES), 1) < half
    for h in range(MLA_HEADS):
        hs = slice(h * LANES, (h + 1) * LANES)
        pe = q_pe[:, hs]
        swapped = jnp.where(first_half, pltpu.roll(pe, LANES - half, axis=1), pltpu.roll(pe, half, axis=1))
        q_ref[:, 2 * h * LANES:(2 * h + 1) * LANES] = _bf(q_nope[:, hs] * scale)
        q_ref[:, (2 * h + 1) * LANES:(2 * h + 2) * LANES] = _bf((pe * cos + swapped * sin) * scale)
    kv_ref[...] = _bf(jnp.dot(ckvn, wkv_ref[...], preferred_element_type=F32))
    blk = kpe_ref[...]
    kr_ref[...] = _bf(blk * cos + pltpu.roll(blk, MLA_ROPE, axis=1) * sin)


def _mla_proj(z_all, cos_t, sin_t, qn, kvn, wqn, wqp, wkv, tm=512):
    T = z_all.shape[0]
    HW = MLA_HEADS * 2 * LANES
    scale = float((MLA_NOPE + MLA_ROPE) ** -0.5 * 1.4426950408889634)
    full = lambda a: pl.BlockSpec(a.shape, lambda i: (0, 0))
    kpe_blk = (z_all.shape[1] - LANES) // LANES
    return pl.pallas_call(
        functools.partial(_mla_proj_kernel, scale=scale),
        grid=(T // tm,),
        in_specs=[pl.BlockSpec((tm, MLA_RANK), lambda i: (i, 6)),
                  pl.BlockSpec((tm, MLA_RANK), lambda i: (i, 7)),
                  pl.BlockSpec((tm, LANES), lambda i: (i, kpe_blk)),
                  pl.BlockSpec((tm, LANES), lambda i: (i, 0)),
                  pl.BlockSpec((tm, LANES), lambda i: (i, 0)),
                  full(qn), full(kvn), full(wqn), full(wqp), full(wkv)],
        out_specs=[pl.BlockSpec((tm, HW), lambda i: (i, 0)),
                   pl.BlockSpec((tm, HW), lambda i: (i, 0)),
                   pl.BlockSpec((tm, LANES), lambda i: (i, 0))],
        out_shape=[jax.ShapeDtypeStruct((T, HW), BF16),
                   jax.ShapeDtypeStruct((T, HW), BF16),
                   jax.ShapeDtypeStruct((T, LANES), BF16)],
        compiler_params=_params("parallel"),
        name="mla_proj",
    )(z_all, z_all, z_all, cos_t, sin_t, qn, kvn, wqn, wqp, wkv)


ATT_HEADS_PER_STEP = 8


def _attn_kernel(qi_ref, ki_ref, q_ref, kv_ref, kr_ref, o_ref, m_ref, l_ref, acc_ref, *, tq, tk, heads):
    t = pl.program_id(2)
    qi = qi_ref[t]
    ki = ki_ref[t]
    H = range(heads)

    @pl.when(ki == 0)
    def _():
        m_ref[...] = jnp.full_like(m_ref, -1e30)
        l_ref[...] = jnp.zeros_like(l_ref)
        acc_ref[...] = jnp.zeros_like(acc_ref)

    def update(r0, nr, nk, masked):
        rows = slice(r0, r0 + nr)
        kr = kr_ref[0:nk, :]
        s = [lax.dot_general(q_ref[rows, 2 * h * LANES:(2 * h + 2) * LANES],
                             jnp.concatenate([kv_ref[0:nk, 2 * h * LANES:(2 * h + 1) * LANES], kr], axis=1),
                             (((1,), (1,)), ((), ())), preferred_element_type=F32) for h in H]
        if masked:
            q_last = (qi * tq + r0 + lax.broadcasted_iota(jnp.int32, (nr, nk), 0)) | (CHUNK - 1)
            k_pos = ki * tk + lax.broadcasted_iota(jnp.int32, (nr, nk), 1)
            vis = k_pos <= q_last
            s = [jnp.where(vis, x, -1e30) for x in s]
        m_prev = [m_ref[h, rows] for h in H]
        m_new = [jnp.maximum(m_prev[h], jnp.max(s[h], axis=-1, keepdims=True)) for h in H]
        p = [jnp.exp2(s[h] - jnp.concatenate([m_new[h]] * (nk // LANES), axis=1)) for h in H]
        alpha = [jnp.exp2(m_prev[h] - m_new[h]) for h in H]
        pv = [jnp.dot(_bf(p[h]), kv_ref[0:nk, (2 * h + 1) * LANES:(2 * h + 2) * LANES], preferred_element_type=F32) for h in H]
        for h in H:
            psum = sum(p[h][:, c:c + LANES] for c in range(0, nk, LANES))
            l_ref[h, rows] = alpha[h] * l_ref[h, rows] + psum
            acc_ref[h, rows] = alpha[h] * acc_ref[h, rows] + pv[h]
            m_ref[h, rows] = m_new[h]

    @pl.when(ki < qi)
    def _():
        update(0, tq, tk, False)

    @pl.when(ki == qi)
    def _():
        update(0, tq // 2, tk // 2, True)
        update(tq // 2, tq // 2, tk, True)
        for h in H:
            o_ref[:, h * LANES:(h + 1) * LANES] = _bf(acc_ref[h] / jnp.sum(l_ref[h], axis=-1, keepdims=True))


def _attention(q, kv, kr, S, tq=512):
    T = q.shape[0]
    B = T // S
    nq = S // tq
    G = ATT_HEADS_PER_STEP
    pairs = [(a, b) for a in range(nq) for b in range(a + 1)]
    qi_tab = jnp.asarray([a for a, _ in pairs], jnp.int32)
    ki_tab = jnp.asarray([b for _, b in pairs], jnp.int32)
    grid_spec = pltpu.PrefetchScalarGridSpec(
        num_scalar_prefetch=2,
        grid=(B, MLA_HEADS // G, len(pairs)),
        in_specs=[pl.BlockSpec((tq, 2 * G * LANES), lambda b, g, t, qt, kt: (b * nq + qt[t], g)),
                  pl.BlockSpec((tq, 2 * G * LANES), lambda b, g, t, qt, kt: (b * nq + kt[t], g)),
                  pl.BlockSpec((tq, LANES), lambda b, g, t, qt, kt: (b * nq + kt[t], 0))],
        out_specs=pl.BlockSpec((tq, G * LANES), lambda b, g, t, qt, kt: (b * nq + qt[t], g)),
        scratch_shapes=[pltpu.VMEM((G, tq, LANES), F32), pltpu.VMEM((G, tq, LANES), F32), pltpu.VMEM((G, tq, LANES), F32)])
    return pl.pallas_call(
        functools.partial(_attn_kernel, tq=tq, tk=tq, heads=G),
        grid_spec=grid_spec,
        out_shape=jax.ShapeDtypeStruct((T, MLA_HEADS * MLA_V), BF16),
        compiler_params=_params("parallel", "parallel", "arbitrary"),
        name="attention",
    )(qi_tab, ki_tab, q, kv, kr)


def _mix_kernel(*refs, n_cast):
    yrw_ref, ymla_ref, gate_ref, x_ref, wa_ref, wb_ref, wo_ref, g_ref = refs[:8]
    cast_in = refs[8:8 + n_cast]
    o_ref = refs[8 + n_cast]
    cast_out = refs[9 + n_cast:]
    D = x_ref.shape[1]
    t = (jax.nn.sigmoid(gate_ref[:, :D]) * jnp.dot(yrw_ref[...], wa_ref[...], preferred_element_type=F32)
         + jax.nn.sigmoid(gate_ref[:, D:]) * jnp.dot(ymla_ref[...], wb_ref[...], preferred_element_type=F32))
    mix = jnp.dot(_bf(t), wo_ref[...], preferred_element_type=F32)
    o_ref[...] = x_ref[...] + _rms(mix, g_ref[...])
    _cast_blocks(cast_in, cast_out)


def _resident(a):
    return pl.BlockSpec(a.shape, lambda *_: (0,) * a.ndim, pipeline_mode=pl.Buffered(1))


def _mix(y_rw, y_mla, z_all, x2, wa, wb, wo, g, cast, tm=256):
    T, D = x2.shape
    cast_specs = _cast_specs(cast, T // tm, lambda i: i)
    outs = pl.pallas_call(
        functools.partial(_mix_kernel, n_cast=len(cast)),
        grid=(T // tm,),
        in_specs=[pl.BlockSpec((tm, y_rw.shape[1]), lambda i: (i, 0)),
                  pl.BlockSpec((tm, y_mla.shape[1]), lambda i: (i, 0)),
                  pl.BlockSpec((tm, 2 * D), lambda i: (i, 1)),
                  pl.BlockSpec((tm, D), lambda i: (i, 0)),
                  _resident(wa), _resident(wb), _resident(wo), _resident(g)] + cast_specs,
        out_specs=[pl.BlockSpec((tm, D), lambda i: (i, 0))] + cast_specs,
        out_shape=[jax.ShapeDtypeStruct((T, D), F32)] + [jax.ShapeDtypeStruct(a.shape, BF16) for a in cast],
        compiler_params=_params("arbitrary"),
        name="mix",
    )(y_rw, y_mla, z_all, x2, wa, wb, wo, g, *cast)
    return outs[0], outs[1:]


HALO = 16
FFN_SUB = 256
FFN_ROW_BLOCKS = (256, 256, 256, 256)


def _gelu_tanh(x):
    return 0.5 * x * (1.0 + jnp.tanh(0.7978845608028654 * (x + 0.044715 * x * x * x)))


def _ffn_kernel(h_ref, halo_ref, gin_ref, wug_ref, wuv_ref, cwg_ref, cwv_ref, cbg_ref, cbv_ref,
                wd_ref, gout_ref, o_ref, xn_ref, upg_ref, upv_ref, *, tm, tiles_per_seq):
    i = pl.program_id(0)
    j = pl.program_id(1)

    @pl.when(j == 0)
    def _():
        xn_ref[0:HALO, :] = _bf(_rms(halo_ref[...], gin_ref[...]))
        xn_ref[HALO:, :] = _bf(_rms(h_ref[...], gin_ref[...]))
        o_ref[...] = jnp.zeros_like(o_ref)

    keep = jnp.where((i % tiles_per_seq) == 0, 0.0, 1.0).astype(F32)
    tf = upg_ref.shape[1]
    subs = [slice(c, c + FFN_SUB) for c in range(0, tf, FFN_SUB)]

    for w_ref, up_ref in ((wug_ref, upg_ref), (wuv_ref, upv_ref)):
        for cs in subs:
            up = jnp.dot(xn_ref[...], w_ref[:, cs], preferred_element_type=F32)
            up_ref[0:HALO, cs] = up[0:HALO] * keep
            up_ref[HALO:, cs] = up[HALO:]

    def conv(up_ref, cw_ref, cb_ref, cs, r0, nr):
        cw = cw_ref[:, cs]
        return (cw[0:1, :] * up_ref[HALO - 2 + r0:HALO - 2 + r0 + nr, cs]
                + cw[1:2, :] * up_ref[HALO - 1 + r0:HALO - 1 + r0 + nr, cs]
                + cw[2:3, :] * up_ref[HALO + r0:HALO + r0 + nr, cs] + cb_ref[:, cs])

    def act(cs, r0, nr):
        return _bf(_gelu_tanh(conv(upg_ref, cwg_ref, cbg_ref, cs, r0, nr)) * conv(upv_ref, cwv_ref, cbv_ref, cs, r0, nr))

    head = [act(cs, 0, tm) for cs in subs[:-1]]
    r0 = 0
    for nr in FFN_ROW_BLOCKS:
        a = jnp.concatenate([x[r0:r0 + nr] for x in head] + [act(subs[-1], r0, nr)], axis=1)
        o_ref[r0:r0 + nr, :] += jnp.dot(a, wd_ref[...], preferred_element_type=F32)
        r0 += nr
    assert r0 == tm

    @pl.when(j == pl.num_programs(1) - 1)
    def _():
        o_ref[...] = h_ref[...] + _rms(o_ref[...], gout_ref[...])


def _ffn(h1, S, gin, w_up, conv_w, conv_b, w_down, gout, tm=1024, tf=512):
    T, D = h1.shape
    F = w_down.shape[0]
    nf = F // tf
    hb = tm // HALO
    return pl.pallas_call(
        functools.partial(_ffn_kernel, tm=tm, tiles_per_seq=S // tm),
        grid=(T // tm, nf),
        in_specs=[pl.BlockSpec((tm, D), lambda i, j: (i, 0), pipeline_mode=pl.Buffered(1)),
                  pl.BlockSpec((HALO, D), lambda i, j: (jnp.maximum(i * hb - 1, 0), 0)),
                  pl.BlockSpec((1, D), lambda i, j: (0, 0)),
                  pl.BlockSpec((D, tf), lambda i, j: (0, j)),
                  pl.BlockSpec((D, tf), lambda i, j: (0, nf + j)),
                  pl.BlockSpec((3, tf), lambda i, j: (0, j)),
                  pl.BlockSpec((3, tf), lambda i, j: (0, nf + j)),
                  pl.BlockSpec((1, tf), lambda i, j: (0, j)),
                  pl.BlockSpec((1, tf), lambda i, j: (0, nf + j)),
                  pl.BlockSpec((tf, D), lambda i, j: (j, 0)),
                  pl.BlockSpec((1, D), lambda i, j: (0, 0))],
        out_specs=pl.BlockSpec((tm, D), lambda i, j: (i, 0)),
        out_shape=jax.ShapeDtypeStruct((T, D), F32),
        scratch_shapes=[pltpu.VMEM((tm + HALO, D), BF16),
                        pltpu.VMEM((tm + HALO, tf), F32), pltpu.VMEM((tm + HALO, tf), F32)],
        compiler_params=_params("parallel", "arbitrary"),
        name="ffn",
    )(h1, h1, gin, w_up, w_up, conv_w, conv_w, conv_b, conv_b, w_down, gout)


def _ple_kernel(h_ref, p_ref, wpg_ref, wple_ref, g_ref, o_ref):
    h = h_ref[...]
    gate = jax.nn.sigmoid(jnp.dot(_bf(h), wpg_ref[...], preferred_element_type=F32))
    e = jnp.dot(_bf(p_ref[...]), wple_ref[...], preferred_element_type=F32)
    o_ref[...] = h + _rms(gate * e, g_ref[...])


def _ple(h2, p2, wpg, wple, g, tm=512):
    T, D = h2.shape
    return pl.pallas_call(
        _ple_kernel,
        grid=(T // tm,),
        in_specs=[pl.BlockSpec((tm, D), lambda i: (i, 0)),
                  pl.BlockSpec((tm, p2.shape[1]), lambda i: (i, 0)),
                  _resident(wpg), _resident(wple), _resident(g)],
        out_specs=pl.BlockSpec((tm, D), lambda i: (i, 0)),
        out_shape=jax.ShapeDtypeStruct((T, D), F32),
        compiler_params=_params("parallel"),
        name="ple",
    )(h2, p2, wpg, wple, g)


def _swap_halves(w):
    half = w.shape[-1] // 2
    return jnp.concatenate([w[..., half:], w[..., :half]], axis=-1)


def _pack_w_in_kernel(off_ref, w_ref, kpe_ref, o_ref):
    j = pl.program_id(0)
    last = pl.num_programs(0) - 1

    @pl.when(j < last)
    def _():
        o_ref[...] = _bf(w_ref[...])

    @pl.when(j == last)
    def _():
        kpe = kpe_ref[...]
        half = MLA_ROPE // 2
        pad = jnp.zeros((W_IN_TILE - RW_LORA - 2 * MLA_ROPE, kpe.shape[1]), F32)
        o_ref[...] = _bf(jnp.concatenate([w_ref[0:RW_LORA, :], pad, kpe, kpe[half:], kpe[:half]], axis=0))


def _pack_w_in(w_in_t):
    N, D = w_in_t.shape
    rkv_end = 3 * RW_WIDTH
    m0 = rkv_end + RW_LORA
    g0 = m0 + 2 * MLA_RANK + MLA_ROPE
    starts = (list(range(0, rkv_end, W_IN_TILE)) + [m0, m0 + MLA_RANK]
              + list(range(g0, g0 + 4096, W_IN_TILE)) + [rkv_end])
    assert len(starts) * W_IN_TILE == W_IN_PACKED and N == g0 + 4096
    assert all(s % W_IN_ALIGN == 0 for s in starts)
    starts = [s // W_IN_ALIGN for s in starts]
    grid_spec = pltpu.PrefetchScalarGridSpec(
        num_scalar_prefetch=1,
        grid=(len(starts),),
        in_specs=[pl.BlockSpec((pl.Element(W_IN_TILE), pl.Element(D)),
                               lambda j, off: (pl.multiple_of(off[j] * W_IN_ALIGN, W_IN_ALIGN), 0)),
                  pl.BlockSpec((pl.Element(MLA_ROPE), pl.Element(D)), lambda j, off: (m0 + 2 * MLA_RANK, 0))],
        out_specs=pl.BlockSpec((W_IN_TILE, D), lambda j, off: (j, 0)))
    return pl.pallas_call(
        _pack_w_in_kernel,
        grid_spec=grid_spec,
        out_shape=jax.ShapeDtypeStruct((W_IN_PACKED, D), BF16),
        compiler_params=_params("arbitrary"),
        name="pack_w_in",
    )(jnp.asarray(starts, jnp.int32), w_in_t, w_in_t)


def _row(v):
    return v.reshape(1, -1).astype(F32)


def kernel(x, p, positions, pre_mix_norm, w_in, rw_mu, rw_w0, rw_w2, rw_a0, rw_a2, rw_g2, rw_k_k, rw_k_a, rw_r_k, rw_lnx_w, rw_lnx_b, mla_q_norm, mla_w_q_up, mla_kv_norm, mla_w_kv_up, w_branch_rw, w_branch_mla, w_out, post_mix_norm, pre_ffn_norm, w_up, conv_w, conv_b, w_down, post_ffn_norm, w_ple, w_ple_gate, ple_norm):
    B, S, D = x.shape
    T = B * S
    depth = w_in.shape[0]

    inv_freq = ROPE_BASE ** (-jnp.arange(0, MLA_ROPE, 2, dtype=F32) / MLA_ROPE)
    ang = positions.astype(F32)[..., None] * inv_freq
    cos = jnp.cos(ang).reshape(T, MLA_ROPE // 2)
    sin = jnp.sin(ang).reshape(T, MLA_ROPE // 2)
    zpad = jnp.zeros((T, LANES - MLA_ROPE), F32)
    cos_t = jnp.concatenate([cos, cos, zpad], axis=1)
    sin_t = jnp.concatenate([-sin, sin, zpad], axis=1)

    h = x.reshape(T, D)
    for i in range(depth):
        w_in_p = _pack_w_in(jnp.swapaxes(w_in, 1, 2)[i])
        z_all, (w_kv_b, w_rw_b, w_mla_b, w_out_b, w_pg_b, w_ple_b) = _in_proj(
            h, _row(pre_mix_norm[i]), w_in_p,
            [mla_w_kv_up[i], w_branch_rw[i], w_branch_mla[i], w_out[i], w_ple_gate[i], w_ple[i]])

        mu = rw_mu[i]
        mu_l = jnp.concatenate([mu[3 * RW_WIDTH:], jnp.zeros((512 - 288,), F32)]).reshape(1, 512)
        g2p = jnp.concatenate([rw_g2[i], jnp.zeros((256 - rw_g2.shape[1], RW_WIDTH), F32)], axis=0)
        w2p = jnp.concatenate([rw_w2[i]], axis=0)
        y_rw, (w_up_b, w_down_b) = _rwkv(
            z_all, S, _row(mu[:RW_WIDTH]), _row(mu[RW_WIDTH:2 * RW_WIDTH]), _row(mu[2 * RW_WIDTH:3 * RW_WIDTH]),
            mu_l, _row(rw_w0[i]), _row(rw_a0[i]), _row(rw_k_k[i]), _row(rw_k_a[i]), _row(rw_r_k[i]),
            _row(rw_lnx_w[i]), _row(rw_lnx_b[i]), _bf(w2p), _bf(rw_a2[i]), _bf(g2p), [w_up[i], w_down[i]])

        wq = mla_w_q_up[i].reshape(MLA_RANK, MLA_HEADS, MLA_NOPE + MLA_ROPE)
        wqn = wq[:, :, :MLA_NOPE].reshape(MLA_RANK, MLA_HEADS * MLA_NOPE)
        wq_pe = wq[:, :, MLA_NOPE:]
        padh = jnp.zeros((MLA_RANK, MLA_HEADS, LANES - MLA_ROPE), F32)
        wqp = jnp.concatenate([wq_pe, padh], axis=-1).reshape(MLA_RANK, MLA_HEADS * LANES)
        q, kv, kr = _mla_proj(z_all, cos_t, sin_t, _row(mla_q_norm[i]), _row(mla_kv_norm[i]),
                              _bf(wqn), _bf(wqp), w_kv_b)
        y_mla = _attention(q, kv, kr, S)

        h, _ = _mix(y_rw, y_mla, z_all, h, w_rw_b, w_mla_b, w_out_b, _row(post_mix_norm[i]), [])
        h = _ffn(h, S, _row(pre_ffn_norm[i]), w_up_b, conv_w[i], _row(conv_b[i]), w_down_b, _row(post_ffn_norm[i]))
        h = _ple(h, p[i].reshape(T, -1), w_pg_b, w_ple_b, _row(ple_norm[i]))
    return h.reshape(B, S, D)
```

```python
import functools

import jax
import jax.numpy as jnp
from jax import lax
from jax.experimental import pallas as pl
from jax.experimental.pallas import tpu as pltpu

NORM_EPS = 1e-6
GN_EPS = 64e-5
CHUNK = 64
ROPE_BASE = 10000.0

RW_HEADS = 16
RW_N = 64
RW_WIDTH = RW_HEADS * RW_N
RW_L = 64
LANES = 128
RW_UNIT_HEADS = 2
RW_UNIT = RW_UNIT_HEADS * RW_N
RW_UNITS_PER_STEP = 8
RW_CHUNKS_PER_STEP = 4

MLA_HEADS = 8
MLA_NOPE = 128
MLA_ROPE = 64
MLA_V = 128
MLA_RANK = 512

RW_LORA = 288
W_IN_TILE = 512
W_IN_ALIGN = 32
W_IN_PACKED = 3 * RW_WIDTH + 2 * MLA_RANK + 4096 + W_IN_TILE

VMEM_LIMIT_BYTES = 56 * 1024 * 1024

BF16 = jnp.bfloat16
F32 = jnp.float32


def _params(*sem):
    return pltpu.CompilerParams(dimension_semantics=sem, vmem_limit_bytes=VMEM_LIMIT_BYTES)


def _bf(x):
    return x.astype(BF16)


def _mm(a, b):
    return jnp.dot(_bf(a), _bf(b), preferred_element_type=F32)


def _mm_nt(a, b):
    return lax.dot_general(_bf(a), _bf(b), (((1,), (1,)), ((), ())), preferred_element_type=F32)


def _rms(x, g):
    return x * lax.rsqrt(jnp.mean(x * x, axis=-1, keepdims=True) + NORM_EPS) * g


def _inproj_kernel(*refs, n_cast, tm):
    x_hbm, g_ref, w_ref = refs[:3]
    cast_in = refs[3:3 + n_cast]
    o_ref = refs[3 + n_cast]
    cast_out = refs[4 + n_cast:4 + 2 * n_cast]
    xbuf_ref, xn_ref, sem = refs[-3:]
    i, j = pl.program_id(0), pl.program_id(1)
    ni, nj = pl.num_programs(0), pl.num_programs(1)
    slot = i % 2

    def x_copy(tile):
        return pltpu.make_async_copy(x_hbm.at[pl.ds(tile * tm, tm), :], xbuf_ref, sem)

    def normalise(dst_slot):
        xn_ref[dst_slot] = _bf(_rms(xbuf_ref[...], g_ref[...]))

    def project():
        o_ref[...] = lax.dot_general(xn_ref[slot], w_ref[...], (((1,), (1,)), ((), ())), preferred_element_type=F32)
        _cast_blocks(cast_in, cast_out)

    @pl.when((i == 0) & (j == 0))
    def _():
        first = x_copy(0)
        first.start()
        first.wait()
        normalise(0)

    has_next = i + 1 < ni

    @pl.when((j == 0) & has_next)
    def _():
        x_copy(i + 1).start()

    @pl.when(j < nj - 1)
    def _():
        project()

    @pl.when((j == nj - 1) & has_next)
    def _():
        x_copy(i + 1).wait()
        normalise(1 - slot)
        project()

    @pl.when((j == nj - 1) & jnp.logical_not(has_next))
    def _():
        project()


def _cast_rows(n_rows, n_steps):
    rb = 16
    while n_rows % rb or n_rows // rb > n_steps:
        rb += 16
    return rb


def _cast_specs(cast, n_steps, step_of):
    specs = []
    for a in cast:
        rb = _cast_rows(a.shape[0], n_steps)
        last = a.shape[0] // rb - 1
        specs.append(pl.BlockSpec((rb, a.shape[1]), lambda *idx, last=last: (jnp.minimum(step_of(*idx), last), 0)))
    return specs


def _cast_blocks(cast_in, cast_out):
    for src, dst in zip(cast_in, cast_out):
        dst[...] = _bf(src[...])


def _in_proj(x2, g, w_t, cast, tm=2048, tn=W_IN_TILE):
    T, D = x2.shape
    N = w_t.shape[0]
    ni, nj = T // tm, N // tn
    cast_specs = _cast_specs(cast, ni * nj, lambda i, j: i * nj + j)
    outs = pl.pallas_call(
        functools.partial(_inproj_kernel, n_cast=len(cast), tm=tm),
        grid=(ni, nj),
        in_specs=[
            pl.BlockSpec(memory_space=pl.ANY),
            pl.BlockSpec((1, D), lambda i, j: (0, 0)),
            pl.BlockSpec((tn, D), lambda i, j: (j, 0)),
        ] + cast_specs,
        out_specs=[pl.BlockSpec((tm, tn), lambda i, j: (i, j))] + cast_specs,
        out_shape=[jax.ShapeDtypeStruct((T, N), F32)] + [jax.ShapeDtypeStruct(a.shape, BF16) for a in cast],
        scratch_shapes=[pltpu.VMEM((tm, D), F32), pltpu.VMEM((2, tm, D), BF16), pltpu.SemaphoreType.DMA(())],
        compiler_params=_params("arbitrary", "arbitrary"),
        name="in_proj",
    )(x2, g, w_t, *cast)
    return outs[0], outs[1:]


def _split3(x):
    hi = _bf(x)
    r1 = x - hi.astype(F32)
    mid = _bf(r1)
    lo = _bf(r1 - mid.astype(F32))
    return hi, mid, lo


def _rwkv_kernel(zr_ref, zk_ref, zv_ref, zl_ref, mur_ref, muk_ref, muv_ref, mul_ref,
                 w0_ref, a0_ref, kk_ref, ka_ref, rk_ref, lnw_ref, lnb_ref,
                 w2_ref, a2_ref, g2_ref, o_ref,
                 st_ref, pr_ref, pk_ref, pv_ref, plr_ref, *, units, chunks):
    L = RW_L
    UW = RW_UNIT
    c = pl.program_id(2)

    @pl.when(c == 0)
    def _():
        st_ref[...] = jnp.zeros_like(st_ref)
        pr_ref[...] = jnp.zeros_like(pr_ref)
        pk_ref[...] = jnp.zeros_like(pk_ref)
        pv_ref[...] = jnp.zeros_like(pv_ref)
        plr_ref[...] = jnp.zeros_like(plr_ref)

    def shift_lerp(z_ref, prev_ref, mu_ref):
        z = z_ref[...]
        rolled = pltpu.roll(z, 1, axis=0)
        row = lax.broadcasted_iota(jnp.int32, z.shape, 0)
        zs = jnp.where(row == 0, prev_ref[...], rolled)
        prev_ref[...] = z[z.shape[0] - 1:, :]
        return z + (zs - z) * mu_ref[...]

    r_all = shift_lerp(zr_ref, pr_ref, mur_ref)
    k_all = shift_lerp(zk_ref, pk_ref, muk_ref)
    v_all = shift_lerp(zv_ref, pv_ref, muv_ref)
    lo_all = shift_lerp(zl_ref, plr_ref, mul_ref)
    wd = jnp.tanh(lo_all[:, 0:64])
    ad = lo_all[:, 64:128]
    gd = jax.nn.sigmoid(lo_all[:, 128:384])

    hshift = RW_N.bit_length() - 1
    head_of_lane = lax.broadcasted_iota(jnp.int32, (1, UW), 1) >> hshift
    head_masks = [head_of_lane == j for j in range(RW_UNIT_HEADS)]
    row_l = lax.broadcasted_iota(jnp.int32, (L, UW), 0)
    col_l = lax.broadcasted_iota(jnp.int32, (L, UW), 1) & (RW_N - 1)
    strict = col_l < row_l
    incl = col_l <= row_l
    eye_pair = (col_l == row_l).astype(F32)
    row_s = lax.broadcasted_iota(jnp.int32, (UW, UW), 0)
    col_s = lax.broadcasted_iota(jnp.int32, (UW, UW), 1)
    same_head = (row_s >> hshift) == (col_s >> hshift)
    diag_s = row_s == col_s
    ones_bd = same_head.astype(BF16)

    def stack(x):
        return jnp.concatenate([jnp.where(m, x, 0.0) for m in head_masks], axis=0)

    def head_sum(x):
        return jnp.dot(_bf(x), ones_bd, preferred_element_type=F32)

    def cat(xs, axis):
        return jnp.concatenate(xs, axis=axis)

    U = range(units)
    uls = [slice(u * UW, (u + 1) * UW) for u in U]
    xw = [w0_ref[:, sl] + _mm(wd, w2_ref[:, sl]) for sl in uls]
    ga = [a0_ref[:, sl] + _mm(ad, a2_ref[:, sl]) for sl in uls]
    g_out = [_mm(gd, g2_ref[:, sl]) for sl in uls]
    kkv = [k_all[:, uls[u]] * kk_ref[:, uls[u]] for u in U]
    ss = [head_sum(x * x) for x in kkv]
    logw_u = [-0.6065306597126334 * jax.nn.sigmoid(x) for x in xw]
    gate_u = [jax.nn.sigmoid(x) for x in ga]
    kkn_u = [kkv[u] / jnp.maximum(jnp.sqrt(ss[u]), 1e-12) for u in U]
    k2_u = [k_all[:, uls[u]] * (1.0 + (gate_u[u] - 1.0) * ka_ref[:, uls[u]]) for u in U]
    bonus_in = [head_sum(r_all[:, uls[u]] * k2_u[u] * rk_ref[:, uls[u]]) for u in U]

    items = [(ci, u) for ci in range(chunks) for u in U]
    P = range(len(items))
    rows = [slice(ci * L, (ci + 1) * L) for ci, _ in items]
    r = [r_all[rows[p], uls[items[p][1]]] for p in P]
    v = [v_all[rows[p], uls[items[p][1]]] for p in P]
    logw = [logw_u[items[p][1]][rows[p]] for p in P]
    gate = [gate_u[items[p][1]][rows[p]] for p in P]
    kkn = [kkn_u[items[p][1]][rows[p]] for p in P]
    k2 = [k2_u[items[p][1]][rows[p]] for p in P]
    parts = [_split3(x) for x in logw]
    tri3 = ((lax.broadcasted_iota(jnp.int32, (L, 3 * L), 1) & (L - 1))
            <= lax.broadcasted_iota(jnp.int32, (L, 3 * L), 0)).astype(BF16)
    cum = [jnp.dot(tri3, cat(list(parts[p]), 0), preferred_element_type=F32) for p in P]
    cum_l = [c_[L - 1:L, :] for c_ in cum]
    e_neg = [jnp.exp(-c_) for c_ in cum]
    e_end = [jnp.exp(cum_l[p] - cum[p]) for p in P]
    kka = [kkn[p] * gate[p] for p in P]
    rt = [r[p] * jnp.exp(cum[p]) for p in P]
    kt = [k2[p] * e_neg[p] for p in P]
    bt = [kka[p] * e_neg[p] for p in P]
    at = [-kkn[p] * jnp.exp(cum[p] - logw[p]) for p in P]
    bk_t = [cat([kka[p] * e_end[p], k2[p] * e_end[p]], 0).T for p in P]
    w_end = [jnp.exp(c_) for c_ in cum_l]

    a_all = [_mm_nt(cat([at[p], rt[p]], 0), cat([stack(bt[p]), stack(kt[p])], 0)) for p in P]
    a_ab = [jnp.where(strict, a[:L, :UW], 0.0) for a in a_all]
    a_ak = [jnp.where(strict, a[:L, UW:], 0.0) for a in a_all]
    a_rb = [jnp.where(incl, a[L:, :UW], 0.0) for a in a_all]
    a_rk = [jnp.where(incl, a[L:, UW:], 0.0) for a in a_all]
    av = [_mm(a_ak[p], stack(v[p])) for p in P]

    tinv = [eye_pair + a for a in a_ab]
    pw = a_ab
    for _ in range(5):
        pw = [_mm(x, stack(x)) for x in pw]
        tinv = [tinv[p] + _mm(tinv[p], stack(pw[p])) for p in P]

    pq = [_mm(tinv[p], cat([stack(at[p]), stack(av[p])], 1)) for p in P]
    pm = [x[:, :UW] for x in pq]
    qm = [x[:, UW:] for x in pq]
    rm = [rt[p] + _mm(a_rb[p], stack(pm[p])) for p in P]
    y0 = [_mm(cat([a_rb[p], a_rk[p]], 1), cat([stack(qm[p]), stack(v[p])], 0)) for p in P]
    mn = [_mm(bk_t[p], cat([cat([pm[p], qm[p]], 1), cat([jnp.zeros_like(v[p]), v[p]], 1)], 0)) for p in P]
    m_mat = [jnp.where(diag_s, w_end[p], 0.0) + jnp.where(same_head, mn[p][:, :UW], 0.0) for p in P]
    n_mat = [jnp.where(same_head, mn[p][:, UW:], 0.0) for p in P]

    state = [st_ref[u] for u in U]
    y_chunks = []
    for ci in range(chunks):
        ps = [ci * units + u for u in U]
        ys = [_mm(cat([rm[p], m_mat[p]], 0), state[u]) for u, p in zip(U, ps)]
        state = [ys[u][L:] + n_mat[p] for u, p in zip(U, ps)]
        y_chunks.append([ys[u][:L] + y0[p] for u, p in zip(U, ps)])
    for u in U:
        st_ref[u] = state[u]

    y = [cat([y_chunks[ci][u] for ci in range(chunks)], 0) for u in U]
    mean = [head_sum(x) * (1.0 / RW_N) for x in y]
    yc = [y[u] - mean[u] for u in U]
    var = [head_sum(x * x) * (1.0 / RW_N) for x in yc]
    for u in U:
        yn = yc[u] * lax.rsqrt(var[u] + GN_EPS) * lnw_ref[:, uls[u]] + lnb_ref[:, uls[u]]
        o_ref[:, uls[u]] = _bf((yn + bonus_in[u] * v_all[:, uls[u]]) * g_out[u])


RW_N_INPUTS = 18
RW_N_SCRATCH = 5


def _rwkv_cast_kernel(*refs, n_cast, units, chunks):
    ins = refs[:RW_N_INPUTS]
    cast_in = refs[RW_N_INPUTS:RW_N_INPUTS + n_cast]
    o_ref = refs[RW_N_INPUTS + n_cast]
    cast_out = refs[RW_N_INPUTS + n_cast + 1:RW_N_INPUTS + 2 * n_cast + 1]
    scratch = refs[-RW_N_SCRATCH:]
    _rwkv_kernel(*ins, o_ref, *scratch, units=units, chunks=chunks)
    _cast_blocks(cast_in, cast_out)


def _rwkv(z_all, S, mu_r, mu_k, mu_v, mu_l, w0, a0, kk, ka, rk, lnw, lnb, w2, a2, g2, cast):
    T = z_all.shape[0]
    B = T // S
    R = RW_CHUNKS_PER_STEP * RW_L
    nc = S // R
    G = RW_UNITS_PER_STEP
    W = G * RW_UNIT
    ngrp = RW_WIDTH // W
    zspec = lambda off: pl.BlockSpec((R, W), lambda b, g, c, off=off: (b * nc + c, off * ngrp + g))
    vspec = pl.BlockSpec((1, W), lambda b, g, c: (0, g))
    lora_col = (z_all.shape[1] - 512) // 512
    cast_specs = _cast_specs(cast, B * ngrp * nc, lambda b, g, c: (b * ngrp + g) * nc + c)
    outs = pl.pallas_call(
        functools.partial(_rwkv_cast_kernel, n_cast=len(cast), units=G, chunks=RW_CHUNKS_PER_STEP),
        grid=(B, ngrp, nc),
        in_specs=[zspec(0), zspec(1), zspec(2),
                  pl.BlockSpec((R, 512), lambda b, g, c: (b * nc + c, lora_col)),
                  vspec, vspec, vspec,
                  pl.BlockSpec((1, 512), lambda b, g, c: (0, 0)),
                  vspec, vspec, vspec, vspec, vspec, vspec, vspec,
                  pl.BlockSpec((64, W), lambda b, g, c: (0, g)),
                  pl.BlockSpec((64, W), lambda b, g, c: (0, g)),
                  pl.BlockSpec((256, W), lambda b, g, c: (0, g))] + cast_specs,
        out_specs=[pl.BlockSpec((R, W), lambda b, g, c: (b * nc + c, g))] + cast_specs,
        out_shape=[jax.ShapeDtypeStruct((T, RW_WIDTH), BF16)] + [jax.ShapeDtypeStruct(a.shape, BF16) for a in cast],
        scratch_shapes=[pltpu.VMEM((G, RW_UNIT, RW_UNIT), F32),
                        pltpu.VMEM((1, W), F32), pltpu.VMEM((1, W), F32), pltpu.VMEM((1, W), F32),
                        pltpu.VMEM((1, 512), F32)],
        compiler_params=_params("arbitrary", "arbitrary", "arbitrary"),
        name="rwkv",
    )(z_all, z_all, z_all, z_all, mu_r, mu_k, mu_v, mu_l, w0, a0, kk, ka, rk, lnw, lnb, w2, a2, g2, *cast)
    return outs[0], outs[1:]


def _mla_proj_kernel(cq_ref, ckv_ref, kpe_ref, cos_ref, sin_ref, qn_ref, kvn_ref,
                     wqn_ref, wqp_ref, wkv_ref, q_ref, kv_ref, kr_ref, *, scale):
    cqn = _bf(_rms(cq_ref[...], qn_ref[...]))
    ckvn = _bf(_rms(ckv_ref[...], kvn_ref[...]))
    cos = cos_ref[...]
    sin = sin_ref[...]
    q_nope = jnp.dot(cqn, wqn_ref[...], preferred_element_type=F32)
    q_pe = jnp.dot(cqn, wqp_ref[...], preferred_element_type=F32)
    half = MLA_ROPE // 2
    first_half = lax.broadcasted_iota(jnp.int32, (1, LANES), 1) < half
    for h in range(MLA_HEADS):
        hs = slice(h * LANES, (h + 1) * LANES)
        pe = q_pe[:, hs]
        swapped = jnp.where(first_half, pltpu.roll(pe, LANES - half, axis=1), pltpu.roll(pe, half, axis=1))
        q_ref[:, 2 * h * LANES:(2 * h + 1) * LANES] = _bf(q_nope[:, hs] * scale)
        q_ref[:, (2 * h + 1) * LANES:(2 * h + 2) * LANES] = _bf((pe * cos + swapped * sin) * scale)
    kv_ref[...] = _bf(jnp.dot(ckvn, wkv_ref[...], preferred_element_type=F32))
    blk = kpe_ref[...]
    kr_ref[...] = _bf(blk * cos + pltpu.roll(blk, MLA_ROPE, axis=1) * sin)


def _mla_proj(z_all, cos_t, sin_t, qn, kvn, wqn, wqp, wkv, tm=512):
    T = z_all.shape[0]
    HW = MLA_HEADS * 2 * LANES
    scale = float((MLA_NOPE + MLA_ROPE) ** -0.5 * 1.4426950408889634)
    full = lambda a: pl.BlockSpec(a.shape, lambda i: (0, 0))
    kpe_blk = (z_all.shape[1] - LANES) // LANES
    return pl.pallas_call(
        functools.partial(_mla_proj_kernel, scale=scale),
        grid=(T // tm,),
        in_specs=[pl.BlockSpec((tm, MLA_RANK), lambda i: (i, 6)),
                  pl.BlockSpec((tm, MLA_RANK), lambda i: (i, 7)),
                  pl.BlockSpec((tm, LANES), lambda i: (i, kpe_blk)),
                  pl.BlockSpec((tm, LANES), lambda i: (i, 0)),
                  pl.BlockSpec((tm, LANES), lambda i: (i, 0)),
                  full(qn), full(kvn), full(wqn), full(wqp), full(wkv)],
        out_specs=[pl.BlockSpec((tm, HW), lambda i: (i, 0)),
                   pl.BlockSpec((tm, HW), lambda i: (i, 0)),
                   pl.BlockSpec((tm, LANES), lambda i: (i, 0))],
        out_shape=[jax.ShapeDtypeStruct((T, HW), BF16),
                   jax.ShapeDtypeStruct((T, HW), BF16),
                   jax.ShapeDtypeStruct((T, LANES), BF16)],
        compiler_params=_params("parallel"),
        name="mla_proj",
    )(z_all, z_all, z_all, cos_t, sin_t, qn, kvn, wqn, wqp, wkv)


ATT_HEADS_PER_STEP = 8


def _attn_kernel(qi_ref, ki_ref, q_ref, kv_ref, kr_ref, o_ref, m_ref, l_ref, acc_ref, *, tq, tk, heads):
    t = pl.program_id(2)
    qi = qi_ref[t]
    ki = ki_ref[t]
    H = range(heads)

    @pl.when(ki == 0)
    def _():
        m_ref[...] = jnp.full_like(m_ref, -1e30)
        l_ref[...] = jnp.zeros_like(l_ref)
        acc_ref[...] = jnp.zeros_like(acc_ref)

    def update(r0, nr, nk, masked):
        rows = slice(r0, r0 + nr)
        kr = kr_ref[0:nk, :]
        s = [lax.dot_general(q_ref[rows, 2 * h * LANES:(2 * h + 2) * LANES],
                             jnp.concatenate([kv_ref[0:nk, 2 * h * LANES:(2 * h + 1) * LANES], kr], axis=1),
                             (((1,), (1,)), ((), ())), preferred_element_type=F32) for h in H]
        if masked:
            q_last = (qi * tq + r0 + lax.broadcasted_iota(jnp.int32, (nr, nk), 0)) | (CHUNK - 1)
            k_pos = ki * tk + lax.broadcasted_iota(jnp.int32, (nr, nk), 1)
            vis = k_pos <= q_last
            s = [jnp.where(vis, x, -1e30) for x in s]
        m_prev = [m_ref[h, rows] for h in H]
        m_new = [jnp.maximum(m_prev[h], jnp.max(s[h], axis=-1, keepdims=True)) for h in H]
        p = [jnp.exp2(s[h] - jnp.concatenate([m_new[h]] * (nk // LANES), axis=1)) for h in H]
        alpha = [jnp.exp2(m_prev[h] - m_new[h]) for h in H]
        pv = [jnp.dot(_bf(p[h]), kv_ref[0:nk, (2 * h + 1) * LANES:(2 * h + 2) * LANES], preferred_element_type=F32) for h in H]
        for h in H:
            psum = sum(p[h][:, c:c + LANES] for c in range(0, nk, LANES))
            l_ref[h, rows] = alpha[h] * l_ref[h, rows] + psum
            acc_ref[h, rows] = alpha[h] * acc_ref[h, rows] + pv[h]
            m_ref[h, rows] = m_new[h]

    @pl.when(ki < qi)
    def _():
        update(0, tq, tk, False)

    @pl.when(ki == qi)
    def _():
        update(0, tq // 2, tk // 2, True)
        update(tq // 2, tq // 2, tk, True)
        for h in H:
            o_ref[:, h * LANES:(h + 1) * LANES] = _bf(acc_ref[h] / jnp.sum(l_ref[h], axis=-1, keepdims=True))


def _attention(q, kv, kr, S, tq=512):
    T = q.shape[0]
    B = T // S
    nq = S // tq
    G = ATT_HEADS_PER_STEP
    pairs = [(a, b) for a in range(nq) for b in range(a + 1)]
    qi_tab = jnp.asarray([a for a, _ in pairs], jnp.int32)
    ki_tab = jnp.asarray([b for _, b in pairs], jnp.int32)
    grid_spec = pltpu.PrefetchScalarGridSpec(
        num_scalar_prefetch=2,
        grid=(B, MLA_HEADS // G, len(pairs)),
        in_specs=[pl.BlockSpec((tq, 2 * G * LANES), lambda b, g, t, qt, kt: (b * nq + qt[t], g)),
                  pl.BlockSpec((tq, 2 * G * LANES), lambda b, g, t, qt, kt: (b * nq + kt[t], g)),
                  pl.BlockSpec((tq, LANES), lambda b, g, t, qt, kt: (b * nq + kt[t], 0))],
        out_specs=pl.BlockSpec((tq, G * LANES), lambda b, g, t, qt, kt: (b * nq + qt[t], g)),
        scratch_shapes=[pltpu.VMEM((G, tq, LANES), F32), pltpu.VMEM((G, tq, LANES), F32), pltpu.VMEM((G, tq, LANES), F32)])
    return pl.pallas_call(
        functools.partial(_attn_kernel, tq=tq, tk=tq, heads=G),
        grid_spec=grid_spec,
        out_shape=jax.ShapeDtypeStruct((T, MLA_HEADS * MLA_V), BF16),
        compiler_params=_params("parallel", "parallel", "arbitrary"),
        name="attention",
    )(qi_tab, ki_tab, q, kv, kr)


def _mix_kernel(yrw_ref, ymla_ref, gate_ref, x_ref, wa_ref, wb_ref, wo_ref, g_ref, o_ref):
    D = x_ref.shape[1]
    t = (jax.nn.sigmoid(gate_ref[:, :D]) * jnp.dot(yrw_ref[...], wa_ref[...], preferred_element_type=F32)
         + jax.nn.sigmoid(gate_ref[:, D:]) * jnp.dot(ymla_ref[...], wb_ref[...], preferred_element_type=F32))
    mix = jnp.dot(_bf(t), wo_ref[...], preferred_element_type=F32)
    o_ref[...] = x_ref[...] + _rms(mix, g_ref[...])


def _resident(a):
    return pl.BlockSpec(a.shape, lambda *_: (0,) * a.ndim, pipeline_mode=pl.Buffered(1))


def _mix(y_rw, y_mla, z_all, x2, wa, wb, wo, g, tm=256):
    T, D = x2.shape
    return pl.pallas_call(
        _mix_kernel,
        grid=(T // tm,),
        in_specs=[pl.BlockSpec((tm, y_rw.shape[1]), lambda i: (i, 0)),
                  pl.BlockSpec((tm, y_mla.shape[1]), lambda i: (i, 0)),
                  pl.BlockSpec((tm, 2 * D), lambda i: (i, 1)),
                  pl.BlockSpec((tm, D), lambda i: (i, 0)),
                  _resident(wa), _resident(wb), _resident(wo), _resident(g)],
        out_specs=pl.BlockSpec((tm, D), lambda i: (i, 0)),
        out_shape=jax.ShapeDtypeStruct((T, D), F32),
        compiler_params=_params("parallel"),
        name="mix",
    )(y_rw, y_mla, z_all, x2, wa, wb, wo, g)


HALO = 16
FFN_SUB = 256
FFN_ROW_BLOCKS = (256, 256, 256, 256)


def _gelu_tanh(x):
    return 0.5 * x * (1.0 + jnp.tanh(0.7978845608028654 * (x + 0.044715 * x * x * x)))


def _ffn_kernel(h_ref, halo_ref, gin_ref, wug_ref, wuv_ref, cwg_ref, cwv_ref, cbg_ref, cbv_ref,
                wd_ref, gout_ref, o_ref, xn_ref, upg_ref, upv_ref, *, tm, tiles_per_seq):
    i = pl.program_id(0)
    j = pl.program_id(1)

    @pl.when(j == 0)
    def _():
        xn_ref[0:HALO, :] = _bf(_rms(halo_ref[...], gin_ref[...]))
        xn_ref[HALO:, :] = _bf(_rms(h_ref[...], gin_ref[...]))
        o_ref[...] = jnp.zeros_like(o_ref)

    keep = jnp.where((i % tiles_per_seq) == 0, 0.0, 1.0).astype(F32)
    tf = upg_ref.shape[1]
    subs = [slice(c, c + FFN_SUB) for c in range(0, tf, FFN_SUB)]

    for w_ref, up_ref in ((wug_ref, upg_ref), (wuv_ref, upv_ref)):
        for cs in subs:
            up = jnp.dot(xn_ref[...], w_ref[:, cs], preferred_element_type=F32)
            up_ref[0:HALO, cs] = up[0:HALO] * keep
            up_ref[HALO:, cs] = up[HALO:]

    def conv(up_ref, cw_ref, cb_ref, cs, r0, nr):
        cw = cw_ref[:, cs]
        return (cw[0:1, :] * up_ref[HALO - 2 + r0:HALO - 2 + r0 + nr, cs]
                + cw[1:2, :] * up_ref[HALO - 1 + r0:HALO - 1 + r0 + nr, cs]
                + cw[2:3, :] * up_ref[HALO + r0:HALO + r0 + nr, cs] + cb_ref[:, cs])

    def act(cs, r0, nr):
        return _bf(_gelu_tanh(conv(upg_ref, cwg_ref, cbg_ref, cs, r0, nr)) * conv(upv_ref, cwv_ref, cbv_ref, cs, r0, nr))

    head = [act(cs, 0, tm) for cs in subs[:-1]]
    r0 = 0
    for nr in FFN_ROW_BLOCKS:
        a = jnp.concatenate([x[r0:r0 + nr] for x in head] + [act(subs[-1], r0, nr)], axis=1)
        o_ref[r0:r0 + nr, :] += jnp.dot(a, wd_ref[...], preferred_element_type=F32)
        r0 += nr
    assert r0 == tm

    @pl.when(j == pl.num_programs(1) - 1)
    def _():
        o_ref[...] = h_ref[...] + _rms(o_ref[...], gout_ref[...])


def _ffn(h1, S, gin, w_up, conv_w, conv_b, w_down, gout, tm=1024, tf=512):
    T, D = h1.shape
    F = w_down.shape[0]
    nf = F // tf
    hb = tm // HALO
    return pl.pallas_call(
        functools.partial(_ffn_kernel, tm=tm, tiles_per_seq=S // tm),
        grid=(T // tm, nf),
        in_specs=[pl.BlockSpec((tm, D), lambda i, j: (i, 0), pipeline_mode=pl.Buffered(1)),
                  pl.BlockSpec((HALO, D), lambda i, j: (jnp.maximum(i * hb - 1, 0), 0)),
                  pl.BlockSpec((1, D), lambda i, j: (0, 0)),
                  pl.BlockSpec((D, tf), lambda i, j: (0, j)),
                  pl.BlockSpec((D, tf), lambda i, j: (0, nf + j)),
                  pl.BlockSpec((3, tf), lambda i, j: (0, j)),
                  pl.BlockSpec((3, tf), lambda i, j: (0, nf + j)),
                  pl.BlockSpec((1, tf), lambda i, j: (0, j)),
                  pl.BlockSpec((1, tf), lambda i, j: (0, nf + j)),
                  pl.BlockSpec((tf, D), lambda i, j: (j, 0)),
                  pl.BlockSpec((1, D), lambda i, j: (0, 0))],
        out_specs=pl.BlockSpec((tm, D), lambda i, j: (i, 0)),
        out_shape=jax.ShapeDtypeStruct((T, D), F32),
        scratch_shapes=[pltpu.VMEM((tm + HALO, D), BF16),
                        pltpu.VMEM((tm + HALO, tf), F32), pltpu.VMEM((tm + HALO, tf), F32)],
        compiler_params=_params("parallel", "arbitrary"),
        name="ffn",
    )(h1, h1, gin, w_up, w_up, conv_w, conv_w, conv_b, conv_b, w_down, gout)


def _ple_kernel(h_ref, p_ref, wpg_ref, wple_ref, g_ref, o_ref):
    h = h_ref[...]
    gate = jax.nn.sigmoid(jnp.dot(_bf(h), wpg_ref[...], preferred_element_type=F32))
    e = jnp.dot(_bf(p_ref[...]), wple_ref[...], preferred_element_type=F32)
    o_ref[...] = h + _rms(gate * e, g_ref[...])


def _ple(h2, p2, wpg, wple, g, tm=512):
    T, D = h2.shape
    return pl.pallas_call(
        _ple_kernel,
        grid=(T // tm,),
        in_specs=[pl.BlockSpec((tm, D), lambda i: (i, 0)),
                  pl.BlockSpec((tm, p2.shape[1]), lambda i: (i, 0)),
                  _resident(wpg), _resident(wple), _resident(g)],
        out_specs=pl.BlockSpec((tm, D), lambda i: (i, 0)),
        out_shape=jax.ShapeDtypeStruct((T, D), F32),
        compiler_params=_params("parallel"),
        name="ple",
    )(h2, p2, wpg, wple, g)


def _pack_w_in_kernel(off_ref, w_ref, kpe_ref, o_ref):
    j = pl.program_id(0)
    last = pl.num_programs(0) - 1

    @pl.when(j < last)
    def _():
        o_ref[...] = _bf(w_ref[...])

    @pl.when(j == last)
    def _():
        kpe = kpe_ref[...]
        half = MLA_ROPE // 2
        pad = jnp.zeros((W_IN_TILE - RW_LORA - 2 * MLA_ROPE, kpe.shape[1]), F32)
        o_ref[...] = _bf(jnp.concatenate([w_ref[0:RW_LORA, :], pad, kpe, kpe[half:], kpe[:half]], axis=0))


def _pack_w_in(w_in_t):
    N, D = w_in_t.shape
    rkv_end = 3 * RW_WIDTH
    m0 = rkv_end + RW_LORA
    g0 = m0 + 2 * MLA_RANK + MLA_ROPE
    starts = (list(range(0, rkv_end, W_IN_TILE)) + [m0, m0 + MLA_RANK]
              + list(range(g0, g0 + 4096, W_IN_TILE)) + [rkv_end])
    assert len(starts) * W_IN_TILE == W_IN_PACKED and N == g0 + 4096
    assert all(s % W_IN_ALIGN == 0 for s in starts)
    starts = [s // W_IN_ALIGN for s in starts]
    grid_spec = pltpu.PrefetchScalarGridSpec(
        num_scalar_prefetch=1,
        grid=(len(starts),),
        in_specs=[pl.BlockSpec((pl.Element(W_IN_TILE), pl.Element(D)),
                               lambda j, off: (pl.multiple_of(off[j] * W_IN_ALIGN, W_IN_ALIGN), 0)),
                  pl.BlockSpec((pl.Element(MLA_ROPE), pl.Element(D)), lambda j, off: (m0 + 2 * MLA_RANK, 0))],
        out_specs=pl.BlockSpec((W_IN_TILE, D), lambda j, off: (j, 0)))
    return pl.pallas_call(
        _pack_w_in_kernel,
        grid_spec=grid_spec,
        out_shape=jax.ShapeDtypeStruct((W_IN_PACKED, D), BF16),
        compiler_params=_params("arbitrary"),
        name="pack_w_in",
    )(jnp.asarray(starts, jnp.int32), w_in_t, w_in_t)


def _row(v):
    return v.reshape(1, -1).astype(F32)


def kernel(x, p, positions, pre_mix_norm, w_in, rw_mu, rw_w0, rw_w2, rw_a0, rw_a2, rw_g2, rw_k_k, rw_k_a, rw_r_k, rw_lnx_w, rw_lnx_b, mla_q_norm, mla_w_q_up, mla_kv_norm, mla_w_kv_up, w_branch_rw, w_branch_mla, w_out, post_mix_norm, pre_ffn_norm, w_up, conv_w, conv_b, w_down, post_ffn_norm, w_ple, w_ple_gate, ple_norm):
    B, S, D = x.shape
    T = B * S
    depth = w_in.shape[0]

    inv_freq = ROPE_BASE ** (-jnp.arange(0, MLA_ROPE, 2, dtype=F32) / MLA_ROPE)
    ang = positions.astype(F32)[..., None] * inv_freq
    cos = jnp.cos(ang).reshape(T, MLA_ROPE // 2)
    sin = jnp.sin(ang).reshape(T, MLA_ROPE // 2)
    zpad = jnp.zeros((T, LANES - MLA_ROPE), F32)
    cos_t = jnp.concatenate([cos, cos, zpad], axis=1)
    sin_t = jnp.concatenate([-sin, sin, zpad], axis=1)

    h = x.reshape(T, D)
    for i in range(depth):
        w_in_p = _pack_w_in(jnp.swapaxes(w_in, 1, 2)[i])
        z_all, (w_kv_b, w_rw_b, w_mla_b, w_out_b, w_pg_b, w_ple_b) = _in_proj(
            h, _row(pre_mix_norm[i]), w_in_p,
            [mla_w_kv_up[i], w_branch_rw[i], w_branch_mla[i], w_out[i], w_ple_gate[i], w_ple[i]])

        mu = rw_mu[i]
        mu_l = jnp.concatenate([mu[3 * RW_WIDTH:], jnp.zeros((W_IN_TILE - RW_LORA,), F32)]).reshape(1, W_IN_TILE)
        g2p = jnp.concatenate([rw_g2[i], jnp.zeros((256 - rw_g2.shape[1], RW_WIDTH), F32)], axis=0)
        y_rw, (w_up_b, w_down_b) = _rwkv(
            z_all, S, _row(mu[:RW_WIDTH]), _row(mu[RW_WIDTH:2 * RW_WIDTH]), _row(mu[2 * RW_WIDTH:3 * RW_WIDTH]),
            mu_l, _row(rw_w0[i]), _row(rw_a0[i]), _row(rw_k_k[i]), _row(rw_k_a[i]), _row(rw_r_k[i]),
            _row(rw_lnx_w[i]), _row(rw_lnx_b[i]), _bf(rw_w2[i]), _bf(rw_a2[i]), _bf(g2p), [w_up[i], w_down[i]])

        wq = mla_w_q_up[i].reshape(MLA_RANK, MLA_HEADS, MLA_NOPE + MLA_ROPE)
        wqn = wq[:, :, :MLA_NOPE].reshape(MLA_RANK, MLA_HEADS * MLA_NOPE)
        wq_pe = wq[:, :, MLA_NOPE:]
        padh = jnp.zeros((MLA_RANK, MLA_HEADS, LANES - MLA_ROPE), F32)
        wqp = jnp.concatenate([wq_pe, padh], axis=-1).reshape(MLA_RANK, MLA_HEADS * LANES)
        q, kv, kr = _mla_proj(z_all, cos_t, sin_t, _row(mla_q_norm[i]), _row(mla_kv_norm[i]),
                              _bf(wqn), _bf(wqp), w_kv_b)
        y_mla = _attention(q, kv, kr, S)

        h = _mix(y_rw, y_mla, z_all, h, w_rw_b, w_mla_b, w_out_b, _row(post_mix_norm[i]))
        h = _ffn(h, S, _row(pre_ffn_norm[i]), w_up_b, conv_w[i], _row(conv_b[i]), w_down_b, _row(post_ffn_norm[i]))
        h = _ple(h, p[i].reshape(T, -1), w_pg_b, w_ple_b, _row(ple_norm[i]))
    return h.reshape(B, S, D)
```
